```python
import math
import jax
import jax.numpy as jnp
from jax import lax


D_MODEL = 1024
BATCH = 8
SEQ = 2048
DEPTH = 4

F32 = jnp.float32
GRID_W = 64
CTX_LEN = 256
N_EVEN = (DEPTH + 1) // 2
N_ODD = DEPTH // 2
DEEPNORM_ALPHA = (2.0 * DEPTH) ** 0.25
DEEPNORM_BETA = (8.0 * DEPTH) ** -0.25
N_MOD = 6
EPS = 1e-6

GDN_HEADS = 4
GDN_DK = 128
GDN_DV = 128
GDN_CONV = 5
GDN_CHUNK = 64
MLA_HEADS = 4
MLA_Q_LORA = 256
MLA_KV_LORA = 256
MLA_NOPE = 128
MLA_ROPE = 64
MLA_DV = 128
Q_BLOCK = 128
ROPE_THETA = 10000.0
NA_HEADS = 8
NA_DH = 64
NA_KH = 8
NA_KW = 16
HG_HEADS = 4
HG_DK = 128
HG_DV = 128
HG_CHUNK = 64
N_EXPERTS = 16
EXPERT_FF = 1024
EC_CAPACITY = 2

GDN_QKV = GDN_HEADS * (2 * GDN_DK + GDN_DV)
EVEN_SPLITS = (GDN_QKV, GDN_HEADS * GDN_DV, 4 * GDN_HEADS, MLA_Q_LORA, MLA_KV_LORA, MLA_ROPE)
EVEN_IN = GDN_QKV + GDN_HEADS * GDN_DV + 4 * GDN_HEADS + MLA_Q_LORA + MLA_KV_LORA + MLA_ROPE
EVEN_MIX = GDN_HEADS * GDN_DV + MLA_HEADS * MLA_DV
NA_QKV = 3 * NA_HEADS * NA_DH
ODD_SPLITS = (NA_QKV, HG_HEADS * HG_DK, HG_HEADS * HG_DK, HG_HEADS * HG_DK, HG_HEADS * HG_DV, HG_HEADS * HG_DV)
ODD_IN = NA_QKV + 3 * HG_HEADS * HG_DK + 2 * HG_HEADS * HG_DV
ODD_MIX = NA_HEADS * NA_DH + HG_HEADS * HG_DV

kernel_name = 'hybrid_diffusion_gdn_mla_na_hgrn2_ec'


def _split(t, sizes):
    out, start = [], 0
    for s in sizes:
        out.append(t[..., start:start + s])
        start += s
    return out


def _heads(t, n_heads):
    B, L, W = t.shape
    return jnp.swapaxes(t.reshape(B, L, n_heads, W // n_heads), 1, 2)


def _flat(t):
    B, L, H, d = t.shape
    return t.reshape(B, L, H * d)


def _rms(x):
    xf = x.astype(F32)
    return (xf * lax.rsqrt(jnp.mean(xf * xf, -1, keepdims=True) + EPS)).astype(x.dtype)


def _l2norm(t):
    tf = t.astype(F32)
    return (tf * lax.rsqrt(jnp.sum(tf * tf, -1, keepdims=True) + EPS)).astype(t.dtype)


def _layer_norm(x, g, b):
    xf = x.astype(F32)
    xc = xf - jnp.mean(xf, -1, keepdims=True)
    y = xc * lax.rsqrt(jnp.mean(xc * xc, -1, keepdims=True) + EPS)
    return y.astype(x.dtype) * g + b


def _gated_rms(o, gain, z):
    B, H, L, d = o.shape
    o = jnp.swapaxes(o, 1, 2)
    return (_rms(o) * gain * jax.nn.silu(z.reshape(B, L, H, d))).reshape(B, L, H * d)


def _short_conv(x, w):
    K, C = w.shape
    pad = K // 2
    return lax.conv_general_dilated(x, w[:, None, :], window_strides=(1,), padding=[(pad, pad)],
                                    dimension_numbers=('NWC', 'WIO', 'NWC'), feature_group_count=C)


def _rope_axis(x, pos):
    half = x.shape[-1] // 2
    inv = ROPE_THETA ** (-jnp.arange(half, dtype=F32) / half)
    ang = pos.astype(F32)[:, None] * inv
    shape = (pos.shape[0],) + (1,) * (x.ndim - 3) + (half,)
    cos, sin = jnp.cos(ang).reshape(shape), jnp.sin(ang).reshape(shape)
    xf = x.astype(F32)
    x1, x2 = xf[..., :half], xf[..., half:]
    return jnp.concatenate([x1 * cos - x2 * sin, x1 * sin + x2 * cos], -1).astype(x.dtype)


def _rope2d(x):
    pos = jnp.arange(x.shape[1])
    half = x.shape[-1] // 2
    return jnp.concatenate([_rope_axis(x[..., :half], pos // GRID_W),
                            _rope_axis(x[..., half:], pos % GRID_W)], -1)


def _softmax_attend(q, k, v, scale):
    s = jnp.einsum('bqhd,bkhd->bhqk', q, k).astype(F32) * scale
    p = jax.nn.softmax(s, axis=-1).astype(v.dtype)
    return jnp.einsum('bhqk,bkhe->bqhe', p, v)


def _blocked_attend(q, k, v, scale):
    B, S, H, d = q.shape
    nb = S // Q_BLOCK
    qb = jnp.moveaxis(q.reshape(B, nb, Q_BLOCK, H, d), 1, 0)
    o = lax.map(lambda qi: _softmax_attend(qi, k, v, scale), qb)
    return jnp.moveaxis(o, 0, 1).reshape(B, S, H, v.shape[-1])


def _gated_delta_chunked(q, k, v, g, beta):
    out_dtype = v.dtype
    q, k, v, g, beta = (t.astype(F32) for t in (q, k, v, g, beta))
    B, H, L, dk = q.shape
    dv = v.shape[-1]
    C = GDN_CHUNK
    N = L // C
    q = (q * dk ** -0.5).reshape(B, H, N, C, dk)
    k = k.reshape(B, H, N, C, dk)
    v = v.reshape(B, H, N, C, dv)
    g = jnp.cumsum(g.reshape(B, H, N, C), axis=-1)
    beta = beta.reshape(B, H, N, C)
    incl = jnp.tril(jnp.ones((C, C), bool))
    strict = jnp.tril(jnp.ones((C, C), bool), -1)
    decay = jnp.exp(jnp.where(incl, g[..., :, None] - g[..., None, :], -jnp.inf))
    k_beta = k * beta[..., None]
    a = jnp.where(strict, jnp.einsum('bhntd,bhnsd->bhnts', k_beta, k) * decay, 0.0)
    rhs = jnp.concatenate([v * beta[..., None], k_beta * jnp.exp(g)[..., None]], -1)
    sol = lax.linalg.triangular_solve(a + jnp.eye(C, dtype=F32), rhs, left_side=True, lower=True,
                                      unit_diagonal=True)
    u, w = sol[..., :dv], sol[..., dv:]
    qk = jnp.einsum('bhntd,bhnsd->bhnts', q, k) * decay
    xs = tuple(jnp.moveaxis(t, 2, 0) for t in (q, k, u, w, g, qk))

    def step(S, inp):
        q_i, k_i, u_i, w_i, g_i, qk_i = inp
        v_new = u_i - jnp.einsum('bhtd,bhde->bhte', w_i, S)
        o = (jnp.einsum('bhtd,bhde->bhte', q_i * jnp.exp(g_i)[..., None], S)
             + jnp.einsum('bhts,bhse->bhte', qk_i, v_new))
        g_last = g_i[..., -1:]
        S = (S * jnp.exp(g_last)[..., None]
             + jnp.einsum('bhsd,bhse->bhde', k_i * jnp.exp(g_last - g_i)[..., None], v_new))
        return S, o

    _, o = lax.scan(step, jnp.zeros((B, H, dk, dv), F32), xs)
    return jnp.moveaxis(o, 0, 2).reshape(B, H, L, dv).astype(out_dtype)


def _gla_chunked(q, v, k, logf):
    out_dtype = v.dtype
    q, v, k, logf = (t.astype(F32) for t in (q, v, k, logf))
    B, H, L, dk = q.shape
    dv = v.shape[-1]
    C = HG_CHUNK
    N = L // C
    G = jnp.cumsum(logf.reshape(B, H, N, C, dk), axis=3)
    xs = tuple(jnp.moveaxis(t, 2, 0) for t in (q.reshape(B, H, N, C, dk), k.reshape(B, H, N, C, dk),
                                                v.reshape(B, H, N, C, dv), G))
    tri = jnp.tril(jnp.ones((C, C), bool))[:, :, None]

    def step(S, inp):
        q_i, k_i, v_i, G_i = inp
        dec = jnp.exp(jnp.where(tri, G_i[:, :, :, None, :] - G_i[:, :, None, :, :], -jnp.inf))
        A = jnp.einsum('bhtd,bhsd,bhtsd->bhts', q_i, k_i, dec)
        o = (jnp.einsum('bhtd,bhde->bhte', q_i * jnp.exp(G_i), S)
             + jnp.einsum('bhts,bhse->bhte', A, v_i))
        G_last = G_i[:, :, -1]
        S = (jnp.exp(G_last)[..., None] * S
             + jnp.einsum('bhsd,bhse->bhde', k_i * jnp.exp(G_last[:, :, None] - G_i), v_i))
        return S, o

    _, o = lax.scan(step, jnp.zeros((B, H, dk, dv), F32), xs)
    return jnp.moveaxis(o, 0, 2).reshape(B, H, L, dv).astype(out_dtype)


def _bidirectional(scan_fn, shared_ctx, shared_lat, dirs_ctx, dirs_lat, need_ctx):
    T = shared_ctx[0].shape[2]
    cat = lambda a, b: jnp.concatenate([a, b], axis=2)
    rev = lambda t: jnp.flip(t, axis=2)
    fwd = scan_fn(*[cat(a, b) for a, b in zip(shared_ctx + dirs_ctx[0], shared_lat + dirs_lat[0])])
    bwd = scan_fn(*[cat(rev(a), rev(b)) for a, b in zip(shared_ctx + dirs_ctx[1], shared_lat + dirs_lat[1])])
    o_lat = fwd[:, :, T:] + rev(bwd[:, :, T:])
    o_ctx = fwd[:, :, :T] + rev(bwd[:, :, :T]) if need_ctx else None
    return o_ctx, o_lat


def _neighbourhood_attend(q, k, v, k_ctx, v_ctx, rpb, scale):
    B, S, H, d = q.shape
    rows = S // GRID_W
    kh, kw = min(NA_KH, rows), NA_KW
    grid = lambda t: t.reshape(B, rows, GRID_W, H, d)
    qg, kg, vg = grid(q), grid(k), grid(v)
    r = jnp.arange(rows)
    row_idx = jnp.clip(r - kh // 2, 0, rows - kh)[:, None] + jnp.arange(kh)[None, :]
    w = jnp.arange(GRID_W)
    c0 = jnp.clip(w - kw // 2, 0, GRID_W - kw)
    col_ok = (w[None, :] >= c0[:, None]) & (w[None, :] < c0[:, None] + kw)
    k_band, v_band = kg[:, row_idx], vg[:, row_idx]
    s_win = jnp.einsum('brwhd,brkchd->bhrwkc', qg, k_band).astype(F32) * scale
    row_off = row_idx - r[:, None] + (NA_KH - 1)
    col_off = jnp.clip(w[None, :] - w[:, None] + (NA_KW - 1), 0, 2 * NA_KW - 2)
    bias = rpb[:, row_off[:, None, :, None], col_off[None, :, None, :]].astype(F32)
    s_win = jnp.where(col_ok[:, None, :], s_win + bias, -jnp.inf)
    s_ctx = jnp.einsum('brwhd,bthd->bhrwt', qg, k_ctx).astype(F32) * scale
    n_win = kh * GRID_W
    p = jax.nn.softmax(jnp.concatenate([s_win.reshape(B, H, rows, GRID_W, n_win), s_ctx], -1), axis=-1)
    p = p.astype(v.dtype)
    p_win = p[..., :n_win].reshape(B, H, rows, GRID_W, kh, GRID_W)
    o = (jnp.einsum('bhrwkc,brkchd->brwhd', p_win, v_band)
         + jnp.einsum('bhrwt,bthd->brwhd', p[..., n_win:], v_ctx))
    return o.reshape(B, S, H * d)


def _expert_choice_ffn(h, w_router, w_gate, w_up, w_down):
    B, N, _ = h.shape
    cap = EC_CAPACITY * N // N_EXPERTS
    aff = jax.nn.softmax(jnp.einsum('bnd,de->bne', h, w_router).astype(F32), axis=-1)
    gate, idx = lax.top_k(jnp.swapaxes(aff, 1, 2), cap)
    bidx = jnp.arange(B)[:, None, None]
    xe = h[bidx, idx]
    hid = jax.nn.silu(jnp.einsum('becd,edf->becf', xe, w_gate)) * jnp.einsum('becd,edf->becf', xe, w_up)
    ye = jnp.einsum('becf,efd->becd', hid, w_down) * gate[..., None].astype(h.dtype)
    return jnp.zeros_like(h).at[bidx, idx].add(ye)


def _even_mixer(u_lat, u_ctx, w_in, conv_w, a_log, dt_bias, gdn_gain, q_gain, kv_gain, w_uq, w_ukv,
                w_out, need_ctx):
    B = u_lat.shape[0]
    p_lat = _split(u_lat @ w_in, EVEN_SPLITS)
    p_ctx = _split(u_ctx @ w_in, EVEN_SPLITS)

    def gdn_prep(p):
        L = p[0].shape[1]
        qkv = jax.nn.silu(_short_conv(p[0], conv_w))
        q, k, v = _split(qkv, (GDN_HEADS * GDN_DK, GDN_HEADS * GDN_DK, GDN_HEADS * GDN_DV))
        q = _l2norm(_heads(q, GDN_HEADS))
        k = _l2norm(_heads(k, GDN_HEADS))
        v = _heads(v, GDN_HEADS)
        ab = jnp.transpose(p[2].astype(F32).reshape(B, L, 2, 2, GDN_HEADS), (2, 3, 0, 4, 1))
        dirs = tuple(
            (-jnp.exp(a_log[d].astype(F32))[:, None] * jax.nn.softplus(ab[0, d] + dt_bias[d].astype(F32)[:, None]),
             jax.nn.sigmoid(ab[1, d]))
            for d in range(2))
        return (q, k, v), dirs

    sh_c, dir_c = gdn_prep(p_ctx)
    sh_l, dir_l = gdn_prep(p_lat)
    gdn_ctx, gdn_lat = _bidirectional(_gated_delta_chunked, sh_c, sh_l, dir_c, dir_l, need_ctx)
    gdn_lat = _gated_rms(gdn_lat, gdn_gain, p_lat[1])

    mla_scale = (MLA_NOPE + MLA_ROPE) ** -0.5

    def mla_kv(p, rotate):
        L = p[4].shape[1]
        kv = ((_rms(p[4]) * kv_gain) @ w_ukv).reshape(B, L, MLA_HEADS, MLA_NOPE + MLA_DV)
        k_rope = _rope2d(p[5]) if rotate else p[5]
        k = jnp.concatenate([kv[..., :MLA_NOPE],
                             jnp.broadcast_to(k_rope[:, :, None, :], (B, L, MLA_HEADS, MLA_ROPE))], -1)
        return k, kv[..., MLA_NOPE:]

    def mla_q(p, rotate):
        L = p[3].shape[1]
        q = ((_rms(p[3]) * q_gain) @ w_uq).reshape(B, L, MLA_HEADS, MLA_NOPE + MLA_ROPE)
        if rotate:
            q = jnp.concatenate([q[..., :MLA_NOPE], _rope2d(q[..., MLA_NOPE:])], -1)
        return q

    k_ctx, v_ctx = mla_kv(p_ctx, False)
    k_lat, v_lat = mla_kv(p_lat, True)
    mla_lat = _blocked_attend(mla_q(p_lat, True), jnp.concatenate([k_ctx, k_lat], 1),
                              jnp.concatenate([v_ctx, v_lat], 1), mla_scale)
    out_lat = jnp.concatenate([gdn_lat, _flat(mla_lat)], -1) @ w_out
    if not need_ctx:
        return out_lat, None
    mla_ctx = _softmax_attend(mla_q(p_ctx, False), k_ctx, v_ctx, mla_scale)
    out_ctx = jnp.concatenate([_gated_rms(gdn_ctx, gdn_gain, p_ctx[1]), _flat(mla_ctx)], -1) @ w_out
    return out_lat, out_ctx


def _odd_mixer(u_lat, u_ctx, w_in, rpb, lb, hg_gain, w_out, need_ctx):
    p_lat = _split(u_lat @ w_in, ODD_SPLITS)
    p_ctx = _split(u_ctx @ w_in, ODD_SPLITS)

    na_scale = NA_DH ** -0.5

    def na_qkv(p):
        B, L, _ = p[0].shape
        return [t.reshape(B, L, NA_HEADS, NA_DH) for t in _split(p[0], (NA_HEADS * NA_DH,) * 3)]

    q_c, k_c, v_c = na_qkv(p_ctx)
    q_l, k_l, v_l = na_qkv(p_lat)
    na_lat = _neighbourhood_attend(q_l, k_l, v_l, k_c, v_c, rpb, na_scale)

    def hg_prep(p):
        q = _heads(jax.nn.silu(p[1]), HG_HEADS)
        v = _heads(p[4], HG_HEADS)
        dirs = []
        for d in range(2):
            z = _heads(p[2 + d], HG_HEADS).astype(F32)
            lbd = lb[d].reshape(1, HG_HEADS, 1, HG_DK)
            logf = jnp.logaddexp(jnp.log(lbd), jnp.log1p(-lbd) + jax.nn.log_sigmoid(z))
            dirs.append(((1.0 - lbd) * jax.nn.sigmoid(-z), logf))
        return (q, v), tuple(dirs)

    sh_c, dir_c = hg_prep(p_ctx)
    sh_l, dir_l = hg_prep(p_lat)
    hg_ctx, hg_lat = _bidirectional(_gla_chunked, sh_c, sh_l, dir_c, dir_l, need_ctx)
    out_lat = jnp.concatenate([na_lat, _gated_rms(hg_lat, hg_gain, p_lat[5])], -1) @ w_out
    if not need_ctx:
        return out_lat, None
    na_ctx = _flat(_softmax_attend(q_c, k_c, v_c, na_scale))
    out_ctx = jnp.concatenate([na_ctx, _gated_rms(hg_ctx, hg_gain, p_ctx[5])], -1) @ w_out
    return out_lat, out_ctx


def setup_inputs(seed: int = 0) -> dict:
    key = jax.random.key(seed)
    ks = jax.random.split(key, 32)
    D = D_MODEL
    nrm = lambda k, shape, s: jax.random.normal(k, shape, F32) * s
    dt = jnp.exp(jax.random.uniform(ks[11], (N_EVEN, 2, GDN_HEADS), F32, math.log(1e-3), math.log(1e-1)))
    return {
        'x': nrm(ks[0], (BATCH, SEQ, D), 1.0),
        'c': nrm(ks[1], (BATCH, D), 1.0),
        'ctx': nrm(ks[2], (BATCH, CTX_LEN, D), 1.0),
        'c_ctx': nrm(ks[3], (D,), 1.0),
        'w_mod': nrm(ks[4], (DEPTH, D, N_MOD * D), 0.5 * D ** -0.5),
        'b_mod': nrm(ks[5], (DEPTH, N_MOD * D), 0.02),
        'ln_g': 1.0 + nrm(ks[6], (DEPTH, 2, D), 0.02),
        'ln_b': nrm(ks[7], (DEPTH, 2, D), 0.02),
        'even_w_in': nrm(ks[8], (N_EVEN, D, EVEN_IN), D ** -0.5),
        'gdn_conv': nrm(ks[9], (N_EVEN, GDN_CONV, GDN_QKV), GDN_CONV ** -0.5),
        'gdn_a_log': jnp.log(jax.random.uniform(ks[10], (N_EVEN, 2, GDN_HEADS), F32, 1.0, 16.0)),
        'gdn_dt_bias': dt + jnp.log(-jnp.expm1(-dt)),
        'gdn_norm': 1.0 + nrm(ks[12], (N_EVEN, GDN_DV), 0.02),
        'mla_q_norm': 1.0 + nrm(ks[13], (N_EVEN, MLA_Q_LORA), 0.02),
        'mla_kv_norm': 1.0 + nrm(ks[14], (N_EVEN, MLA_KV_LORA), 0.02),
        'mla_w_uq': nrm(ks[15], (N_EVEN, MLA_Q_LORA, MLA_HEADS * (MLA_NOPE + MLA_ROPE)), MLA_Q_LORA ** -0.5),
        'mla_w_ukv': nrm(ks[16], (N_EVEN, MLA_KV_LORA, MLA_HEADS * (MLA_NOPE + MLA_DV)), MLA_KV_LORA ** -0.5),
        'even_w_out': nrm(ks[17], (N_EVEN, EVEN_MIX, D), DEEPNORM_BETA * EVEN_MIX ** -0.5),
        'odd_w_in': nrm(ks[18], (N_ODD, D, ODD_IN), D ** -0.5),
        'na_rpb': nrm(ks[19], (N_ODD, NA_HEADS, 2 * NA_KH - 1, 2 * NA_KW - 1), 0.02),
        'hg_lb': nrm(ks[20], (N_ODD, 2, HG_HEADS * HG_DK), 1.0),
        'hg_norm': 1.0 + nrm(ks[21], (N_ODD, HG_DV), 0.02),
        'odd_w_out': nrm(ks[22], (N_ODD, ODD_MIX, D), DEEPNORM_BETA * ODD_MIX ** -0.5),
        'moe_router': nrm(ks[23], (DEPTH, D, N_EXPERTS), D ** -0.5),
        'moe_w_gate': nrm(ks[24], (DEPTH, N_EXPERTS, D, EXPERT_FF), D ** -0.5),
        'moe_w_up': nrm(ks[25], (DEPTH, N_EXPERTS, D, EXPERT_FF), D ** -0.5),
        'moe_w_down': nrm(ks[26], (DEPTH, N_EXPERTS, EXPERT_FF, D), DEEPNORM_BETA * EXPERT_FF ** -0.5),
    }


def reference(x, c, ctx, c_ctx, w_mod, b_mod, ln_g, ln_b, even_w_in, gdn_conv, gdn_a_log, gdn_dt_bias,
              gdn_norm, mla_q_norm, mla_kv_norm, mla_w_uq, mla_w_ukv, even_w_out, odd_w_in, na_rpb, hg_lb,
              hg_norm, odd_w_out, moe_router, moe_w_gate, moe_w_up, moe_w_down):
    lb_all = jnp.cumsum(jax.nn.softmax(hg_lb.astype(F32), axis=0), axis=0)
    lb_all = lb_all - lb_all[:1]
    s_lat = jax.nn.silu(c)
    s_ctx = jax.nn.silu(c_ctx)
    for layer in range(DEPTH):
        j = layer // 2
        need_ctx = layer < DEPTH - 1
        m_lat = jnp.split((s_lat @ w_mod[layer] + b_mod[layer])[:, None, :], N_MOD, axis=-1)
        m_ctx = jnp.split((s_ctx @ w_mod[layer] + b_mod[layer])[None, None, :], N_MOD, axis=-1)
        u_lat = x * (1 + m_lat[1]) + m_lat[0]
        u_ctx = ctx * (1 + m_ctx[1]) + m_ctx[0]
        if layer % 2 == 0:
            o_lat, o_ctx = _even_mixer(u_lat, u_ctx, even_w_in[j], gdn_conv[j], gdn_a_log[j], gdn_dt_bias[j],
                                       gdn_norm[j], mla_q_norm[j], mla_kv_norm[j], mla_w_uq[j], mla_w_ukv[j],
                                       even_w_out[j], need_ctx)
        else:
            o_lat, o_ctx = _odd_mixer(u_lat, u_ctx, odd_w_in[j], na_rpb[j], lb_all[j], hg_norm[j],
                                      odd_w_out[j], need_ctx)
        x = _layer_norm(DEEPNORM_ALPHA * x + m_lat[2] * o_lat, ln_g[layer, 0], ln_b[layer, 0])
        y = _expert_choice_ffn(x * (1 + m_lat[4]) + m_lat[3], moe_router[layer], moe_w_gate[layer],
                               moe_w_up[layer], moe_w_down[layer])
        x = _layer_norm(DEEPNORM_ALPHA * x + m_lat[5] * y, ln_g[layer, 1], ln_b[layer, 1])
        if need_ctx:
            ctx = _layer_norm(DEEPNORM_ALPHA * ctx + m_ctx[2] * o_ctx, ln_g[layer, 0], ln_b[layer, 0])
            y_ctx = _expert_choice_ffn(ctx * (1 + m_ctx[4]) + m_ctx[3], moe_router[layer], moe_w_gate[layer],
                                       moe_w_up[layer], moe_w_down[layer])
            ctx = _layer_norm(DEEPNORM_ALPHA * ctx + m_ctx[5] * y_ctx, ln_g[layer, 1], ln_b[layer, 1])
    return x
```

```python
import functools
import math

import jax
import jax.numpy as jnp
from jax import lax
from jax.experimental import pallas as pl
from jax.experimental.pallas import tpu as pltpu

F32 = jnp.float32
BF16 = jnp.bfloat16
HIGHEST = lax.Precision.HIGHEST

DEPTH = 4
GRID_W = 64
N_MOD = 6
EPS = 1e-6
DEEPNORM_ALPHA = (2.0 * DEPTH) ** 0.25

GDN_HEADS = 4
GDN_DK = 128
GDN_DV = 128
GDN_CONV = 5
MLA_HEADS = 4
MLA_Q_LORA = 256
MLA_KV_LORA = 256
MLA_NOPE = 128
MLA_ROPE = 64
MLA_DV = 128
ROPE_THETA = 10000.0
NA_HEADS = 8
NA_DH = 64
NA_KH = 8
NA_KW = 16
HG_HEADS = 4
HG_DK = 128
HG_DV = 128
N_EXPERTS = 16
EC_CAPACITY = 2

CHUNK = 64
SUB = 16
PREP_CHUNKS = 4
SCATTER_GROUP = 8
LANES = 128
MLA_QK_PAD = 256
NEG_BIG = -1e30
VMEM_LIMIT = 56 * 1024 * 1024

_NT = (((1,), (1,)), ((), ()))


def _cparams(sem):
    return pltpu.CompilerParams(dimension_semantics=sem, vmem_limit_bytes=VMEM_LIMIT)


def _mm(a, b):
    return jnp.dot(a.astype(BF16), b.astype(BF16), preferred_element_type=F32)


def _mm_nt(a, b):
    return lax.dot_general(a.astype(BF16), b.astype(BF16), _NT, preferred_element_type=F32)


def _mm32(a, b):
    return jnp.dot(a, b, precision=HIGHEST, preferred_element_type=F32)


def _mm32_nt(a, b):
    return lax.dot_general(a, b, _NT, precision=HIGHEST, preferred_element_type=F32)


def _silu(x):
    return x * jax.nn.sigmoid(x)


def _rms(x):
    return x * lax.rsqrt(jnp.mean(x * x, -1, keepdims=True) + EPS)


def _ln(y, g, b):
    yc = y - jnp.mean(y, -1, keepdims=True)
    return yc * lax.rsqrt(jnp.mean(yc * yc, -1, keepdims=True) + EPS) * g + b


def _mod_kernel(c_ref, w_ref, b_ref, o_ref):
    o_ref[...] = _mm32(_silu(c_ref[...]), w_ref[...]) + b_ref[...]


def _modulation(cc, w_mod, b_mod):
    depth, d, n = w_mod.shape
    tn = n // 4
    return pl.pallas_call(
        _mod_kernel,
        grid=(depth, n // tn),
        in_specs=[pl.BlockSpec((cc.shape[0], d), lambda l, j: (0, 0)),
                  pl.BlockSpec((None, d, tn), lambda l, j: (l, 0, j)),
                  pl.BlockSpec((None, 1, tn), lambda l, j: (l, 0, j))],
        out_specs=pl.BlockSpec((None, cc.shape[0], tn), lambda l, j: (l, 0, j)),
        out_shape=jax.ShapeDtypeStruct((depth, cc.shape[0], n), F32),
        compiler_params=_cparams(("parallel", "parallel")),
    )(cc, w_mod, b_mod.reshape(depth, 1, n))


def _inproj_kernel(x_ref, sh_ref, sc_ref, w_ref, o_ref):
    u = x_ref[...] * (1.0 + sc_ref[...]) + sh_ref[...]
    o_ref[...] = jnp.dot(u.astype(BF16), w_ref[...], preferred_element_type=F32)


def _tab_spec(d, tps, nct):
    return pl.BlockSpec((None, None, 1, d), lambda i: (i // tps, jnp.where(i % tps >= nct, 1, 0), 0, 0))


def _inproj(xx, sh, sc, w, tm, T):
    B, L, D = xx.shape
    n = w.shape[1]
    tps, nct = L // tm, T // tm
    out = pl.pallas_call(
        _inproj_kernel,
        grid=(B * tps,),
        in_specs=[pl.BlockSpec((tm, D), lambda i: (i, 0)),
                  _tab_spec(D, tps, nct), _tab_spec(D, tps, nct),
                  pl.BlockSpec((D, n), lambda i: (0, 0))],
        out_specs=pl.BlockSpec((tm, n), lambda i: (i, 0)),
        out_shape=jax.ShapeDtypeStruct((B * L, n), F32),
        compiler_params=_cparams(("parallel",)),
    )(xx.reshape(B * L, D), sh, sc, w)
    return out.reshape(B, L, n)


def _scan_masks(d):
    ri = lax.broadcasted_iota(jnp.int32, (CHUNK, CHUNK), 0)
    ci = lax.broadcasted_iota(jnp.int32, (CHUNK, CHUNK), 1)
    if d == 0:
        return ci <= ri, ci < ri, ri <= ci, ci == ri
    return ci >= ri, ci > ri, ri >= ci, ci == ri


def _bwd_chunk(i, nc, n):
    return jnp.where(i < nc, nc - 1 - i, n - 1 - (i - nc))


def _gdn_kernel(nega_ref, dtb_ref, q_ref, k_ref, v_ref, z_ref, ab_ref, cwq_ref, cwk_ref, cwv_ref, gain_ref,
                o_ref,
                pad_ref, qn_ref, kn_ref, vn_ref, u_ref, w_ref, qg_ref, kdt_ref, qk_ref, el_ref, of_ref, ob_ref,
                *, T, L):
    h = pl.program_id(1)
    n_chunks = L // CHUNK
    nc = T // CHUNK

    def conv_prep(src_ref, cw_ref, dst_ref, post):
        zero8 = jnp.zeros((8, LANES), F32)
        pad_ref[0:8, :] = zero8
        pad_ref[8:8 + T, :] = src_ref[0:T, :]
        pad_ref[8 + T:16 + T, :] = zero8
        pad_ref[16 + T:16 + L, :] = src_ref[T:L, :]
        pad_ref[16 + L:24 + L, :] = zero8
        cw = cw_ref[...]

        def body(n, carry):
            r0 = pl.multiple_of(n * CHUNK, CHUNK)
            p0 = r0 + jnp.where(r0 < T, 8, 16) - GDN_CONV // 2
            acc = jnp.zeros((CHUNK, LANES), F32)
            for j in range(GDN_CONV):
                acc = acc + pad_ref[pl.ds(p0 + j, CHUNK), :] * cw[j:j + 1, :]
            dst_ref[pl.ds(r0, CHUNK), :] = post(_silu(acc))
            return carry

        lax.fori_loop(0, n_chunks, body, 0)

    def l2n(y):
        return y * lax.rsqrt(jnp.sum(y * y, -1, keepdims=True) + EPS)

    conv_prep(q_ref, cwq_ref, qn_ref, lambda y: l2n(y) * (GDN_DK ** -0.5))
    conv_prep(k_ref, cwk_ref, kn_ref, l2n)
    conv_prep(v_ref, cwv_ref, vn_ref, lambda y: y)

    lane = lax.broadcasted_iota(jnp.int32, (CHUNK, LANES), 1)

    def chunk_prep(i, carry):
        chains = []
        for c in range(PREP_CHUNKS):
            n = PREP_CHUNKS * i + c
            rows = pl.ds(pl.multiple_of(n * CHUNK, CHUNK), CHUNK)
            q, k, v, ab = qn_ref[rows, :], kn_ref[rows, :], vn_ref[rows, :], ab_ref[rows, :]
            for d in range(2):
                chains.append(dict(n=n, d=d, rows=rows, q=q, k=k, v=v, ab=ab))
        for ch in chains:
            d = ch["d"]
            incl, strict, incl_t, eye = _scan_masks(d)
            a_col = jnp.sum(jnp.where(lane == 4 * d + h, ch["ab"], 0.0), axis=1, keepdims=True)
            b_col = jnp.sum(jnp.where(lane == 8 + 4 * d + h, ch["ab"], 0.0), axis=1, keepdims=True)
            g_col = nega_ref[4 * d + h] * jax.nn.softplus(a_col + dtb_ref[4 * d + h])
            beta = jax.nn.sigmoid(b_col)
            g_row = jnp.sum(jnp.where(eye, g_col, 0.0), axis=0, keepdims=True)
            gc_col = jnp.sum(jnp.where(incl, g_row, 0.0), axis=1, keepdims=True)
            gc_row = jnp.sum(jnp.where(incl_t, g_col, 0.0), axis=0, keepdims=True)
            g_tot = jnp.sum(g_col, axis=0, keepdims=True)
            ch["decay"] = jnp.where(incl, jnp.exp(jnp.where(incl, gc_col - gc_row, 0.0)), 0.0)
            ch["kb"] = ch["k"] * beta
            ch["vb"] = ch["v"] * beta
            ch["e_col"] = jnp.exp(gc_col)
            ch["kd"] = ch["k"] * jnp.exp(g_tot - gc_col)
            ch["e_tot"] = jnp.exp(g_tot)
        for ch in chains:
            _, strict, _, eye = _scan_masks(ch["d"])
            a = jnp.where(strict, _mm_nt(ch["kb"], ch["k"]) * ch["decay"], 0.0)
            ch["p"] = jnp.where(eye, 1.0, 0.0) - a
            ch["bp"] = a
        for _ in range(int(math.log2(CHUNK)) - 1):
            for ch in chains:
                ch["bp"] = _mm(ch["bp"], ch["bp"])
            for ch in chains:
                ch["p"] = ch["p"] + _mm(ch["p"], ch["bp"])
        for ch in chains:
            d, n, rows = ch["d"], ch["n"], ch["rows"]
            incl = _scan_masks(d)[0]
            uw = _mm(ch["p"], jnp.concatenate([ch["vb"], ch["kb"] * ch["e_col"]], axis=1))
            u_ref[d, rows, :] = uw[:, :GDN_DV]
            w_ref[d, rows, :] = uw[:, GDN_DV:].astype(BF16)
            qk_ref[d, rows, :] = jnp.where(incl, _mm_nt(ch["q"], ch["k"]) * ch["decay"], 0.0).astype(BF16)
            qg_ref[d, rows, :] = (ch["q"] * ch["e_col"]).astype(BF16)
            kdt_ref[d * n_chunks + n] = ch["kd"].T.astype(BF16)
            el_ref[d * n_chunks + n] = jnp.broadcast_to(ch["e_tot"], (8, LANES))
        return carry

    lax.fori_loop(0, n_chunks // PREP_CHUNKS, chunk_prep, 0)

    def step(d, n, s):
        rows = pl.ds(pl.multiple_of(n * CHUNK, CHUNK), CHUNK)
        v_new = u_ref[d, rows, :] - _mm(w_ref[d, rows, :], s)
        o = _mm(qg_ref[d, rows, :], s) + _mm(qk_ref[d, rows, :], v_new)
        s = s * el_ref[d * n_chunks + n][0:1, :] + _mm(kdt_ref[d * n_chunks + n], v_new)
        return s, o, rows

    def scan_body(i, carry):
        sf, sb = carry
        sf, o_f, rows_f = step(0, i, sf)
        sb, o_b, rows_b = step(1, _bwd_chunk(i, nc, n_chunks), sb)
        of_ref[rows_f, :] = o_f
        ob_ref[rows_b, :] = o_b
        return sf, sb

    zero = jnp.zeros((GDN_DK, GDN_DV), F32)
    lax.fori_loop(0, n_chunks, scan_body, (zero, zero))

    o = of_ref[...] + ob_ref[...]
    o_ref[...] = (_rms(o) * gain_ref[...] * _silu(z_ref[...])).astype(BF16)


def _gdn(p, conv_w, neg_a, dt_bias, gain, T):
    B, L, _ = p.shape
    H = GDN_HEADS
    n_chunks = L // CHUNK
    col = lambda off: pl.BlockSpec((None, L, LANES), lambda b, h: (b, 0, off + h))
    cw = lambda off: pl.BlockSpec((GDN_CONV, LANES), lambda b, h: (0, off + h))
    smem = pl.BlockSpec(memory_space=pltpu.SMEM)
    return pl.pallas_call(
        functools.partial(_gdn_kernel, T=T, L=L),
        grid=(B, H),
        in_specs=[smem, smem, col(0), col(H), col(2 * H), col(3 * H),
                  pl.BlockSpec((None, L, LANES), lambda b, h: (b, 0, 21)),
                  cw(0), cw(H), cw(2 * H),
                  pl.BlockSpec((1, LANES), lambda b, h: (0, 0))],
        out_specs=pl.BlockSpec((None, L, LANES), lambda b, h: (b, 0, h)),
        out_shape=jax.ShapeDtypeStruct((B, L, H * GDN_DV), BF16),
        scratch_shapes=[pltpu.VMEM((L + 24, LANES), F32),
                        pltpu.VMEM((L, LANES), F32), pltpu.VMEM((L, LANES), F32), pltpu.VMEM((L, LANES), F32),
                        pltpu.VMEM((2, L, LANES), F32), pltpu.VMEM((2, L, LANES), BF16),
                        pltpu.VMEM((2, L, LANES), BF16),
                        pltpu.VMEM((2 * n_chunks, LANES, CHUNK), BF16),
                        pltpu.VMEM((2, L, CHUNK), BF16),
                        pltpu.VMEM((2 * n_chunks, 8, LANES), F32),
                        pltpu.VMEM((L, LANES), F32), pltpu.VMEM((L, LANES), F32)],
        compiler_params=_cparams(("parallel", "parallel")),
    )(neg_a, dt_bias, p, p, p, p, p, conv_w, conv_w, conv_w, gain.reshape(1, LANES))


def _mlaproj_kernel(ql_ref, kvl_ref, kr_ref, cs_ref, qg_ref, kvg_ref, wq_ref, wkv_ref, q_ref, k_ref, v_ref):
    q = jnp.dot((_rms(ql_ref[...]) * qg_ref[...]).astype(BF16), wq_ref[...], preferred_element_type=F32)
    kv = jnp.dot((_rms(kvl_ref[...]) * kvg_ref[...]).astype(BF16), wkv_ref[...], preferred_element_type=F32)
    cs = cs_ref[...]
    lane = lax.broadcasted_iota(jnp.int32, cs.shape, 1)

    def rope(blk):
        t = blk * cs
        return jnp.where(lane < MLA_ROPE, t + pltpu.roll(t, MLA_ROPE, 1), 0.0)

    kr = rope(kr_ref[...]).astype(BF16)
    for h in range(MLA_HEADS):
        o = h * MLA_QK_PAD
        q_ref[:, o:o + MLA_NOPE] = q[:, o:o + MLA_NOPE].astype(BF16)
        q_ref[:, o + MLA_NOPE:o + MLA_QK_PAD] = rope(q[:, o + MLA_NOPE:o + MLA_QK_PAD]).astype(BF16)
        k_ref[:, o:o + MLA_NOPE] = kv[:, o:o + MLA_NOPE].astype(BF16)
        k_ref[:, o + MLA_NOPE:o + MLA_QK_PAD] = kr
        v_ref[:, h * MLA_DV:(h + 1) * MLA_DV] = kv[:, o + MLA_NOPE:o + MLA_QK_PAD].astype(BF16)


def _mlaproj(p, cs, q_gain, kv_gain, wq, wkv, tm):
    B, L, n = p.shape
    tps = L // tm
    p2 = p.reshape(B * L, n)
    hq = MLA_HEADS * MLA_QK_PAD
    q, k, v = pl.pallas_call(
        _mlaproj_kernel,
        grid=(B * tps,),
        in_specs=[pl.BlockSpec((tm, MLA_Q_LORA), lambda i: (i, 8)),
                  pl.BlockSpec((tm, MLA_KV_LORA), lambda i: (i, 9)),
                  pl.BlockSpec((tm, LANES), lambda i: (i, 20)),
                  pl.BlockSpec((tm, LANES), lambda i: (i % tps, 0)),
                  pl.BlockSpec((1, MLA_Q_LORA), lambda i: (0, 0)),
                  pl.BlockSpec((1, MLA_KV_LORA), lambda i: (0, 0)),
                  pl.BlockSpec((MLA_Q_LORA, hq), lambda i: (0, 0)),
                  pl.BlockSpec((MLA_KV_LORA, hq), lambda i: (0, 0))],
        out_specs=[pl.BlockSpec((tm, hq), lambda i: (i, 0)),
                   pl.BlockSpec((tm, hq), lambda i: (i, 0)),
                   pl.BlockSpec((tm, MLA_HEADS * MLA_DV), lambda i: (i, 0))],
        out_shape=[jax.ShapeDtypeStruct((B * L, hq), BF16),
                   jax.ShapeDtypeStruct((B * L, hq), BF16),
                   jax.ShapeDtypeStruct((B * L, MLA_HEADS * MLA_DV), BF16)],
        compiler_params=_cparams(("parallel",)),
    )(p2, p2, p2, cs, q_gain.reshape(1, -1), kv_gain.reshape(1, -1), wq, wkv)
    return q.reshape(B, L, hq), k.reshape(B, L, hq), v.reshape(B, L, -1)


def _softmax_pv(s, v):
    m = jnp.max(s, axis=-1, keepdims=True)
    p = jnp.exp(s - m)
    l = jnp.sum(p, axis=-1, keepdims=True)
    return jnp.dot(p.astype(BF16), v, preferred_element_type=F32) / l


def _mla_attn_kernel(q_ref, k_ref, v_ref, o_ref, *, T, nct, scale):
    qt = pl.program_id(2)
    q = q_ref[...]

    @pl.when(qt < nct)
    def _():
        s = lax.dot_general(q, k_ref[0:T, :], _NT, preferred_element_type=F32) * scale
        o_ref[...] = _softmax_pv(s, v_ref[0:T, :]).astype(BF16)

    @pl.when(qt >= nct)
    def _():
        s = lax.dot_general(q, k_ref[...], _NT, preferred_element_type=F32) * scale
        o_ref[...] = _softmax_pv(s, v_ref[...]).astype(BF16)


def _mla_attn(q, k, v, T, tq):
    B, L, _ = q.shape
    H = MLA_HEADS
    scale = (MLA_NOPE + MLA_ROPE) ** -0.5
    return pl.pallas_call(
        functools.partial(_mla_attn_kernel, T=T, nct=T // tq, scale=scale),
        grid=(B, H, L // tq),
        in_specs=[pl.BlockSpec((None, tq, MLA_QK_PAD), lambda b, h, i: (b, i, h)),
                  pl.BlockSpec((None, L, MLA_QK_PAD), lambda b, h, i: (b, 0, h)),
                  pl.BlockSpec((None, L, MLA_DV), lambda b, h, i: (b, 0, h))],
        out_specs=pl.BlockSpec((None, tq, MLA_DV), lambda b, h, i: (b, i, h)),
        out_shape=jax.ShapeDtypeStruct((B, L, H * MLA_DV), BF16),
        compiler_params=_cparams(("parallel", "parallel", "arbitrary")),
    )(q, k, v)


def _outproj_kernel(a_ref, b_ref, w_ref, x_ref, gate_ref, lng_ref, lnb_ref, sh_ref, sc_ref, wr_ref,
                    xo_ref, aff_ref):
    ka = a_ref.shape[1]
    o = (jnp.dot(a_ref[...], w_ref[0:ka, :], preferred_element_type=F32)
         + jnp.dot(b_ref[...], w_ref[ka:, :], preferred_element_type=F32))
    xn = _ln(DEEPNORM_ALPHA * x_ref[...] + gate_ref[...] * o, lng_ref[...], lnb_ref[...])
    xo_ref[...] = xn
    hmod = xn * (1.0 + sc_ref[...]) + sh_ref[...]
    logits = _mm32_nt(wr_ref[...], hmod)
    e = jnp.exp(logits - jnp.max(logits, axis=0, keepdims=True))
    aff_ref[...] = e / jnp.sum(e, axis=0, keepdims=True)


def _outproj(a, b, w, xx, gate, ln_g, ln_b, sh, sc, wr_t, tm, T):
    B, L, D = xx.shape
    tps, nct = L // tm, T // tm
    ka, kb = a.shape[-1], b.shape[-1]
    row = lambda n: pl.BlockSpec((1, n), lambda i: (0, 0))
    xo, aff = pl.pallas_call(
        _outproj_kernel,
        grid=(B * tps,),
        in_specs=[pl.BlockSpec((tm, ka), lambda i: (i, 0)),
                  pl.BlockSpec((tm, kb), lambda i: (i, 0)),
                  pl.BlockSpec((ka + kb, D), lambda i: (0, 0)),
                  pl.BlockSpec((tm, D), lambda i: (i, 0)),
                  _tab_spec(D, tps, nct), row(D), row(D),
                  _tab_spec(D, tps, nct), _tab_spec(D, tps, nct),
                  pl.BlockSpec((N_EXPERTS, D), lambda i: (0, 0))],
        out_specs=[pl.BlockSpec((tm, D), lambda i: (i, 0)),
                   pl.BlockSpec((None, N_EXPERTS, tm), lambda i: (i // tps, 0, i % tps))],
        out_shape=[jax.ShapeDtypeStruct((B * L, D), F32),
                   jax.ShapeDtypeStruct((B, N_EXPERTS, L), F32)],
        compiler_params=_cparams(("parallel",)),
    )(a.reshape(B * L, ka), b.reshape(B * L, kb), w, xx.reshape(B * L, D), gate,
      ln_g.reshape(1, D), ln_b.reshape(1, D), sh, sc, wr_t)
    return xo.reshape(B, L, D), aff


def _topk_kernel(aff_ref, idx_ref, gate_ref, *, T, cap_c, cap_l):
    ri = lax.broadcasted_iota(jnp.int32, (LANES, LANES), 0)
    ci = lax.broadcasted_iota(jnp.int32, (LANES, LANES), 1)
    tri = jnp.where(ri <= ci, 1.0, 0.0).astype(BF16)

    def prefix_incl(x):
        carry = jnp.zeros((x.shape[0], 1), F32)
        outs = []
        for j in range(x.shape[1] // LANES):
            pj = jnp.dot(x[:, j * LANES:(j + 1) * LANES].astype(BF16), tri, preferred_element_type=F32) + carry
            outs.append(pj)
            carry = pj[:, LANES - 1:LANES]
        return jnp.concatenate(outs, axis=1)

    def select(aff, cap, base, row0):
        ne, n = aff.shape
        def bisect(_, lh):
            lo, hi = lh
            mid = lo + lax.shift_right_logical(hi - lo, 1)
            cnt = jnp.sum(jnp.where(aff >= pltpu.bitcast(mid, F32), 1.0, 0.0), axis=1, keepdims=True)
            ge = cnt >= float(cap)
            return jnp.where(ge, mid, lo), jnp.where(ge, hi, mid)

        lo0 = jnp.zeros((ne, 1), jnp.int32)
        hi0 = jnp.full((ne, 1), 0x7F800000, jnp.int32)
        thr_bits, _ = lax.fori_loop(0, 31, bisect, (lo0, hi0))
        thr = pltpu.bitcast(thr_bits, F32)
        gt = aff > thr
        eqf = jnp.where(aff == thr, 1.0, 0.0)
        need = float(cap) - jnp.sum(jnp.where(gt, 1.0, 0.0), axis=1, keepdims=True)
        eq_before = prefix_incl(eqf) - eqf
        self_ = jnp.where(gt, 1.0, jnp.where(eq_before < need, eqf, 0.0))
        cnt = prefix_incl(self_)
        slot = lax.broadcasted_iota(jnp.int32, (cap, 1), 0).astype(F32)
        lane = lax.broadcasted_iota(jnp.int32, (cap, LANES), 1)
        idx_acc = jnp.zeros((cap, LANES), F32)
        gate_acc = jnp.zeros((cap, LANES), F32)
        for e in range(ne):
            ce, se, ae = cnt[e:e + 1, :], self_[e:e + 1, :], aff[e:e + 1, :]
            idx_e = jnp.sum(jnp.where(ce <= slot, 1.0, 0.0), axis=1, keepdims=True)
            hit = jnp.where(ce == slot + 1.0, se, 0.0)
            gate_e = jnp.sum(hit * ae, axis=1, keepdims=True)
            idx_acc = jnp.where(lane == e, idx_e + float(base), idx_acc)
            gate_acc = jnp.where(lane == e, gate_e, gate_acc)
        idx_ref[row0:row0 + cap, :] = idx_acc.astype(jnp.int32)
        gate_ref[row0:row0 + cap, :] = gate_acc

    aff = aff_ref[...]
    select(aff[:, 0:T], cap_c, 0, 0)
    select(aff[:, T:], cap_l, T, cap_c)


def _topk(aff, T):
    B, E, L = aff.shape
    cap_c = EC_CAPACITY * T // E
    cap_l = EC_CAPACITY * (L - T) // E
    cap = cap_c + cap_l
    idx, gate = pl.pallas_call(
        functools.partial(_topk_kernel, T=T, cap_c=cap_c, cap_l=cap_l),
        grid=(B,),
        in_specs=[pl.BlockSpec((None, E, L), lambda b: (b, 0, 0))],
        out_specs=[pl.BlockSpec((None, cap, LANES), lambda b: (b, 0, 0)),
                   pl.BlockSpec((None, cap, LANES), lambda b: (b, 0, 0))],
        out_shape=[jax.ShapeDtypeStruct((B, cap, LANES), jnp.int32),
                   jax.ShapeDtypeStruct((B, cap, LANES), F32)],
        compiler_params=_cparams(("parallel",)),
    )(aff)
    idx = jnp.swapaxes(idx[:, :, :E], 1, 2).reshape(B, E * cap)
    gate = jnp.swapaxes(gate[:, :, :E], 1, 2).reshape(B, E * cap)
    return idx, gate, cap_c, cap


def _gather_kernel(idx_ref, x_ref, sh_ref, sc_ref, o_ref, stage_ref, *, cap_c, cap):
    b = pl.program_id(0)
    base = pl.program_id(1) * cap

    def body(s, carry):
        r = idx_ref[b, base + s]
        stage_ref[pl.ds(s, 1), :] = x_ref[pl.ds(r, 1), :]
        return carry

    lax.fori_loop(0, cap, body, 0, unroll=8)
    o_ref[0:cap_c, :] = (stage_ref[0:cap_c, :] * (1.0 + sc_ref[0]) + sh_ref[0]).astype(BF16)
    o_ref[cap_c:, :] = (stage_ref[cap_c:, :] * (1.0 + sc_ref[1]) + sh_ref[1]).astype(BF16)


def _gather(idx, xx, sh, sc, cap_c, cap):
    B, L, D = xx.shape
    E = N_EXPERTS
    tab = pl.BlockSpec((None, 2, 1, D), lambda b, e, idx: (b, 0, 0, 0))
    return pl.pallas_call(
        functools.partial(_gather_kernel, cap_c=cap_c, cap=cap),
        grid_spec=pltpu.PrefetchScalarGridSpec(
            num_scalar_prefetch=1, grid=(B, E),
            in_specs=[pl.BlockSpec((None, L, D), lambda b, e, idx: (b, 0, 0)), tab, tab],
            out_specs=pl.BlockSpec((None, None, cap, D), lambda b, e, idx: (e, b, 0, 0)),
            scratch_shapes=[pltpu.VMEM((cap, D), F32)]),
        out_shape=jax.ShapeDtypeStruct((E, B, cap, D), BF16),
        compiler_params=_cparams(("parallel", "arbitrary")),
    )(idx, xx, sh, sc)


def _expert_kernel(x_ref, wg_ref, wu_ref, wd_ref, o_ref):
    x = x_ref[...]
    g = jnp.dot(x, wg_ref[...].astype(BF16), preferred_element_type=F32)
    u = jnp.dot(x, wu_ref[...].astype(BF16), preferred_element_type=F32)
    hid = (_silu(g) * u).astype(BF16)
    o_ref[...] = jnp.dot(hid, wd_ref[...].astype(BF16), preferred_element_type=F32)


def _experts(xe, w_gate, w_up, w_down, tm):
    E, M, D = xe.shape
    F = w_gate.shape[-1]
    return pl.pallas_call(
        _expert_kernel,
        grid=(E, M // tm),
        in_specs=[pl.BlockSpec((None, tm, D), lambda e, m: (e, m, 0)),
                  pl.BlockSpec((None, D, F), lambda e, m: (e, 0, 0)),
                  pl.BlockSpec((None, D, F), lambda e, m: (e, 0, 0)),
                  pl.BlockSpec((None, F, D), lambda e, m: (e, 0, 0))],
        out_specs=pl.BlockSpec((None, tm, D), lambda e, m: (e, m, 0)),
        out_shape=jax.ShapeDtypeStruct((E, M, D), F32),
        compiler_params=_cparams(("parallel", "arbitrary")),
    )(xe, w_gate, w_up, w_down)


def _scatter_kernel(idx_ref, gate_ref, ye_ref, y_ref, *, cap):
    b = pl.program_id(0)
    e = pl.program_id(1)

    @pl.when(e == 0)
    def _():
        y_ref[...] = jnp.zeros(y_ref.shape, F32)

    def body(i, carry):
        s0 = pl.multiple_of(i * SCATTER_GROUP, SCATTER_GROUP)
        ye = ye_ref[pl.ds(s0, SCATTER_GROUP), :]
        rs = [idx_ref[b, e * cap + s0 + j] for j in range(SCATTER_GROUP)]
        new = [y_ref[pl.ds(rs[j], 1), :] + gate_ref[b, e * cap + s0 + j] * ye[j:j + 1, :]
               for j in range(SCATTER_GROUP)]
        for j in range(SCATTER_GROUP):
            y_ref[pl.ds(rs[j], 1), :] = new[j]
        return carry

    lax.fori_loop(0, cap // SCATTER_GROUP, body, 0)


def _scatter(idx, gate, ye, L):
    E, B, cap, D = ye.shape
    return pl.pallas_call(
        functools.partial(_scatter_kernel, cap=cap),
        grid_spec=pltpu.PrefetchScalarGridSpec(
            num_scalar_prefetch=2, grid=(B, E),
            in_specs=[pl.BlockSpec((None, None, cap, D), lambda b, e, i, g: (e, b, 0, 0))],
            out_specs=pl.BlockSpec((None, L, D), lambda b, e, i, g: (b, 0, 0))),
        out_shape=jax.ShapeDtypeStruct((B, L, D), F32),
        compiler_params=_cparams(("parallel", "arbitrary")),
    )(idx, gate, ye)


def _resln_kernel(x_ref, y_ref, gate_ref, lng_ref, lnb_ref, o_ref):
    o_ref[...] = _ln(DEEPNORM_ALPHA * x_ref[...] + gate_ref[...] * y_ref[...], lng_ref[...], lnb_ref[...])


def _resln(xx, y, gate, ln_g, ln_b, tm, T):
    B, L, D = xx.shape
    tps, nct = L // tm, T // tm
    row = pl.BlockSpec((1, D), lambda i: (0, 0))
    blk = pl.BlockSpec((tm, D), lambda i: (i, 0))
    out = pl.pallas_call(
        _resln_kernel,
        grid=(B * tps,),
        in_specs=[blk, blk, _tab_spec(D, tps, nct), row, row],
        out_specs=blk,
        out_shape=jax.ShapeDtypeStruct((B * L, D), F32),
        compiler_params=_cparams(("parallel",)),
    )(xx.reshape(B * L, D), y.reshape(B * L, D), gate, ln_g.reshape(1, D), ln_b.reshape(1, D))
    return out.reshape(B, L, D)


def _na_kernel(q_ref, k_ref, v_ref, bias_ref, o_ref, *, T, nct, rows, kh, scale):
    i = pl.program_id(2)
    q = q_ref[...]
    lane = lax.broadcasted_iota(jnp.int32, q.shape, 1)
    kc = k_ref[0:T, :].astype(BF16)
    vc = v_ref[0:T, :].astype(BF16)
    qh = [jnp.where(lane < NA_DH, q, 0.0).astype(BF16), jnp.where(lane >= NA_DH, q, 0.0).astype(BF16)]

    @pl.when(i < nct)
    def _():
        outs = [_softmax_pv(lax.dot_general(qh[j], kc, _NT, preferred_element_type=F32) * scale, vc)
                for j in range(2)]
        o_ref[...] = jnp.where(lane < NA_DH, outs[0], outs[1]).astype(BF16)

    @pl.when(i >= nct)
    def _():
        r = i - nct
        base = jnp.clip(r - kh // 2, 0, rows - kh)
        t0 = pl.multiple_of(T + base * GRID_W, GRID_W)
        kb = k_ref[pl.ds(t0, kh * GRID_W), :].astype(BF16)
        vb = v_ref[pl.ds(t0, kh * GRID_W), :].astype(BF16)
        outs = []
        for j in range(2):
            s_w = lax.dot_general(qh[j], kb, _NT, preferred_element_type=F32) * scale + bias_ref[j]
            s_c = lax.dot_general(qh[j], kc, _NT, preferred_element_type=F32) * scale
            m = jnp.maximum(jnp.max(s_w, axis=-1, keepdims=True), jnp.max(s_c, axis=-1, keepdims=True))
            p_w = jnp.exp(s_w - m)
            p_c = jnp.exp(s_c - m)
            l = jnp.sum(p_w, axis=-1, keepdims=True) + jnp.sum(p_c, axis=-1, keepdims=True)
            outs.append((jnp.dot(p_w.astype(BF16), vb, preferred_element_type=F32)
                         + jnp.dot(p_c.astype(BF16), vc, preferred_element_type=F32)) / l)
        o_ref[...] = jnp.where(lane < NA_DH, outs[0], outs[1]).astype(BF16)


def _na_bias_table(rpb, rows, kh):
    w = jnp.arange(GRID_W)
    c0 = jnp.clip(w - NA_KW // 2, 0, GRID_W - NA_KW)
    col_ok = (w[None, :] >= c0[:, None]) & (w[None, :] < c0[:, None] + NA_KW)
    col_off = jnp.clip(w[None, :] - w[:, None] + (NA_KW - 1), 0, 2 * NA_KW - 2)
    n_slab = 2 * NA_KH - kh
    onehot = (col_off[:, :, None] == jnp.arange(2 * NA_KW - 1)).astype(F32)
    base = jnp.einsum('hrm,wcm->hrwc', rpb.astype(F32), onehot, precision=HIGHEST)
    base = jnp.where(col_ok[None, None], base, NEG_BIG)
    bias = jnp.stack([base[:, s:s + kh] for s in range(n_slab)], axis=1)
    return jnp.transpose(bias, (0, 1, 3, 2, 4)).reshape(NA_HEADS, n_slab, GRID_W, kh * GRID_W)


def _na(p, bias, T):
    B, L, _ = p.shape
    rows = (L - T) // GRID_W
    kh = min(NA_KH, rows)
    nct = T // GRID_W
    hp = NA_HEADS // 2

    def slab(i):
        r = jnp.maximum(i - nct, 0)
        return jnp.clip(r - kh // 2, 0, rows - kh) - r + (NA_KH - 1)

    return pl.pallas_call(
        functools.partial(_na_kernel, T=T, nct=nct, rows=rows, kh=kh, scale=NA_DH ** -0.5),
        grid=(B, hp, L // GRID_W),
        in_specs=[pl.BlockSpec((None, GRID_W, LANES), lambda b, h, i: (b, i, h)),
                  pl.BlockSpec((None, L, LANES), lambda b, h, i: (b, 0, hp + h)),
                  pl.BlockSpec((None, L, LANES), lambda b, h, i: (b, 0, 2 * hp + h)),
                  pl.BlockSpec((2, None, GRID_W, kh * GRID_W), lambda b, h, i: (h, slab(i), 0, 0))],
        out_specs=pl.BlockSpec((None, GRID_W, LANES), lambda b, h, i: (b, i, h)),
        out_shape=jax.ShapeDtypeStruct((B, L, NA_HEADS * NA_DH), BF16),
        compiler_params=_cparams(("parallel", "parallel", "arbitrary")),
    )(p, p, p, bias)


def _hgrn_kernel(q_ref, zf_ref, zb_ref, v_ref, gz_ref, llb_ref, l1m_ref, oml_ref, gain_ref,
                 o_ref,
                 qg_ref, kd_ref, oi_ref, vt_ref, el_ref, of_ref, ob_ref, *, T, L):
    n_chunks = L // CHUNK
    nc = T // CHUNK
    nsub = CHUNK // SUB
    ri = lax.broadcasted_iota(jnp.int32, (CHUNK, CHUNK), 0)
    ci = lax.broadcasted_iota(jnp.int32, (CHUNK, CHUNK), 1)
    lane_s = lax.broadcasted_iota(jnp.int32, (SUB, LANES), 1)
    row_s = lax.broadcasted_iota(jnp.int32, (SUB, 1), 0)
    zpad = jnp.zeros((LANES - SUB, LANES), F32)

    def chunk_prep(n, carry):
        rows = pl.ds(pl.multiple_of(n * CHUNK, CHUNK), CHUNK)
        qs = _silu(q_ref[rows, :])
        v = v_ref[rows, :]
        vt_ref[n] = v.T
        for d in range(2):
            z = (zf_ref if d == 0 else zb_ref)[rows, :]
            logf = jnp.logaddexp(llb_ref[d:d + 1, :], l1m_ref[d:d + 1, :] + jax.nn.log_sigmoid(z))
            kk = oml_ref[d:d + 1, :] * jax.nn.sigmoid(-z)
            incl = (ci <= ri) if d == 0 else (ci >= ri)
            g = _mm32(jnp.where(incl, 1.0, 0.0), logf)
            g_tot = jnp.sum(logf, axis=0, keepdims=True)
            a_rows = []
            for i in range(nsub):
                ref_row = i * SUB if d == 0 else i * SUB + SUB - 1
                g_ref = g[ref_row:ref_row + 1, :]
                qt = qs[i * SUB:(i + 1) * SUB, :] * jnp.exp(g[i * SUB:(i + 1) * SUB, :] - g_ref)
                kt = kk * jnp.exp(jnp.minimum(g_ref - g, 0.0))
                a_rows.append(_mm32_nt(qt, kt))
            a_off = jnp.concatenate(a_rows, axis=0)
            blk_r, blk_c = ri // SUB, ci // SUB
            a_off = jnp.where((blk_c < blk_r) if d == 0 else (blk_c > blk_r), a_off, 0.0)
            o_intra = _mm(a_off, v)
            diag = []
            for i in range(nsub):
                gb = g[i * SUB:(i + 1) * SUB, :]
                qb = qs[i * SUB:(i + 1) * SUB, :]
                acc = jnp.zeros((SUB, LANES), F32)
                for s in range(SUB):
                    r = i * SUB + s
                    tile = qb * jnp.exp(jnp.minimum(gb - g[r:r + 1, :], 0.0)) * kk[r:r + 1, :]
                    col = jnp.sum(tile, axis=1, keepdims=True)
                    col = jnp.where((row_s >= s) if d == 0 else (row_s <= s), col, 0.0)
                    acc = jnp.where(lane_s == s, col, acc)
                v_ext = jnp.concatenate([v[i * SUB:(i + 1) * SUB, :], zpad], axis=0)
                diag.append(_mm(acc, v_ext))
            oi_ref[d, rows, :] = o_intra + jnp.concatenate(diag, axis=0)
            qg_ref[d, rows, :] = qs * jnp.exp(g)
            kd_ref[d, rows, :] = kk * jnp.exp(g_tot - g)
            el_ref[d * n_chunks + n] = jnp.broadcast_to(jnp.exp(g_tot), (8, LANES))
        return carry

    lax.fori_loop(0, n_chunks, chunk_prep, 0)

    def step(d, n, st):
        rows = pl.ds(pl.multiple_of(n * CHUNK, CHUNK), CHUNK)
        o = _mm_nt(qg_ref[d, rows, :], st) + oi_ref[d, rows, :]
        st = st * el_ref[d * n_chunks + n][0:1, :] + _mm(vt_ref[n], kd_ref[d, rows, :])
        return st, o, rows

    def scan_body(i, carry):
        sf, sb = carry
        sf, o_f, rows_f = step(0, i, sf)
        sb, o_b, rows_b = step(1, _bwd_chunk(i, nc, n_chunks), sb)
        of_ref[rows_f, :] = o_f
        ob_ref[rows_b, :] = o_b
        return sf, sb

    zero = jnp.zeros((HG_DV, HG_DK), F32)
    lax.fori_loop(0, n_chunks, scan_body, (zero, zero))
    o = of_ref[...] + ob_ref[...]
    o_ref[...] = (_rms(o) * gain_ref[...] * _silu(gz_ref[...])).astype(BF16)


def _hgrn(p, llb, l1m, oml, gain, T):
    B, L, _ = p.shape
    H = HG_HEADS
    n_chunks = L // CHUNK
    col = lambda off: pl.BlockSpec((None, L, LANES), lambda b, h: (b, 0, off + h))
    lbs = pl.BlockSpec((2, LANES), lambda b, h: (0, h))
    return pl.pallas_call(
        functools.partial(_hgrn_kernel, T=T, L=L),
        grid=(B, H),
        in_specs=[col(12), col(16), col(20), col(24), col(28), lbs, lbs, lbs,
                  pl.BlockSpec((1, LANES), lambda b, h: (0, 0))],
        out_specs=pl.BlockSpec((None, L, LANES), lambda b, h: (b, 0, h)),
        out_shape=jax.ShapeDtypeStruct((B, L, H * HG_DV), BF16),
        scratch_shapes=[pltpu.VMEM((2, L, LANES), F32), pltpu.VMEM((2, L, LANES), F32),
                        pltpu.VMEM((2, L, LANES), F32),
                        pltpu.VMEM((n_chunks, LANES, CHUNK), F32),
                        pltpu.VMEM((2 * n_chunks, 8, LANES), F32),
                        pltpu.VMEM((L, LANES), F32), pltpu.VMEM((L, LANES), F32)],
        compiler_params=_cparams(("parallel", "parallel")),
    )(p, p, p, p, p, llb, l1m, oml, gain.reshape(1, LANES))


def _rot_cols(w):
    q = MLA_ROPE // 4
    return jnp.concatenate([-w[..., q:2 * q], w[..., 0:q], -w[..., 3 * q:4 * q], w[..., 2 * q:3 * q]], -1)


def _even_w_in(w):
    d = w.shape[0]
    qkv_z = GDN_HEADS * (2 * GDN_DK + GDN_DV) + GDN_HEADS * GDN_DV
    ab = w[:, qkv_z:qkv_z + 4 * GDN_HEADS]
    o = qkv_z + 4 * GDN_HEADS
    lora = w[:, o:o + MLA_Q_LORA + MLA_KV_LORA]
    kr = w[:, o + MLA_Q_LORA + MLA_KV_LORA:]
    pad = jnp.zeros((d, LANES - 4 * GDN_HEADS), w.dtype)
    return jnp.concatenate([w[:, :qkv_z], lora, kr, _rot_cols(kr), ab, pad], axis=1).astype(BF16)


def _mla_wq(w):
    w = w.reshape(w.shape[0], MLA_HEADS, MLA_NOPE + MLA_ROPE)
    rope = w[..., MLA_NOPE:]
    return jnp.concatenate([w[..., :MLA_NOPE], rope, _rot_cols(rope)], -1).reshape(w.shape[0], -1).astype(BF16)


def _rope_table(T, S):
    half = MLA_ROPE // 4
    inv = ROPE_THETA ** (-jnp.arange(half, dtype=F32) / half)
    pos = jnp.arange(S)
    ang_r = (pos // GRID_W).astype(F32)[:, None] * inv
    ang_c = (pos % GRID_W).astype(F32)[:, None] * inv
    ang = jnp.concatenate([ang_r, ang_r, ang_c, ang_c], -1)
    lat = jnp.concatenate([jnp.cos(ang), jnp.sin(ang)], -1)
    ctx = jnp.concatenate([jnp.ones((T, MLA_ROPE), F32), jnp.zeros((T, MLA_ROPE), F32)], -1)
    return jnp.concatenate([ctx, lat], 0)


def kernel(x, c, ctx, c_ctx, w_mod, b_mod, ln_g, ln_b, even_w_in, gdn_conv, gdn_a_log, gdn_dt_bias, gdn_norm,
           mla_q_norm, mla_kv_norm, mla_w_uq, mla_w_ukv, even_w_out, odd_w_in, na_rpb, hg_lb, hg_norm, odd_w_out,
           moe_router, moe_w_gate, moe_w_up, moe_w_down):
    B, S, D = x.shape
    T = ctx.shape[1]
    L = T + S
    tm = min(256, T)
    depth = w_mod.shape[0]

    xx = jnp.concatenate([ctx, x], axis=1)
    cc = jnp.concatenate([c, c_ctx[None, :], jnp.zeros((16 - B - 1, D), F32)], axis=0)
    mods = _modulation(cc, w_mod, b_mod)
    mods = mods.reshape(depth, 16, N_MOD, D)
    tabs = jnp.stack([jnp.broadcast_to(mods[:, B:B + 1], (depth, B, N_MOD, D)), mods[:, :B]], axis=2)
    tabs = jnp.transpose(tabs, (0, 3, 1, 2, 4))[:, :, :, :, None, :]

    lb_all = jnp.cumsum(jax.nn.softmax(hg_lb.astype(F32), axis=0), axis=0)
    lb_all = lb_all - lb_all[:1]
    cs = _rope_table(T, S)
    rows = S // GRID_W
    kh = min(NA_KH, rows)

    for layer in range(depth):
        j = layer // 2
        tab = tabs[layer]
        if layer % 2 == 0:
            p = _inproj(xx, tab[0], tab[1], _even_w_in(even_w_in[j]), tm, T)
            mix_a = _gdn(p, gdn_conv[j], (-jnp.exp(gdn_a_log[j].astype(F32))).reshape(-1),
                         gdn_dt_bias[j].astype(F32).reshape(-1), gdn_norm[j], T)
            q, k, v = _mlaproj(p, cs, mla_q_norm[j], mla_kv_norm[j], _mla_wq(mla_w_uq[j]),
                               mla_w_ukv[j].astype(BF16), tm)
            mix_b = _mla_attn(q, k, v, T, tm)
            w_out = even_w_out[j]
        else:
            p = _inproj(xx, tab[0], tab[1], odd_w_in[j].astype(BF16), tm, T)
            mix_a = _na(p, _na_bias_table(na_rpb[j], rows, kh), T)
            lb = lb_all[j]
            mix_b = _hgrn(p, jnp.log(lb), jnp.log1p(-lb), 1.0 - lb, hg_norm[j], T)
            w_out = odd_w_out[j]
        xx, aff = _outproj(mix_a, mix_b, w_out.astype(BF16), xx, tab[2], ln_g[layer, 0], ln_b[layer, 0],
                           tab[3], tab[4], moe_router[layer].T, tm, T)
        idx, gate, cap_c, cap = _topk(aff, T)
        xe = _gather(idx, xx, tab[3], tab[4], cap_c, cap)
        m_rows = B * cap
        ye = _experts(xe.reshape(N_EXPERTS, m_rows, D), moe_w_gate[layer], moe_w_up[layer], moe_w_down[layer],
                      m_rows // 4 if m_rows % 64 == 0 else m_rows)
        y = _scatter(idx, gate, ye.reshape(N_EXPERTS, B, cap, D), L)
        xx = _resln(xx, y, tab[5], ln_g[layer, 1], ln_b[layer, 1], tm, T)
    return xx[:, T:, :]
```

```python
import functools
import math

import jax
import jax.numpy as jnp
from jax import lax
from jax.experimental import pallas as pl
from jax.experimental.pallas import tpu as pltpu

F32 = jnp.float32
BF16 = jnp.bfloat16
HIGHEST = lax.Precision.HIGHEST

DEPTH = 4
GRID_W = 64
N_MOD = 6
EPS = 1e-6
DEEPNORM_ALPHA = (2.0 * DEPTH) ** 0.25

GDN_HEADS = 4
GDN_DK = 128
GDN_DV = 128
GDN_CONV = 5
MLA_HEADS = 4
MLA_Q_LORA = 256
MLA_KV_LORA = 256
MLA_NOPE = 128
MLA_ROPE = 64
MLA_DV = 128
ROPE_THETA = 10000.0
NA_HEADS = 8
NA_DH = 64
NA_KH = 8
NA_KW = 16
HG_HEADS = 4
HG_DK = 128
HG_DV = 128
N_EXPERTS = 16
EC_CAPACITY = 2

CHUNK = 64
NA_GROUP = 4
PREP_CHUNKS = 4
SCAN_UNROLL = 4
SCATTER_GROUP = 8
LANES = 128
MLA_QK_PAD = 256
NEG_BIG = -1e30
VMEM_LIMIT = 56 * 1024 * 1024

_NT = (((1,), (1,)), ((), ()))


def _cparams(sem):
    return pltpu.CompilerParams(dimension_semantics=sem, vmem_limit_bytes=VMEM_LIMIT)


def _mm(a, b):
    return jnp.dot(a.astype(BF16), b.astype(BF16), preferred_element_type=F32)


def _mm_nt(a, b):
    return lax.dot_general(a.astype(BF16), b.astype(BF16), _NT, preferred_element_type=F32)


def _mm32(a, b):
    return jnp.dot(a, b, precision=HIGHEST, preferred_element_type=F32)


def _mm32_nt(a, b):
    return lax.dot_general(a, b, _NT, precision=HIGHEST, preferred_element_type=F32)


def _silu(x):
    return x * jax.nn.sigmoid(x)


def _rms(x):
    return x * lax.rsqrt(jnp.mean(x * x, -1, keepdims=True) + EPS)


def _ln(y, g, b):
    yc = y - jnp.mean(y, -1, keepdims=True)
    return yc * lax.rsqrt(jnp.mean(yc * yc, -1, keepdims=True) + EPS) * g + b


def _mod_kernel(c_ref, w_ref, b_ref, o_ref):
    o_ref[...] = _mm32(_silu(c_ref[...]), w_ref[...]) + b_ref[...]


def _modulation(cc, w_mod, b_mod):
    depth, d, n = w_mod.shape
    tn = n // 4
    return pl.pallas_call(
        _mod_kernel,
        grid=(depth, n // tn),
        in_specs=[pl.BlockSpec((cc.shape[0], d), lambda l, j: (0, 0)),
                  pl.BlockSpec((None, d, tn), lambda l, j: (l, 0, j)),
                  pl.BlockSpec((None, 1, tn), lambda l, j: (l, 0, j))],
        out_specs=pl.BlockSpec((None, cc.shape[0], tn), lambda l, j: (l, 0, j)),
        out_shape=jax.ShapeDtypeStruct((depth, cc.shape[0], n), F32),
        compiler_params=_cparams(("parallel", "parallel")),
    )(cc, w_mod, b_mod.reshape(depth, 1, n))


def _inproj_kernel(x_ref, sh_ref, sc_ref, w_ref, o_ref):
    u = x_ref[...] * (1.0 + sc_ref[...]) + sh_ref[...]
    o_ref[...] = jnp.dot(u.astype(BF16), w_ref[...], preferred_element_type=F32)


def _tab_spec(d, tps, nct):
    return pl.BlockSpec((None, None, 1, d), lambda i: (i // tps, jnp.where(i % tps >= nct, 1, 0), 0, 0))


def _inproj(xx, sh, sc, w, tm, T):
    B, L, D = xx.shape
    n = w.shape[1]
    tps, nct = L // tm, T // tm
    out = pl.pallas_call(
        _inproj_kernel,
        grid=(B * tps,),
        in_specs=[pl.BlockSpec((tm, D), lambda i: (i, 0)),
                  _tab_spec(D, tps, nct), _tab_spec(D, tps, nct),
                  pl.BlockSpec((D, n), lambda i: (0, 0))],
        out_specs=pl.BlockSpec((tm, n), lambda i: (i, 0)),
        out_shape=jax.ShapeDtypeStruct((B * L, n), F32),
        compiler_params=_cparams(("parallel",)),
    )(xx.reshape(B * L, D), sh, sc, w)
    return out.reshape(B, L, n)


def _scan_masks(d):
    ri = lax.broadcasted_iota(jnp.int32, (CHUNK, CHUNK), 0)
    ci = lax.broadcasted_iota(jnp.int32, (CHUNK, CHUNK), 1)
    if d == 0:
        return ci <= ri, ci < ri, ri <= ci, ci == ri
    return ci >= ri, ci > ri, ri >= ci, ci == ri


def _bwd_chunk(i, nc, n):
    return jnp.where(i < nc, nc - 1 - i, n - 1 - (i - nc))


def _gdn_kernel(nega_ref, dtb_ref, q_ref, k_ref, v_ref, z_ref, ab_ref, cwq_ref, cwk_ref, cwv_ref, gain_ref,
                o_ref,
                pad_ref, qn_ref, kn_ref, vn_ref, ncs_ref, kw_ref, ou_ref, qp_ref, el_ref, of_ref, ob_ref,
                *, T, L):
    h = pl.program_id(1)
    n_chunks = L // CHUNK
    nc = T // CHUNK

    def conv_prep(src_ref, cw_ref, dst_ref, post):
        zero8 = jnp.zeros((8, LANES), F32)
        pad_ref[0:8, :] = zero8
        pad_ref[8:8 + T, :] = src_ref[0:T, :]
        pad_ref[8 + T:16 + T, :] = zero8
        pad_ref[16 + T:16 + L, :] = src_ref[T:L, :]
        pad_ref[16 + L:24 + L, :] = zero8
        cw = cw_ref[...]

        def body(n, carry):
            r0 = pl.multiple_of(n * CHUNK, CHUNK)
            p0 = r0 + jnp.where(r0 < T, 8, 16) - GDN_CONV // 2
            acc = jnp.zeros((CHUNK, LANES), F32)
            for j in range(GDN_CONV):
                acc = acc + pad_ref[pl.ds(p0 + j, CHUNK), :] * cw[j:j + 1, :]
            dst_ref[pl.ds(r0, CHUNK), :] = post(_silu(acc))
            return carry

        lax.fori_loop(0, n_chunks, body, 0, unroll=PREP_CHUNKS)

    def l2n(y):
        return y * lax.rsqrt(jnp.sum(y * y, -1, keepdims=True) + EPS)

    conv_prep(q_ref, cwq_ref, qn_ref, lambda y: l2n(y) * (GDN_DK ** -0.5))
    conv_prep(k_ref, cwk_ref, kn_ref, l2n)
    conv_prep(v_ref, cwv_ref, vn_ref, lambda y: y)

    lane = lax.broadcasted_iota(jnp.int32, (CHUNK, LANES), 1)

    def chunk_prep(i, carry):
        chains = []
        for c in range(PREP_CHUNKS):
            n = PREP_CHUNKS * i + c
            rows = pl.ds(pl.multiple_of(n * CHUNK, CHUNK), CHUNK)
            q, k, v, ab = qn_ref[rows, :], kn_ref[rows, :], vn_ref[rows, :], ab_ref[rows, :]
            for d in range(2):
                chains.append(dict(n=n, d=d, rows=rows, q=q, k=k, v=v, ab=ab))
        for ch in chains:
            d = ch["d"]
            incl, strict, incl_t, eye = _scan_masks(d)
            a_col = jnp.sum(jnp.where(lane == 4 * d + h, ch["ab"], 0.0), axis=1, keepdims=True)
            b_col = jnp.sum(jnp.where(lane == 8 + 4 * d + h, ch["ab"], 0.0), axis=1, keepdims=True)
            g_col = nega_ref[4 * d + h] * jax.nn.softplus(a_col + dtb_ref[4 * d + h])
            beta = jax.nn.sigmoid(b_col)
            g_row = jnp.sum(jnp.where(eye, g_col, 0.0), axis=0, keepdims=True)
            gc_col = jnp.sum(jnp.where(incl, g_row, 0.0), axis=1, keepdims=True)
            gc_row = jnp.sum(jnp.where(incl_t, g_col, 0.0), axis=0, keepdims=True)
            g_tot = jnp.sum(g_col, axis=0, keepdims=True)
            ch["decay"] = jnp.where(incl, jnp.exp(jnp.where(incl, gc_col - gc_row, 0.0)), 0.0)
            ch["kb"] = ch["k"] * beta
            ch["vb"] = ch["v"] * beta
            ch["e_col"] = jnp.exp(gc_col)
            ch["kd"] = ch["k"] * jnp.exp(g_tot - gc_col)
            ch["e_tot"] = jnp.exp(g_tot)
        for ch in chains:
            _, strict, _, eye = _scan_masks(ch["d"])
            a = jnp.where(strict, _mm_nt(ch["kb"], ch["k"]) * ch["decay"], 0.0)
            ch["p"] = jnp.where(eye, 1.0, 0.0) - a
            ch["bp"] = a
        for _ in range(int(math.log2(CHUNK)) - 1):
            for ch in chains:
                ch["bp"] = _mm(ch["bp"], ch["bp"])
            for ch in chains:
                ch["p"] = ch["p"] + _mm(ch["p"], ch["bp"])
        for ch in chains:
            incl = _scan_masks(ch["d"])[0]
            ch["uw"] = _mm(ch["p"], jnp.concatenate([ch["vb"], ch["kb"] * ch["e_col"]], axis=1)).astype(BF16)
            ch["qk"] = jnp.where(incl, _mm_nt(ch["q"], ch["k"]) * ch["decay"], 0.0)
        for ch in chains:
            d, n, rows = ch["d"], ch["n"], ch["rows"]
            kd_uw = _mm(ch["kd"].T, ch["uw"])
            qk_uw = _mm(ch["qk"], ch["uw"])
            ncs_ref[d * n_chunks + n] = kd_uw[:, :GDN_DV]
            kw_ref[d * n_chunks + n] = kd_uw[:, GDN_DV:].astype(BF16)
            ou_ref[d, rows, :] = qk_uw[:, :GDN_DV]
            qp_ref[d, rows, :] = (ch["q"] * ch["e_col"] - qk_uw[:, GDN_DV:]).astype(BF16)
            el_ref[d * n_chunks + n] = jnp.broadcast_to(ch["e_tot"], (8, LANES))
        return carry

    lax.fori_loop(0, n_chunks // PREP_CHUNKS, chunk_prep, 0)

    def step(d, n, s):
        rows = pl.ds(pl.multiple_of(n * CHUNK, CHUNK), CHUNK)
        sb = s.astype(BF16)
        o = jnp.dot(qp_ref[d, rows, :], sb, preferred_element_type=F32) + ou_ref[d, rows, :]
        s = (s * el_ref[d * n_chunks + n][0:1, :] + ncs_ref[d * n_chunks + n]
             - jnp.dot(kw_ref[d * n_chunks + n], sb, preferred_element_type=F32))
        return s, o, rows

    def scan_body(i, carry):
        sf, sb = carry
        sf, o_f, rows_f = step(0, i, sf)
        sb, o_b, rows_b = step(1, _bwd_chunk(i, nc, n_chunks), sb)
        of_ref[rows_f, :] = o_f
        ob_ref[rows_b, :] = o_b
        return sf, sb

    zero = jnp.zeros((GDN_DK, GDN_DV), F32)
    lax.fori_loop(0, n_chunks, scan_body, (zero, zero), unroll=SCAN_UNROLL)

    o = of_ref[...] + ob_ref[...]
    o_ref[...] = (_rms(o) * gain_ref[...] * _silu(z_ref[...])).astype(BF16)


def _gdn(p, conv_w, neg_a, dt_bias, gain, T):
    B, L, _ = p.shape
    H = GDN_HEADS
    n_chunks = L // CHUNK
    col = lambda off: pl.BlockSpec((None, L, LANES), lambda b, h: (b, 0, off + h))
    cw = lambda off: pl.BlockSpec((GDN_CONV, LANES), lambda b, h: (0, off + h))
    smem = pl.BlockSpec(memory_space=pltpu.SMEM)
    return pl.pallas_call(
        functools.partial(_gdn_kernel, T=T, L=L),
        grid=(B, H),
        in_specs=[smem, smem, col(0), col(H), col(2 * H), col(3 * H),
                  pl.BlockSpec((None, L, LANES), lambda b, h: (b, 0, 21)),
                  cw(0), cw(H), cw(2 * H),
                  pl.BlockSpec((1, LANES), lambda b, h: (0, 0))],
        out_specs=pl.BlockSpec((None, L, LANES), lambda b, h: (b, 0, h)),
        out_shape=jax.ShapeDtypeStruct((B, L, H * GDN_DV), BF16),
        scratch_shapes=[pltpu.VMEM((L + 24, LANES), F32),
                        pltpu.VMEM((L, LANES), F32), pltpu.VMEM((L, LANES), F32), pltpu.VMEM((L, LANES), F32),
                        pltpu.VMEM((2 * n_chunks, GDN_DK, GDN_DV), F32),
                        pltpu.VMEM((2 * n_chunks, GDN_DK, GDN_DK), BF16),
                        pltpu.VMEM((2, L, LANES), F32), pltpu.VMEM((2, L, LANES), BF16),
                        pltpu.VMEM((2 * n_chunks, 8, LANES), F32),
                        pltpu.VMEM((L, LANES), F32), pltpu.VMEM((L, LANES), F32)],
        compiler_params=_cparams(("parallel", "parallel")),
    )(neg_a, dt_bias, p, p, p, p, p, conv_w, conv_w, conv_w, gain.reshape(1, LANES))


def _mlaproj_kernel(ql_ref, kvl_ref, kr_ref, cs_ref, qg_ref, kvg_ref, wq_ref, wkv_ref, q_ref, k_ref, v_ref):
    q = jnp.dot((_rms(ql_ref[...]) * qg_ref[...]).astype(BF16), wq_ref[...], preferred_element_type=F32)
    kv = jnp.dot((_rms(kvl_ref[...]) * kvg_ref[...]).astype(BF16), wkv_ref[...], preferred_element_type=F32)
    cs = cs_ref[...]
    lane = lax.broadcasted_iota(jnp.int32, cs.shape, 1)

    def rope(blk):
        t = blk * cs
        return jnp.where(lane < MLA_ROPE, t + pltpu.roll(t, MLA_ROPE, 1), 0.0)

    kr = rope(kr_ref[...]).astype(BF16)
    for h in range(MLA_HEADS):
        o = h * MLA_QK_PAD
        q_ref[:, o:o + MLA_NOPE] = q[:, o:o + MLA_NOPE].astype(BF16)
        q_ref[:, o + MLA_NOPE:o + MLA_QK_PAD] = rope(q[:, o + MLA_NOPE:o + MLA_QK_PAD]).astype(BF16)
        k_ref[:, o:o + MLA_NOPE] = kv[:, o:o + MLA_NOPE].astype(BF16)
        k_ref[:, o + MLA_NOPE:o + MLA_QK_PAD] = kr
        v_ref[:, h * MLA_DV:(h + 1) * MLA_DV] = kv[:, o + MLA_NOPE:o + MLA_QK_PAD].astype(BF16)


def _mlaproj(p, cs, q_gain, kv_gain, wq, wkv, tm):
    B, L, n = p.shape
    tps = L // tm
    p2 = p.reshape(B * L, n)
    hq = MLA_HEADS * MLA_QK_PAD
    q, k, v = pl.pallas_call(
        _mlaproj_kernel,
        grid=(B * tps,),
        in_specs=[pl.BlockSpec((tm, MLA_Q_LORA), lambda i: (i, 8)),
                  pl.BlockSpec((tm, MLA_KV_LORA), lambda i: (i, 9)),
                  pl.BlockSpec((tm, LANES), lambda i: (i, 20)),
                  pl.BlockSpec((tm, LANES), lambda i: (i % tps, 0)),
                  pl.BlockSpec((1, MLA_Q_LORA), lambda i: (0, 0)),
                  pl.BlockSpec((1, MLA_KV_LORA), lambda i: (0, 0)),
                  pl.BlockSpec((MLA_Q_LORA, hq), lambda i: (0, 0)),
                  pl.BlockSpec((MLA_KV_LORA, hq), lambda i: (0, 0))],
        out_specs=[pl.BlockSpec((tm, hq), lambda i: (i, 0)),
                   pl.BlockSpec((tm, hq), lambda i: (i, 0)),
                   pl.BlockSpec((tm, MLA_HEADS * MLA_DV), lambda i: (i, 0))],
        out_shape=[jax.ShapeDtypeStruct((B * L, hq), BF16),
                   jax.ShapeDtypeStruct((B * L, hq), BF16),
                   jax.ShapeDtypeStruct((B * L, MLA_HEADS * MLA_DV), BF16)],
        compiler_params=_cparams(("parallel",)),
    )(p2, p2, p2, cs, q_gain.reshape(1, -1), kv_gain.reshape(1, -1), wq, wkv)
    return q.reshape(B, L, hq), k.reshape(B, L, hq), v.reshape(B, L, -1)


def _softmax_pv(s, v):
    m = jnp.max(s, axis=-1, keepdims=True)
    p = jnp.exp(s - m)
    l = jnp.sum(p, axis=-1, keepdims=True)
    return jnp.dot(p.astype(BF16), v, preferred_element_type=F32) / l


def _mla_attn_kernel(q_ref, k_ref, v_ref, o_ref, *, T, nct, scale):
    qt = pl.program_id(2)
    q = q_ref[...]

    @pl.when(qt < nct)
    def _():
        s = lax.dot_general(q, k_ref[0:T, :], _NT, preferred_element_type=F32) * scale
        o_ref[...] = _softmax_pv(s, v_ref[0:T, :]).astype(BF16)

    @pl.when(qt >= nct)
    def _():
        s = lax.dot_general(q, k_ref[...], _NT, preferred_element_type=F32) * scale
        o_ref[...] = _softmax_pv(s, v_ref[...]).astype(BF16)


def _mla_attn(q, k, v, T, tq):
    B, L, _ = q.shape
    H = MLA_HEADS
    scale = (MLA_NOPE + MLA_ROPE) ** -0.5
    return pl.pallas_call(
        functools.partial(_mla_attn_kernel, T=T, nct=T // tq, scale=scale),
        grid=(B, H, L // tq),
        in_specs=[pl.BlockSpec((None, tq, MLA_QK_PAD), lambda b, h, i: (b, i, h)),
                  pl.BlockSpec((None, L, MLA_QK_PAD), lambda b, h, i: (b, 0, h)),
                  pl.BlockSpec((None, L, MLA_DV), lambda b, h, i: (b, 0, h))],
        out_specs=pl.BlockSpec((None, tq, MLA_DV), lambda b, h, i: (b, i, h)),
        out_shape=jax.ShapeDtypeStruct((B, L, H * MLA_DV), BF16),
        compiler_params=_cparams(("parallel", "parallel", "arbitrary")),
    )(q, k, v)


def _outproj_kernel(a_ref, b_ref, w_ref, x_ref, gate_ref, lng_ref, lnb_ref, sh_ref, sc_ref, wr_ref,
                    xo_ref, aff_ref):
    ka = a_ref.shape[1]
    o = (jnp.dot(a_ref[...], w_ref[0:ka, :], preferred_element_type=F32)
         + jnp.dot(b_ref[...], w_ref[ka:, :], preferred_element_type=F32))
    xn = _ln(DEEPNORM_ALPHA * x_ref[...] + gate_ref[...] * o, lng_ref[...], lnb_ref[...])
    xo_ref[...] = xn
    hmod = xn * (1.0 + sc_ref[...]) + sh_ref[...]
    logits = _mm_nt(wr_ref[...], hmod)
    e = jnp.exp(logits - jnp.max(logits, axis=0, keepdims=True))
    aff_ref[...] = e / jnp.sum(e, axis=0, keepdims=True)


def _outproj(a, b, w, xx, gate, ln_g, ln_b, sh, sc, wr_t, tm, T):
    B, L, D = xx.shape
    tps, nct = L // tm, T // tm
    ka, kb = a.shape[-1], b.shape[-1]
    row = lambda n: pl.BlockSpec((1, n), lambda i: (0, 0))
    xo, aff = pl.pallas_call(
        _outproj_kernel,
        grid=(B * tps,),
        in_specs=[pl.BlockSpec((tm, ka), lambda i: (i, 0)),
                  pl.BlockSpec((tm, kb), lambda i: (i, 0)),
                  pl.BlockSpec((ka + kb, D), lambda i: (0, 0)),
                  pl.BlockSpec((tm, D), lambda i: (i, 0)),
                  _tab_spec(D, tps, nct), row(D), row(D),
                  _tab_spec(D, tps, nct), _tab_spec(D, tps, nct),
                  pl.BlockSpec((N_EXPERTS, D), lambda i: (0, 0))],
        out_specs=[pl.BlockSpec((tm, D), lambda i: (i, 0)),
                   pl.BlockSpec((None, N_EXPERTS, tm), lambda i: (i // tps, 0, i % tps))],
        out_shape=[jax.ShapeDtypeStruct((B * L, D), F32),
                   jax.ShapeDtypeStruct((B, N_EXPERTS, L), F32)],
        compiler_params=_cparams(("parallel",)),
    )(a.reshape(B * L, ka), b.reshape(B * L, kb), w, xx.reshape(B * L, D), gate,
      ln_g.reshape(1, D), ln_b.reshape(1, D), sh, sc, wr_t)
    return xo.reshape(B, L, D), aff


def _topk_kernel(aff_ref, idx_ref, gate_ref, *, T, cap_c, cap_l):
    ri = lax.broadcasted_iota(jnp.int32, (LANES, LANES), 0)
    ci = lax.broadcasted_iota(jnp.int32, (LANES, LANES), 1)
    tri = jnp.where(ri <= ci, 1.0, 0.0).astype(BF16)

    def prefix_incl(x):
        carry = jnp.zeros((x.shape[0], 1), F32)
        outs = []
        for j in range(x.shape[1] // LANES):
            pj = jnp.dot(x[:, j * LANES:(j + 1) * LANES].astype(BF16), tri, preferred_element_type=F32) + carry
            outs.append(pj)
            carry = pj[:, LANES - 1:LANES]
        return jnp.concatenate(outs, axis=1)

    def select(aff, cap, base, row0):
        ne, n = aff.shape
        def bisect(_, lh):
            lo, hi = lh
            mid = lo + lax.shift_right_logical(hi - lo, 1)
            cnt = jnp.sum(jnp.where(aff >= pltpu.bitcast(mid, F32), 1.0, 0.0), axis=1, keepdims=True)
            ge = cnt >= float(cap)
            return jnp.where(ge, mid, lo), jnp.where(ge, hi, mid)

        lo0 = jnp.zeros((ne, 1), jnp.int32)
        hi0 = jnp.full((ne, 1), 0x7F800000, jnp.int32)
        thr_bits, _ = lax.fori_loop(0, 31, bisect, (lo0, hi0))
        thr = pltpu.bitcast(thr_bits, F32)
        gt = aff > thr
        eqf = jnp.where(aff == thr, 1.0, 0.0)
        need = float(cap) - jnp.sum(jnp.where(gt, 1.0, 0.0), axis=1, keepdims=True)
        eq_before = prefix_incl(eqf) - eqf
        self_ = jnp.where(gt, 1.0, jnp.where(eq_before < need, eqf, 0.0))
        cnt = prefix_incl(self_)
        slot = lax.broadcasted_iota(jnp.int32, (cap, 1), 0).astype(F32)
        lane = lax.broadcasted_iota(jnp.int32, (cap, LANES), 1)
        idx_acc = jnp.zeros((cap, LANES), F32)
        gate_acc = jnp.zeros((cap, LANES), F32)
        for e in range(ne):
            ce, se, ae = cnt[e:e + 1, :], self_[e:e + 1, :], aff[e:e + 1, :]
            idx_e = jnp.sum(jnp.where(ce <= slot, 1.0, 0.0), axis=1, keepdims=True)
            hit = jnp.where(ce == slot + 1.0, se, 0.0)
            gate_e = jnp.sum(hit * ae, axis=1, keepdims=True)
            idx_acc = jnp.where(lane == e, idx_e + float(base), idx_acc)
            gate_acc = jnp.where(lane == e, gate_e, gate_acc)
        idx_ref[row0:row0 + cap, :] = idx_acc.astype(jnp.int32)
        gate_ref[row0:row0 + cap, :] = gate_acc

    aff = aff_ref[...]
    select(aff[:, 0:T], cap_c, 0, 0)
    select(aff[:, T:], cap_l, T, cap_c)


def _topk(aff, T):
    B, E, L = aff.shape
    cap_c = EC_CAPACITY * T // E
    cap_l = EC_CAPACITY * (L - T) // E
    cap = cap_c + cap_l
    idx, gate = pl.pallas_call(
        functools.partial(_topk_kernel, T=T, cap_c=cap_c, cap_l=cap_l),
        grid=(B,),
        in_specs=[pl.BlockSpec((None, E, L), lambda b: (b, 0, 0))],
        out_specs=[pl.BlockSpec((None, cap, LANES), lambda b: (b, 0, 0)),
                   pl.BlockSpec((None, cap, LANES), lambda b: (b, 0, 0))],
        out_shape=[jax.ShapeDtypeStruct((B, cap, LANES), jnp.int32),
                   jax.ShapeDtypeStruct((B, cap, LANES), F32)],
        compiler_params=_cparams(("parallel",)),
    )(aff)
    idx = jnp.swapaxes(idx[:, :, :E], 1, 2).reshape(B, E * cap)
    gate = jnp.swapaxes(gate[:, :, :E], 1, 2).reshape(B, E * cap)
    return idx, gate, cap_c, cap


def _gather_kernel(idx_ref, x_ref, sh_ref, sc_ref, o_ref, stage_ref, *, cap_c, cap):
    b = pl.program_id(0)
    base = pl.program_id(1) * cap

    def body(s, carry):
        r = idx_ref[b, base + s]
        stage_ref[pl.ds(s, 1), :] = x_ref[pl.ds(r, 1), :]
        return carry

    lax.fori_loop(0, cap, body, 0, unroll=8)
    o_ref[0:cap_c, :] = (stage_ref[0:cap_c, :] * (1.0 + sc_ref[0]) + sh_ref[0]).astype(BF16)
    o_ref[cap_c:, :] = (stage_ref[cap_c:, :] * (1.0 + sc_ref[1]) + sh_ref[1]).astype(BF16)


def _gather(idx, xx, sh, sc, cap_c, cap):
    B, L, D = xx.shape
    E = N_EXPERTS
    tab = pl.BlockSpec((None, 2, 1, D), lambda b, e, idx: (b, 0, 0, 0))
    return pl.pallas_call(
        functools.partial(_gather_kernel, cap_c=cap_c, cap=cap),
        grid_spec=pltpu.PrefetchScalarGridSpec(
            num_scalar_prefetch=1, grid=(B, E),
            in_specs=[pl.BlockSpec((None, L, D), lambda b, e, idx: (b, 0, 0)), tab, tab],
            out_specs=pl.BlockSpec((None, None, cap, D), lambda b, e, idx: (e, b, 0, 0)),
            scratch_shapes=[pltpu.VMEM((cap, D), F32)]),
        out_shape=jax.ShapeDtypeStruct((E, B, cap, D), BF16),
        compiler_params=_cparams(("parallel", "arbitrary")),
    )(idx, xx, sh, sc)


def _expert_kernel(x_ref, wg_ref, wu_ref, wd_ref, o_ref):
    x = x_ref[...]
    g = jnp.dot(x, wg_ref[...].astype(BF16), preferred_element_type=F32)
    u = jnp.dot(x, wu_ref[...].astype(BF16), preferred_element_type=F32)
    hid = (_silu(g) * u).astype(BF16)
    o_ref[...] = jnp.dot(hid, wd_ref[...].astype(BF16), preferred_element_type=F32)


def _experts(xe, w_gate, w_up, w_down, layer, tm):
    E, M, D = xe.shape
    F = w_gate.shape[-1]
    return pl.pallas_call(
        _expert_kernel,
        grid=(E, M // tm),
        in_specs=[pl.BlockSpec((None, tm, D), lambda e, m: (e, m, 0)),
                  pl.BlockSpec((None, None, D, F), lambda e, m: (layer, e, 0, 0)),
                  pl.BlockSpec((None, None, D, F), lambda e, m: (layer, e, 0, 0)),
                  pl.BlockSpec((None, None, F, D), lambda e, m: (layer, e, 0, 0))],
        out_specs=pl.BlockSpec((None, tm, D), lambda e, m: (e, m, 0)),
        out_shape=jax.ShapeDtypeStruct((E, M, D), F32),
        compiler_params=_cparams(("parallel", "arbitrary")),
    )(xe, w_gate, w_up, w_down)


def _scatter_kernel(idx_ref, gate_ref, ye_ref, y_ref, *, cap):
    b = pl.program_id(0)
    e = pl.program_id(1)

    @pl.when(e == 0)
    def _():
        y_ref[...] = jnp.zeros(y_ref.shape, F32)

    def body(i, carry):
        s0 = pl.multiple_of(i * SCATTER_GROUP, SCATTER_GROUP)
        ye = ye_ref[pl.ds(s0, SCATTER_GROUP), :]
        rs = [idx_ref[b, e * cap + s0 + j] for j in range(SCATTER_GROUP)]
        new = [y_ref[pl.ds(rs[j], 1), :] + gate_ref[b, e * cap + s0 + j] * ye[j:j + 1, :]
               for j in range(SCATTER_GROUP)]
        for j in range(SCATTER_GROUP):
            y_ref[pl.ds(rs[j], 1), :] = new[j]
        return carry

    lax.fori_loop(0, cap // SCATTER_GROUP, body, 0)


def _scatter(idx, gate, ye, L):
    E, B, cap, D = ye.shape
    return pl.pallas_call(
        functools.partial(_scatter_kernel, cap=cap),
        grid_spec=pltpu.PrefetchScalarGridSpec(
            num_scalar_prefetch=2, grid=(B, E),
            in_specs=[pl.BlockSpec((None, None, cap, D), lambda b, e, i, g: (e, b, 0, 0))],
            out_specs=pl.BlockSpec((None, L, D), lambda b, e, i, g: (b, 0, 0))),
        out_shape=jax.ShapeDtypeStruct((B, L, D), F32),
        compiler_params=_cparams(("parallel", "arbitrary")),
    )(idx, gate, ye)


def _resln_kernel(x_ref, y_ref, gate_ref, lng_ref, lnb_ref, o_ref):
    o_ref[...] = _ln(DEEPNORM_ALPHA * x_ref[...] + gate_ref[...] * y_ref[...], lng_ref[...], lnb_ref[...])


def _resln(xx, y, gate, ln_g, ln_b, tm, T):
    B, L, D = xx.shape
    tps, nct = L // tm, T // tm
    row = pl.BlockSpec((1, D), lambda i: (0, 0))
    blk = pl.BlockSpec((tm, D), lambda i: (i, 0))
    out = pl.pallas_call(
        _resln_kernel,
        grid=(B * tps,),
        in_specs=[blk, blk, _tab_spec(D, tps, nct), row, row],
        out_specs=blk,
        out_shape=jax.ShapeDtypeStruct((B * L, D), F32),
        compiler_params=_cparams(("parallel",)),
    )(xx.reshape(B * L, D), y.reshape(B * L, D), gate, ln_g.reshape(1, D), ln_b.reshape(1, D))
    return out.reshape(B, L, D)


def _na_plan(rows, kh):
    band = min(kh + NA_GROUP - 1, rows)
    plan, keys = [], []
    for g in range(rows // NA_GROUP):
        q0 = NA_GROUP * g
        win = [min(max(q0 + a - kh // 2, 0), rows - kh) for a in range(NA_GROUP)]
        ub = min(win[0], rows - band)
        key = (ub - q0, tuple(w - q0 for w in win))
        if key not in keys:
            keys.append(key)
        plan.append((ub, keys.index(key)))
    return band, plan, keys


def _na_kernel(plan_ref, q_ref, k_ref, v_ref, bias_ref, o_ref, *, T, nct, band, scale):
    i = pl.program_id(2)
    q = q_ref[...]
    lane = lax.broadcasted_iota(jnp.int32, q.shape, 1)
    kc = k_ref[0:T, :].astype(BF16)
    vc = v_ref[0:T, :].astype(BF16)
    qh = [jnp.where(lane < NA_DH, q, 0.0).astype(BF16), jnp.where(lane >= NA_DH, q, 0.0).astype(BF16)]

    @pl.when(i < nct)
    def _():
        outs = [_softmax_pv(lax.dot_general(qh[j], kc, _NT, preferred_element_type=F32) * scale, vc)
                for j in range(2)]
        o_ref[...] = jnp.where(lane < NA_DH, outs[0], outs[1]).astype(BF16)

    @pl.when(i >= nct)
    def _():
        t0 = pl.multiple_of(T + plan_ref[0, i - nct] * GRID_W, GRID_W)
        kb = k_ref[pl.ds(t0, band * GRID_W), :].astype(BF16)
        vb = v_ref[pl.ds(t0, band * GRID_W), :].astype(BF16)
        outs = []
        for j in range(2):
            s_w = lax.dot_general(qh[j], kb, _NT, preferred_element_type=F32) * scale + bias_ref[j]
            s_c = lax.dot_general(qh[j], kc, _NT, preferred_element_type=F32) * scale
            m = jnp.maximum(jnp.max(s_w, axis=-1, keepdims=True), jnp.max(s_c, axis=-1, keepdims=True))
            p_w = jnp.exp(s_w - m)
            p_c = jnp.exp(s_c - m)
            l = jnp.sum(p_w, axis=-1, keepdims=True) + jnp.sum(p_c, axis=-1, keepdims=True)
            outs.append((jnp.dot(p_w.astype(BF16), vb, preferred_element_type=F32)
                         + jnp.dot(p_c.astype(BF16), vc, preferred_element_type=F32)) / l)
        o_ref[...] = jnp.where(lane < NA_DH, outs[0], outs[1]).astype(BF16)


def _na_bias_tables(rpb, band, keys, kh):
    w = jnp.arange(GRID_W)
    c0 = jnp.clip(w - NA_KW // 2, 0, GRID_W - NA_KW)
    col_ok = (w[None, :] >= c0[:, None]) & (w[None, :] < c0[:, None] + NA_KW)
    col_off = jnp.clip(w[None, :] - w[:, None] + (NA_KW - 1), 0, 2 * NA_KW - 2)
    onehot = (col_off[:, :, None] == jnp.arange(2 * NA_KW - 1)).astype(F32)
    base = jnp.einsum('hrm,wcm->hrwc', rpb.astype(F32), onehot, precision=HIGHEST)
    base = jnp.where(col_ok[None, None], base, NEG_BIG)
    masked = jnp.full((NA_HEADS, GRID_W, GRID_W), NEG_BIG, F32)
    tabs = []
    for ub_off, win_offs in keys:
        per_row = []
        for a in range(NA_GROUP):
            blocks = []
            for u in range(band):
                rel = ub_off + u
                inside = win_offs[a] <= rel < win_offs[a] + kh
                blocks.append(base[:, rel - a + NA_KH - 1] if inside else masked)
            per_row.append(jnp.concatenate(blocks, axis=-1))
        tabs.append(jnp.concatenate(per_row, axis=1))
    return jnp.stack(tabs)


def _na(p, rpb, T):
    B, L, _ = p.shape
    rows = (L - T) // GRID_W
    kh = min(NA_KH, rows)
    tq = NA_GROUP * GRID_W
    assert T % tq == 0 and rows % NA_GROUP == 0
    nct = T // tq
    hp = NA_HEADS // 2
    band, plan, keys = _na_plan(rows, kh)
    bias = _na_bias_tables(rpb, band, keys, kh)
    plan = jnp.asarray(plan, jnp.int32).T
    return pl.pallas_call(
        functools.partial(_na_kernel, T=T, nct=nct, band=band, scale=NA_DH ** -0.5),
        grid_spec=pltpu.PrefetchScalarGridSpec(
            num_scalar_prefetch=1, grid=(B, hp, nct + rows // NA_GROUP),
            in_specs=[pl.BlockSpec((None, tq, LANES), lambda b, h, i, pr: (b, i, h)),
                      pl.BlockSpec((None, L, LANES), lambda b, h, i, pr: (b, 0, hp + h)),
                      pl.BlockSpec((None, L, LANES), lambda b, h, i, pr: (b, 0, 2 * hp + h)),
                      pl.BlockSpec((None, 2, tq, band * GRID_W),
                                   lambda b, h, i, pr: (pr[1, jnp.maximum(i - nct, 0)], h, 0, 0))],
            out_specs=pl.BlockSpec((None, tq, LANES), lambda b, h, i, pr: (b, i, h))),
        out_shape=jax.ShapeDtypeStruct((B, L, NA_HEADS * NA_DH), BF16),
        compiler_params=_cparams(("parallel", "parallel", "arbitrary")),
    )(plan, p, p, p, bias)


def _hgrn_kernel(q_ref, zf_ref, zb_ref, v_ref, gz_ref, llb_ref, l1m_ref, oml_ref, gain_ref,
                 o_ref,
                 qg_ref, ut_ref, oi_ref, el_ref, of_ref, ob_ref, *, T, L):
    n_chunks = L // CHUNK
    nc = T // CHUNK
    ri = lax.broadcasted_iota(jnp.int32, (CHUNK, CHUNK), 0)
    ci = lax.broadcasted_iota(jnp.int32, (CHUNK, CHUNK), 1)
    row = lax.broadcasted_iota(jnp.int32, (CHUNK, LANES), 0)
    eye = ri == ci

    def shifted(x, k):
        return x if k == 0 else pltpu.roll(x, (-k) % CHUNK, 0)

    def scan_cumsum(x, d):
        k = 1
        while k < CHUNK:
            if d == 0:
                x = x + jnp.where(row >= k, shifted(x, -k), 0.0)
            else:
                x = x + jnp.where(row < CHUNK - k, shifted(x, k), 0.0)
            k *= 2
        return x

    def block_ref(g, size, r_in):
        if size >= 8:
            g3 = g.reshape(CHUNK // size, size, LANES)
            return jnp.broadcast_to(g3[:, r_in:r_in + 1, :], g3.shape).reshape(CHUNK, LANES)
        pos = row & (size - 1)
        out = shifted(g, r_in)
        for m in range(1, size):
            out = jnp.where(pos == m, shifted(g, r_in - m), out)
        return out

    def chunk_prep(n, carry):
        rows = pl.ds(pl.multiple_of(n * CHUNK, CHUNK), CHUNK)
        qs = _silu(q_ref[rows, :])
        v = v_ref[rows, :]
        vt = v.T
        for d in range(2):
            z = (zf_ref if d == 0 else zb_ref)[rows, :]
            logf = jnp.logaddexp(llb_ref[d:d + 1, :], l1m_ref[d:d + 1, :] + jax.nn.log_sigmoid(z))
            kk = oml_ref[d:d + 1, :] * jax.nn.sigmoid(-z)
            g = scan_cumsum(logf, d)
            g_tot = g[CHUNK - 1:CHUNK, :] if d == 0 else g[0:1, :]
            a = jnp.where(eye, jnp.sum(qs * kk, axis=1, keepdims=True), 0.0)
            half = CHUNK // 2
            while half >= 1:
                size = 2 * half
                e = jnp.exp(-jnp.abs(g - block_ref(g, size, half if d == 0 else half - 1)))
                s_l = _mm_nt(qs * e, kk * e)
                same = (ri & -size) == (ci & -size)
                t_late = (ri & (size - 1)) >= half
                s_late = (ci & (size - 1)) >= half
                pick = (t_late & ~s_late) if d == 0 else (~t_late & s_late)
                a = jnp.where(same & pick, s_l, a)
                half //= 2
            oi_ref[d, rows, :] = _mm(a, v)
            qg_ref[d, rows, :] = (qs * jnp.exp(g)).astype(BF16)
            ut_ref[d * n_chunks + n] = _mm(vt, kk * jnp.exp(g_tot - g))
            el_ref[d * n_chunks + n] = jnp.broadcast_to(jnp.exp(g_tot), (8, LANES))
        return carry

    lax.fori_loop(0, n_chunks, chunk_prep, 0)

    def step(d, n, st):
        rows = pl.ds(pl.multiple_of(n * CHUNK, CHUNK), CHUNK)
        o = _mm_nt(qg_ref[d, rows, :], st) + oi_ref[d, rows, :]
        st = st * el_ref[d * n_chunks + n][0:1, :] + ut_ref[d * n_chunks + n]
        return st, o, rows

    def scan_body(i, carry):
        sf, sb = carry
        sf, o_f, rows_f = step(0, i, sf)
        sb, o_b, rows_b = step(1, _bwd_chunk(i, nc, n_chunks), sb)
        of_ref[rows_f, :] = o_f
        ob_ref[rows_b, :] = o_b
        return sf, sb

    zero = jnp.zeros((HG_DV, HG_DK), F32)
    lax.fori_loop(0, n_chunks, scan_body, (zero, zero), unroll=SCAN_UNROLL)
    o = of_ref[...] + ob_ref[...]
    o_ref[...] = (_rms(o) * gain_ref[...] * _silu(gz_ref[...])).astype(BF16)


def _hgrn(p, llb, l1m, oml, gain, T):
    B, L, _ = p.shape
    H = HG_HEADS
    n_chunks = L // CHUNK
    col = lambda off: pl.BlockSpec((None, L, LANES), lambda b, h: (b, 0, off + h))
    lbs = pl.BlockSpec((2, LANES), lambda b, h: (0, h))
    return pl.pallas_call(
        functools.partial(_hgrn_kernel, T=T, L=L),
        grid=(B, H),
        in_specs=[col(12), col(16), col(20), col(24), col(28), lbs, lbs, lbs,
                  pl.BlockSpec((1, LANES), lambda b, h: (0, 0))],
        out_specs=pl.BlockSpec((None, L, LANES), lambda b, h: (b, 0, h)),
        out_shape=jax.ShapeDtypeStruct((B, L, H * HG_DV), BF16),
        scratch_shapes=[pltpu.VMEM((2, L, LANES), BF16),
                        pltpu.VMEM((2 * n_chunks, HG_DV, HG_DK), F32),
                        pltpu.VMEM((2, L, LANES), F32),
                        pltpu.VMEM((2 * n_chunks, 8, LANES), F32),
                        pltpu.VMEM((L, LANES), F32), pltpu.VMEM((L, LANES), F32)],
        compiler_params=_cparams(("parallel", "parallel")),
    )(p, p, p, p, p, llb, l1m, oml, gain.reshape(1, LANES))


def _rot_cols(w):
    q = MLA_ROPE // 4
    return jnp.concatenate([-w[..., q:2 * q], w[..., 0:q], -w[..., 3 * q:4 * q], w[..., 2 * q:3 * q]], -1)


def _even_w_in(w):
    d = w.shape[0]
    qkv_z = GDN_HEADS * (2 * GDN_DK + GDN_DV) + GDN_HEADS * GDN_DV
    ab = w[:, qkv_z:qkv_z + 4 * GDN_HEADS]
    o = qkv_z + 4 * GDN_HEADS
    lora = w[:, o:o + MLA_Q_LORA + MLA_KV_LORA]
    kr = w[:, o + MLA_Q_LORA + MLA_KV_LORA:]
    pad = jnp.zeros((d, LANES - 4 * GDN_HEADS), w.dtype)
    return jnp.concatenate([w[:, :qkv_z], lora, kr, _rot_cols(kr), ab, pad], axis=1).astype(BF16)


def _mla_wq(w):
    w = w.reshape(w.shape[0], MLA_HEADS, MLA_NOPE + MLA_ROPE)
    rope = w[..., MLA_NOPE:]
    return jnp.concatenate([w[..., :MLA_NOPE], rope, _rot_cols(rope)], -1).reshape(w.shape[0], -1).astype(BF16)


def _rope_table(T, S):
    half = MLA_ROPE // 4
    inv = ROPE_THETA ** (-jnp.arange(half, dtype=F32) / half)
    pos = jnp.arange(S)
    ang_r = (pos // GRID_W).astype(F32)[:, None] * inv
    ang_c = (pos % GRID_W).astype(F32)[:, None] * inv
    ang = jnp.concatenate([ang_r, ang_r, ang_c, ang_c], -1)
    lat = jnp.concatenate([jnp.cos(ang), jnp.sin(ang)], -1)
    ctx = jnp.concatenate([jnp.ones((T, MLA_ROPE), F32), jnp.zeros((T, MLA_ROPE), F32)], -1)
    return jnp.concatenate([ctx, lat], 0)


def kernel(x, c, ctx, c_ctx, w_mod, b_mod, ln_g, ln_b, even_w_in, gdn_conv, gdn_a_log, gdn_dt_bias, gdn_norm,
           mla_q_norm, mla_kv_norm, mla_w_uq, mla_w_ukv, even_w_out, odd_w_in, na_rpb, hg_lb, hg_norm, odd_w_out,
           moe_router, moe_w_gate, moe_w_up, moe_w_down):
    B, S, D = x.shape
    T = ctx.shape[1]
    L = T + S
    tm = min(256, T)
    depth = w_mod.shape[0]

    xx = jnp.concatenate([ctx, x], axis=1)
    cc = jnp.concatenate([c, c_ctx[None, :], jnp.zeros((16 - B - 1, D), F32)], axis=0)
    mods = _modulation(cc, w_mod, b_mod)
    mods = mods.reshape(depth, 16, N_MOD, D)
    tabs = jnp.stack([jnp.broadcast_to(mods[:, B:B + 1], (depth, B, N_MOD, D)), mods[:, :B]], axis=2)
    tabs = jnp.transpose(tabs, (0, 3, 1, 2, 4))[:, :, :, :, None, :]

    lb_all = jnp.cumsum(jax.nn.softmax(hg_lb.astype(F32), axis=0), axis=0)
    lb_all = lb_all - lb_all[:1]
    cs = _rope_table(T, S)

    for layer in range(depth):
        j = layer // 2
        tab = tabs[layer]
        if layer % 2 == 0:
            p = _inproj(xx, tab[0], tab[1], _even_w_in(even_w_in[j]), tm, T)
            mix_a = _gdn(p, gdn_conv[j], (-jnp.exp(gdn_a_log[j].astype(F32))).reshape(-1),
                         gdn_dt_bias[j].astype(F32).reshape(-1), gdn_norm[j], T)
            q, k, v = _mlaproj(p, cs, mla_q_norm[j], mla_kv_norm[j], _mla_wq(mla_w_uq[j]),
                               mla_w_ukv[j].astype(BF16), tm)
            mix_b = _mla_attn(q, k, v, T, tm)
            w_out = even_w_out[j]
        else:
            p = _inproj(xx, tab[0], tab[1], odd_w_in[j].astype(BF16), tm, T)
            mix_a = _na(p, na_rpb[j], T)
            lb = lb_all[j]
            mix_b = _hgrn(p, jnp.log(lb), jnp.log1p(-lb), 1.0 - lb, hg_norm[j], T)
            w_out = odd_w_out[j]
        xx, aff = _outproj(mix_a, mix_b, w_out.astype(BF16), xx, tab[2], ln_g[layer, 0], ln_b[layer, 0],
                           tab[3], tab[4], moe_router[layer].T, tm, T)
        idx, gate, cap_c, cap = _topk(aff, T)
        xe = _gather(idx, xx, tab[3], tab[4], cap_c, cap)
        m_rows = B * cap
        ye = _experts(xe.reshape(N_EXPERTS, m_rows, D), moe_w_gate, moe_w_up, moe_w_down, layer,
                      m_rows // 4 if m_rows % 64 == 0 else m_rows)
        y = _scatter(idx, gate, ye.reshape(N_EXPERTS, B, cap, D), L)
        xx = _resln(xx, y, tab[5], ln_g[layer, 1], ln_b[layer, 1], tm, T)
    return xx[:, T:, :]
```

```python
import functools
import math

import jax
import jax.numpy as jnp
from jax import lax
from jax.experimental import pallas as pl
from jax.experimental.pallas import tpu as pltpu

F32 = jnp.float32
BF16 = jnp.bfloat16
HIGHEST = lax.Precision.HIGHEST

DEPTH = 4
GRID_W = 64
N_MOD = 6
EPS = 1e-6
DEEPNORM_ALPHA = (2.0 * DEPTH) ** 0.25

GDN_HEADS = 4
GDN_DK = 128
GDN_DV = 128
GDN_CONV = 5
MLA_HEADS = 4
MLA_Q_LORA = 256
MLA_KV_LORA = 256
MLA_NOPE = 128
MLA_ROPE = 64
MLA_DV = 128
ROPE_THETA = 10000.0
NA_HEADS = 8
NA_DH = 64
NA_KH = 8
NA_KW = 16
HG_HEADS = 4
HG_DK = 128
HG_DV = 128
N_EXPERTS = 16
EC_CAPACITY = 2

CHUNK = 64
NA_GROUP = 4
PREP_CHUNKS = 4
SCAN_UNROLL = 4
SCATTER_GROUP = 8
LANES = 128
MLA_QK_PAD = 256
NEG_BIG = -1e30
VMEM_LIMIT = 56 * 1024 * 1024

_NT = (((1,), (1,)), ((), ()))


def _cparams(sem):
    return pltpu.CompilerParams(dimension_semantics=sem, vmem_limit_bytes=VMEM_LIMIT)


def _mm(a, b):
    return jnp.dot(a.astype(BF16), b.astype(BF16), preferred_element_type=F32)


def _mm_nt(a, b):
    return lax.dot_general(a.astype(BF16), b.astype(BF16), _NT, preferred_element_type=F32)


def _mm32(a, b):
    return jnp.dot(a, b, precision=HIGHEST, preferred_element_type=F32)


def _mm32_nt(a, b):
    return lax.dot_general(a, b, _NT, precision=HIGHEST, preferred_element_type=F32)


def _silu(x):
    return x * jax.nn.sigmoid(x)


def _rms(x):
    return x * lax.rsqrt(jnp.mean(x * x, -1, keepdims=True) + EPS)


def _ln(y, g, b):
    yc = y - jnp.mean(y, -1, keepdims=True)
    return yc * lax.rsqrt(jnp.mean(yc * yc, -1, keepdims=True) + EPS) * g + b


def _mod_kernel(c_ref, w_ref, b_ref, o_ref):
    o_ref[...] = _mm32(_silu(c_ref[...]), w_ref[...]) + b_ref[...]


def _modulation(cc, w_mod, b_mod):
    depth, d, n = w_mod.shape
    tn = n // 4
    return pl.pallas_call(
        _mod_kernel,
        grid=(depth, n // tn),
        in_specs=[pl.BlockSpec((cc.shape[0], d), lambda l, j: (0, 0)),
                  pl.BlockSpec((None, d, tn), lambda l, j: (l, 0, j)),
                  pl.BlockSpec((None, 1, tn), lambda l, j: (l, 0, j))],
        out_specs=pl.BlockSpec((None, cc.shape[0], tn), lambda l, j: (l, 0, j)),
        out_shape=jax.ShapeDtypeStruct((depth, cc.shape[0], n), F32),
        compiler_params=_cparams(("parallel", "parallel")),
    )(cc, w_mod, b_mod.reshape(depth, 1, n))


def _inproj_kernel(x_ref, sh_ref, sc_ref, w_ref, o_ref):
    u = x_ref[...] * (1.0 + sc_ref[...]) + sh_ref[...]
    o_ref[...] = jnp.dot(u.astype(BF16), w_ref[...], preferred_element_type=F32)


def _tab_spec(d, tps, nct):
    return pl.BlockSpec((None, None, 1, d), lambda i: (i // tps, jnp.where(i % tps >= nct, 1, 0), 0, 0))


def _inproj(xx, sh, sc, w, tm, T):
    B, L, D = xx.shape
    n = w.shape[1]
    tps, nct = L // tm, T // tm
    out = pl.pallas_call(
        _inproj_kernel,
        grid=(B * tps,),
        in_specs=[pl.BlockSpec((tm, D), lambda i: (i, 0)),
                  _tab_spec(D, tps, nct), _tab_spec(D, tps, nct),
                  pl.BlockSpec((D, n), lambda i: (0, 0))],
        out_specs=pl.BlockSpec((tm, n), lambda i: (i, 0)),
        out_shape=jax.ShapeDtypeStruct((B * L, n), F32),
        compiler_params=_cparams(("parallel",)),
    )(xx.reshape(B * L, D), sh, sc, w)
    return out.reshape(B, L, n)


def _scan_masks(d):
    ri = lax.broadcasted_iota(jnp.int32, (CHUNK, CHUNK), 0)
    ci = lax.broadcasted_iota(jnp.int32, (CHUNK, CHUNK), 1)
    if d == 0:
        return ci <= ri, ci < ri, ri <= ci, ci == ri
    return ci >= ri, ci > ri, ri >= ci, ci == ri


def _bwd_chunk(i, nc, n):
    return jnp.where(i < nc, nc - 1 - i, n - 1 - (i - nc))


def _gdn_kernel(nega_ref, dtb_ref, q_ref, k_ref, v_ref, z_ref, ab_ref, cwq_ref, cwk_ref, cwv_ref, gain_ref,
                o_ref,
                pad_ref, qn_ref, kn_ref, vn_ref, ncs_ref, kw_ref, ou_ref, qp_ref, el_ref, of_ref, ob_ref,
                *, T, L):
    h = pl.program_id(1)
    n_chunks = L // CHUNK
    nc = T // CHUNK

    def conv_prep(src_ref, cw_ref, dst_ref, post):
        zero8 = jnp.zeros((8, LANES), F32)
        pad_ref[0:8, :] = zero8
        pad_ref[8:8 + T, :] = src_ref[0:T, :]
        pad_ref[8 + T:16 + T, :] = zero8
        pad_ref[16 + T:16 + L, :] = src_ref[T:L, :]
        pad_ref[16 + L:24 + L, :] = zero8
        cw = cw_ref[...]

        def body(n, carry):
            r0 = pl.multiple_of(n * CHUNK, CHUNK)
            p0 = r0 + jnp.where(r0 < T, 8, 16) - GDN_CONV // 2
            acc = jnp.zeros((CHUNK, LANES), F32)
            for j in range(GDN_CONV):
                acc = acc + pad_ref[pl.ds(p0 + j, CHUNK), :] * cw[j:j + 1, :]
            dst_ref[pl.ds(r0, CHUNK), :] = post(_silu(acc))
            return carry

        lax.fori_loop(0, n_chunks, body, 0, unroll=PREP_CHUNKS)

    def l2n(y):
        return y * lax.rsqrt(jnp.sum(y * y, -1, keepdims=True) + EPS)

    conv_prep(q_ref, cwq_ref, qn_ref, lambda y: l2n(y) * (GDN_DK ** -0.5))
    conv_prep(k_ref, cwk_ref, kn_ref, l2n)
    conv_prep(v_ref, cwv_ref, vn_ref, lambda y: y)

    lane = lax.broadcasted_iota(jnp.int32, (CHUNK, LANES), 1)

    def chunk_prep(i, carry):
        chains = []
        for c in range(PREP_CHUNKS):
            n = PREP_CHUNKS * i + c
            rows = pl.ds(pl.multiple_of(n * CHUNK, CHUNK), CHUNK)
            q, k, v, ab = qn_ref[rows, :], kn_ref[rows, :], vn_ref[rows, :], ab_ref[rows, :]
            for d in range(2):
                chains.append(dict(n=n, d=d, rows=rows, q=q, k=k, v=v, ab=ab))
        for ch in chains:
            d = ch["d"]
            incl, strict, incl_t, eye = _scan_masks(d)
            a_col = jnp.sum(jnp.where(lane == 4 * d + h, ch["ab"], 0.0), axis=1, keepdims=True)
            b_col = jnp.sum(jnp.where(lane == 8 + 4 * d + h, ch["ab"], 0.0), axis=1, keepdims=True)
            g_col = nega_ref[4 * d + h] * jax.nn.softplus(a_col + dtb_ref[4 * d + h])
            beta = jax.nn.sigmoid(b_col)
            g_row = jnp.sum(jnp.where(eye, g_col, 0.0), axis=0, keepdims=True)
            gc_col = jnp.sum(jnp.where(incl, g_row, 0.0), axis=1, keepdims=True)
            gc_row = jnp.sum(jnp.where(incl_t, g_col, 0.0), axis=0, keepdims=True)
            g_tot = jnp.sum(g_col, axis=0, keepdims=True)
            ch["decay"] = jnp.where(incl, jnp.exp(jnp.where(incl, gc_col - gc_row, 0.0)), 0.0)
            ch["kb"] = ch["k"] * beta
            ch["vb"] = ch["v"] * beta
            ch["e_col"] = jnp.exp(gc_col)
            ch["kd"] = ch["k"] * jnp.exp(g_tot - gc_col)
            ch["e_tot"] = jnp.exp(g_tot)
        for ch in chains:
            _, strict, _, eye = _scan_masks(ch["d"])
            a = jnp.where(strict, _mm_nt(ch["kb"], ch["k"]) * ch["decay"], 0.0)
            ch["p"] = jnp.where(eye, 1.0, 0.0) - a
            ch["bp"] = a
        for _ in range(int(math.log2(CHUNK)) - 1):
            for ch in chains:
                ch["bp"] = _mm(ch["bp"], ch["bp"])
            for ch in chains:
                ch["p"] = ch["p"] + _mm(ch["p"], ch["bp"])
        for ch in chains:
            incl = _scan_masks(ch["d"])[0]
            ch["uw"] = _mm(ch["p"], jnp.concatenate([ch["vb"], ch["kb"] * ch["e_col"]], axis=1)).astype(BF16)
            ch["qk"] = jnp.where(incl, _mm_nt(ch["q"], ch["k"]) * ch["decay"], 0.0)
        for ch in chains:
            d, n, rows = ch["d"], ch["n"], ch["rows"]
            kd_uw = _mm(ch["kd"].T, ch["uw"])
            qk_uw = _mm(ch["qk"], ch["uw"])
            ncs_ref[d * n_chunks + n] = kd_uw[:, :GDN_DV]
            kw_ref[d * n_chunks + n] = kd_uw[:, GDN_DV:].astype(BF16)
            ou_ref[d, rows, :] = qk_uw[:, :GDN_DV]
            qp_ref[d, rows, :] = (ch["q"] * ch["e_col"] - qk_uw[:, GDN_DV:]).astype(BF16)
            el_ref[d * n_chunks + n] = jnp.broadcast_to(ch["e_tot"], (8, LANES))
        return carry

    lax.fori_loop(0, n_chunks // PREP_CHUNKS, chunk_prep, 0)

    def step(d, n, s):
        rows = pl.ds(pl.multiple_of(n * CHUNK, CHUNK), CHUNK)
        sb = s.astype(BF16)
        o = jnp.dot(qp_ref[d, rows, :], sb, preferred_element_type=F32) + ou_ref[d, rows, :]
        s = (s * el_ref[d * n_chunks + n][0:1, :] + ncs_ref[d * n_chunks + n]
             - jnp.dot(kw_ref[d * n_chunks + n], sb, preferred_element_type=F32))
        return s, o, rows

    def scan_body(i, carry):
        sf, sb = carry
        sf, o_f, rows_f = step(0, i, sf)
        sb, o_b, rows_b = step(1, _bwd_chunk(i, nc, n_chunks), sb)
        of_ref[rows_f, :] = o_f
        ob_ref[rows_b, :] = o_b
        return sf, sb

    zero = jnp.zeros((GDN_DK, GDN_DV), F32)
    lax.fori_loop(0, n_chunks, scan_body, (zero, zero), unroll=SCAN_UNROLL)

    o = of_ref[...] + ob_ref[...]
    o_ref[...] = (_rms(o) * gain_ref[...] * _silu(z_ref[...])).astype(BF16)


def _gdn(p, conv_w, neg_a, dt_bias, gain, T):
    B, L, _ = p.shape
    H = GDN_HEADS
    n_chunks = L // CHUNK
    col = lambda off: pl.BlockSpec((None, L, LANES), lambda b, h: (b, 0, off + h))
    cw = lambda off: pl.BlockSpec((GDN_CONV, LANES), lambda b, h: (0, off + h))
    smem = pl.BlockSpec(memory_space=pltpu.SMEM)
    return pl.pallas_call(
        functools.partial(_gdn_kernel, T=T, L=L),
        grid=(B, H),
        in_specs=[smem, smem, col(0), col(H), col(2 * H), col(3 * H),
                  pl.BlockSpec((None, L, LANES), lambda b, h: (b, 0, 21)),
                  cw(0), cw(H), cw(2 * H),
                  pl.BlockSpec((1, LANES), lambda b, h: (0, 0))],
        out_specs=pl.BlockSpec((None, L, LANES), lambda b, h: (b, 0, h)),
        out_shape=jax.ShapeDtypeStruct((B, L, H * GDN_DV), BF16),
        scratch_shapes=[pltpu.VMEM((L + 24, LANES), F32),
                        pltpu.VMEM((L, LANES), F32), pltpu.VMEM((L, LANES), F32), pltpu.VMEM((L, LANES), F32),
                        pltpu.VMEM((2 * n_chunks, GDN_DK, GDN_DV), F32),
                        pltpu.VMEM((2 * n_chunks, GDN_DK, GDN_DK), BF16),
                        pltpu.VMEM((2, L, LANES), F32), pltpu.VMEM((2, L, LANES), BF16),
                        pltpu.VMEM((2 * n_chunks, 8, LANES), F32),
                        pltpu.VMEM((L, LANES), F32), pltpu.VMEM((L, LANES), F32)],
        compiler_params=_cparams(("parallel", "parallel")),
    )(neg_a, dt_bias, p, p, p, p, p, conv_w, conv_w, conv_w, gain.reshape(1, LANES))


def _mlaproj_kernel(ql_ref, kvl_ref, kr_ref, cs_ref, qg_ref, kvg_ref, wq_ref, wkv_ref, q_ref, k_ref, v_ref):
    q = jnp.dot((_rms(ql_ref[...]) * qg_ref[...]).astype(BF16), wq_ref[...], preferred_element_type=F32)
    kv = jnp.dot((_rms(kvl_ref[...]) * kvg_ref[...]).astype(BF16), wkv_ref[...], preferred_element_type=F32)
    cs = cs_ref[...]
    lane = lax.broadcasted_iota(jnp.int32, cs.shape, 1)

    def rope(blk):
        t = blk * cs
        return jnp.where(lane < MLA_ROPE, t + pltpu.roll(t, MLA_ROPE, 1), 0.0)

    kr = rope(kr_ref[...]).astype(BF16)
    for h in range(MLA_HEADS):
        o = h * MLA_QK_PAD
        q_ref[:, o:o + MLA_NOPE] = q[:, o:o + MLA_NOPE].astype(BF16)
        q_ref[:, o + MLA_NOPE:o + MLA_QK_PAD] = rope(q[:, o + MLA_NOPE:o + MLA_QK_PAD]).astype(BF16)
        k_ref[:, o:o + MLA_NOPE] = kv[:, o:o + MLA_NOPE].astype(BF16)
        k_ref[:, o + MLA_NOPE:o + MLA_QK_PAD] = kr
        v_ref[:, h * MLA_DV:(h + 1) * MLA_DV] = kv[:, o + MLA_NOPE:o + MLA_QK_PAD].astype(BF16)


def _mlaproj(p, cs, q_gain, kv_gain, wq, wkv, tm):
    B, L, n = p.shape
    tps = L // tm
    p2 = p.reshape(B * L, n)
    hq = MLA_HEADS * MLA_QK_PAD
    q, k, v = pl.pallas_call(
        _mlaproj_kernel,
        grid=(B * tps,),
        in_specs=[pl.BlockSpec((tm, MLA_Q_LORA), lambda i: (i, 8)),
                  pl.BlockSpec((tm, MLA_KV_LORA), lambda i: (i, 9)),
                  pl.BlockSpec((tm, LANES), lambda i: (i, 20)),
                  pl.BlockSpec((tm, LANES), lambda i: (i % tps, 0)),
                  pl.BlockSpec((1, MLA_Q_LORA), lambda i: (0, 0)),
                  pl.BlockSpec((1, MLA_KV_LORA), lambda i: (0, 0)),
                  pl.BlockSpec((MLA_Q_LORA, hq), lambda i: (0, 0)),
                  pl.BlockSpec((MLA_KV_LORA, hq), lambda i: (0, 0))],
        out_specs=[pl.BlockSpec((tm, hq), lambda i: (i, 0)),
                   pl.BlockSpec((tm, hq), lambda i: (i, 0)),
                   pl.BlockSpec((tm, MLA_HEADS * MLA_DV), lambda i: (i, 0))],
        out_shape=[jax.ShapeDtypeStruct((B * L, hq), BF16),
                   jax.ShapeDtypeStruct((B * L, hq), BF16),
                   jax.ShapeDtypeStruct((B * L, MLA_HEADS * MLA_DV), BF16)],
        compiler_params=_cparams(("parallel",)),
    )(p2, p2, p2, cs, q_gain.reshape(1, -1), kv_gain.reshape(1, -1), wq, wkv)
    return q.reshape(B, L, hq), k.reshape(B, L, hq), v.reshape(B, L, -1)


def _softmax_pv(s, v):
    m = jnp.max(s, axis=-1, keepdims=True)
    p = jnp.exp(s - m)
    l = jnp.sum(p, axis=-1, keepdims=True)
    return jnp.dot(p.astype(BF16), v, preferred_element_type=F32) / l


def _mla_attn_kernel(q_ref, k_ref, v_ref, o_ref, *, T, nct, scale):
    qt = pl.program_id(2)
    q = q_ref[...]

    @pl.when(qt < nct)
    def _():
        s = lax.dot_general(q, k_ref[0:T, :], _NT, preferred_element_type=F32) * scale
        o_ref[...] = _softmax_pv(s, v_ref[0:T, :]).astype(BF16)

    @pl.when(qt >= nct)
    def _():
        s = lax.dot_general(q, k_ref[...], _NT, preferred_element_type=F32) * scale
        o_ref[...] = _softmax_pv(s, v_ref[...]).astype(BF16)


def _mla_attn(q, k, v, T, tq):
    B, L, _ = q.shape
    H = MLA_HEADS
    scale = (MLA_NOPE + MLA_ROPE) ** -0.5
    return pl.pallas_call(
        functools.partial(_mla_attn_kernel, T=T, nct=T // tq, scale=scale),
        grid=(B, H, L // tq),
        in_specs=[pl.BlockSpec((None, tq, MLA_QK_PAD), lambda b, h, i: (b, i, h)),
                  pl.BlockSpec((None, L, MLA_QK_PAD), lambda b, h, i: (b, 0, h)),
                  pl.BlockSpec((None, L, MLA_DV), lambda b, h, i: (b, 0, h))],
        out_specs=pl.BlockSpec((None, tq, MLA_DV), lambda b, h, i: (b, i, h)),
        out_shape=jax.ShapeDtypeStruct((B, L, H * MLA_DV), BF16),
        compiler_params=_cparams(("parallel", "parallel", "arbitrary")),
    )(q, k, v)


def _outproj_kernel(a_ref, b_ref, w_ref, x_ref, gate_ref, lng_ref, lnb_ref, sh_ref, sc_ref, wr_ref,
                    xo_ref, aff_ref):
    ka = a_ref.shape[1]
    o = (jnp.dot(a_ref[...], w_ref[0:ka, :], preferred_element_type=F32)
         + jnp.dot(b_ref[...], w_ref[ka:, :], preferred_element_type=F32))
    xn = _ln(DEEPNORM_ALPHA * x_ref[...] + gate_ref[...] * o, lng_ref[...], lnb_ref[...])
    xo_ref[...] = xn
    hmod = xn * (1.0 + sc_ref[...]) + sh_ref[...]
    logits = _mm_nt(wr_ref[...], hmod)
    e = jnp.exp(logits - jnp.max(logits, axis=0, keepdims=True))
    aff_ref[...] = e / jnp.sum(e, axis=0, keepdims=True)


def _outproj(a, b, w, xx, gate, ln_g, ln_b, sh, sc, wr_t, tm, T):
    B, L, D = xx.shape
    tps, nct = L // tm, T // tm
    ka, kb = a.shape[-1], b.shape[-1]
    row = lambda n: pl.BlockSpec((1, n), lambda i: (0, 0))
    xo, aff = pl.pallas_call(
        _outproj_kernel,
        grid=(B * tps,),
        in_specs=[pl.BlockSpec((tm, ka), lambda i: (i, 0)),
                  pl.BlockSpec((tm, kb), lambda i: (i, 0)),
                  pl.BlockSpec((ka + kb, D), lambda i: (0, 0)),
                  pl.BlockSpec((tm, D), lambda i: (i, 0)),
                  _tab_spec(D, tps, nct), row(D), row(D),
                  _tab_spec(D, tps, nct), _tab_spec(D, tps, nct),
                  pl.BlockSpec((N_EXPERTS, D), lambda i: (0, 0))],
        out_specs=[pl.BlockSpec((tm, D), lambda i: (i, 0)),
                   pl.BlockSpec((None, N_EXPERTS, tm), lambda i: (i // tps, 0, i % tps))],
        out_shape=[jax.ShapeDtypeStruct((B * L, D), F32),
                   jax.ShapeDtypeStruct((B, N_EXPERTS, L), F32)],
        compiler_params=_cparams(("parallel",)),
    )(a.reshape(B * L, ka), b.reshape(B * L, kb), w, xx.reshape(B * L, D), gate,
      ln_g.reshape(1, D), ln_b.reshape(1, D), sh, sc, wr_t)
    return xo.reshape(B, L, D), aff


def _topk_kernel(aff_ref, idx_ref, gate_ref, *, T, cap_c, cap_l):
    ri = lax.broadcasted_iota(jnp.int32, (LANES, LANES), 0)
    ci = lax.broadcasted_iota(jnp.int32, (LANES, LANES), 1)
    tri = jnp.where(ri <= ci, 1.0, 0.0).astype(BF16)

    def prefix_incl(x):
        carry = jnp.zeros((x.shape[0], 1), F32)
        outs = []
        for j in range(x.shape[1] // LANES):
            pj = jnp.dot(x[:, j * LANES:(j + 1) * LANES].astype(BF16), tri, preferred_element_type=F32) + carry
            outs.append(pj)
            carry = pj[:, LANES - 1:LANES]
        return jnp.concatenate(outs, axis=1)

    def select(aff, cap, base, row0):
        ne, n = aff.shape
        def bisect(_, lh):
            lo, hi = lh
            mid = lo + lax.shift_right_logical(hi - lo, 1)
            cnt = jnp.sum(jnp.where(aff >= pltpu.bitcast(mid, F32), 1.0, 0.0), axis=1, keepdims=True)
            ge = cnt >= float(cap)
            return jnp.where(ge, mid, lo), jnp.where(ge, hi, mid)

        lo0 = jnp.zeros((ne, 1), jnp.int32)
        hi0 = jnp.full((ne, 1), 0x7F800000, jnp.int32)
        thr_bits, _ = lax.fori_loop(0, 31, bisect, (lo0, hi0))
        thr = pltpu.bitcast(thr_bits, F32)
        gt = aff > thr
        eqf = jnp.where(aff == thr, 1.0, 0.0)
        need = float(cap) - jnp.sum(jnp.where(gt, 1.0, 0.0), axis=1, keepdims=True)
        eq_before = prefix_incl(eqf) - eqf
        self_ = jnp.where(gt, 1.0, jnp.where(eq_before < need, eqf, 0.0))
        cnt = prefix_incl(self_)
        slot = lax.broadcasted_iota(jnp.int32, (cap, 1), 0).astype(F32)
        lane = lax.broadcasted_iota(jnp.int32, (cap, LANES), 1)
        idx_acc = jnp.zeros((cap, LANES), F32)
        gate_acc = jnp.zeros((cap, LANES), F32)
        for e in range(ne):
            ce, se, ae = cnt[e:e + 1, :], self_[e:e + 1, :], aff[e:e + 1, :]
            idx_e = jnp.sum(jnp.where(ce <= slot, 1.0, 0.0), axis=1, keepdims=True)
            hit = jnp.where(ce == slot + 1.0, se, 0.0)
            gate_e = jnp.sum(hit * ae, axis=1, keepdims=True)
            idx_acc = jnp.where(lane == e, idx_e + float(base), idx_acc)
            gate_acc = jnp.where(lane == e, gate_e, gate_acc)
        idx_ref[row0:row0 + cap, :] = idx_acc.astype(jnp.int32)
        gate_ref[row0:row0 + cap, :] = gate_acc

    aff = aff_ref[...]
    select(aff[:, 0:T], cap_c, 0, 0)
    select(aff[:, T:], cap_l, T, cap_c)


def _topk(aff, T):
    B, E, L = aff.shape
    cap_c = EC_CAPACITY * T // E
    cap_l = EC_CAPACITY * (L - T) // E
    cap = cap_c + cap_l
    idx, gate = pl.pallas_call(
        functools.partial(_topk_kernel, T=T, cap_c=cap_c, cap_l=cap_l),
        grid=(B,),
        in_specs=[pl.BlockSpec((None, E, L), lambda b: (b, 0, 0))],
        out_specs=[pl.BlockSpec((None, cap, LANES), lambda b: (b, 0, 0)),
                   pl.BlockSpec((None, cap, LANES), lambda b: (b, 0, 0))],
        out_shape=[jax.ShapeDtypeStruct((B, cap, LANES), jnp.int32),
                   jax.ShapeDtypeStruct((B, cap, LANES), F32)],
        compiler_params=_cparams(("parallel",)),
    )(aff)
    idx = jnp.swapaxes(idx[:, :, :E], 1, 2).reshape(B * E * cap)
    gate = jnp.swapaxes(gate[:, :, :E], 1, 2).reshape(B * E * cap)
    return idx, gate, cap_c, cap


def _gather_kernel(idx_ref, x_ref, sh_ref, sc_ref, o_ref, stage_ref, *, cap_c, cap):
    base = (pl.program_id(0) * pl.num_programs(1) + pl.program_id(1)) * cap

    def body(s, carry):
        stage_ref[pl.ds(s, 1), :] = x_ref[pl.ds(idx_ref[base + s], 1), :]
        return carry

    lax.fori_loop(0, cap, body, 0, unroll=8)
    o_ref[0:cap_c, :] = (stage_ref[0:cap_c, :] * (1.0 + sc_ref[0]) + sh_ref[0]).astype(BF16)
    o_ref[cap_c:, :] = (stage_ref[cap_c:, :] * (1.0 + sc_ref[1]) + sh_ref[1]).astype(BF16)


def _gather(idx, xx, sh, sc, cap_c, cap):
    B, L, D = xx.shape
    E = N_EXPERTS
    tab = pl.BlockSpec((None, 2, 1, D), lambda b, e, idx: (b, 0, 0, 0))
    return pl.pallas_call(
        functools.partial(_gather_kernel, cap_c=cap_c, cap=cap),
        grid_spec=pltpu.PrefetchScalarGridSpec(
            num_scalar_prefetch=1, grid=(B, E),
            in_specs=[pl.BlockSpec((None, L, D), lambda b, e, idx: (b, 0, 0)), tab, tab],
            out_specs=pl.BlockSpec((None, None, cap, D), lambda b, e, idx: (e, b, 0, 0)),
            scratch_shapes=[pltpu.VMEM((cap, D), F32)]),
        out_shape=jax.ShapeDtypeStruct((E, B, cap, D), BF16),
        compiler_params=_cparams(("parallel", "arbitrary")),
    )(idx, xx, sh, sc)


def _expert_kernel(x_ref, wg_ref, wu_ref, wd_ref, o_ref):
    x = x_ref[...]
    g = jnp.dot(x, wg_ref[...].astype(BF16), preferred_element_type=F32)
    u = jnp.dot(x, wu_ref[...].astype(BF16), preferred_element_type=F32)
    hid = (_silu(g) * u).astype(BF16)
    o_ref[...] = jnp.dot(hid, wd_ref[...].astype(BF16), preferred_element_type=F32)


def _experts(xe, w_gate, w_up, w_down, layer, tm):
    E, M, D = xe.shape
    F = w_gate.shape[-1]
    return pl.pallas_call(
        _expert_kernel,
        grid=(E, M // tm),
        in_specs=[pl.BlockSpec((None, tm, D), lambda e, m: (e, m, 0)),
                  pl.BlockSpec((None, None, D, F), lambda e, m: (layer, e, 0, 0)),
                  pl.BlockSpec((None, None, D, F), lambda e, m: (layer, e, 0, 0)),
                  pl.BlockSpec((None, None, F, D), lambda e, m: (layer, e, 0, 0))],
        out_specs=pl.BlockSpec((None, tm, D), lambda e, m: (e, m, 0)),
        out_shape=jax.ShapeDtypeStruct((E, M, D), F32),
        compiler_params=_cparams(("parallel", "arbitrary")),
    )(xe, w_gate, w_up, w_down)


def _combine_kernel(idx_ref, gate_ref, ye_ref, x_ref, gt_ref, lng_ref, lnb_ref, o_ref, *, cap, T, tm):
    e = pl.program_id(1)
    n_e = pl.num_programs(1)
    base = (pl.program_id(0) * n_e + e) * cap

    @pl.when(e == 0)
    def _():
        o_ref[...] = jnp.zeros(o_ref.shape, F32)

    def body(i, carry):
        s0 = base + i * SCATTER_GROUP
        rs = [idx_ref[s0 + j] for j in range(SCATTER_GROUP)]
        new = [o_ref[pl.ds(rs[j], 1), :] + gate_ref[s0 + j] * ye_ref[pl.ds(i * SCATTER_GROUP + j, 1), :]
               for j in range(SCATTER_GROUP)]
        for j in range(SCATTER_GROUP):
            o_ref[pl.ds(rs[j], 1), :] = new[j]
        return carry

    lax.fori_loop(0, cap // SCATTER_GROUP, body, 0)

    @pl.when(e == n_e - 1)
    def _():
        def tile(t, carry):
            rows = pl.ds(pl.multiple_of(t * tm, tm), tm)
            gt = jnp.where(t < T // tm, gt_ref[0], gt_ref[1])
            o_ref[rows, :] = _ln(DEEPNORM_ALPHA * x_ref[rows, :] + gt * o_ref[rows, :], lng_ref[...], lnb_ref[...])
            return carry

        lax.fori_loop(0, o_ref.shape[0] // tm, tile, 0)


def _combine(idx, gate, ye, xx, gt, ln_g, ln_b, tm, T):
    E, B, cap, D = ye.shape
    L = xx.shape[1]
    row = pl.BlockSpec((1, D), lambda b, e, i, g: (0, 0))
    return pl.pallas_call(
        functools.partial(_combine_kernel, cap=cap, T=T, tm=tm),
        grid_spec=pltpu.PrefetchScalarGridSpec(
            num_scalar_prefetch=2, grid=(B, E),
            in_specs=[pl.BlockSpec((None, None, cap, D), lambda b, e, i, g: (e, b, 0, 0)),
                      pl.BlockSpec((None, L, D), lambda b, e, i, g: (b, 0, 0)),
                      pl.BlockSpec((None, 2, 1, D), lambda b, e, i, g: (b, 0, 0, 0)),
                      row, row],
            out_specs=pl.BlockSpec((None, L, D), lambda b, e, i, g: (b, 0, 0))),
        out_shape=jax.ShapeDtypeStruct((B, L, D), F32),
        compiler_params=_cparams(("parallel", "arbitrary")),
    )(idx, gate, ye, xx, gt, ln_g.reshape(1, D), ln_b.reshape(1, D))


def _na_plan(rows, kh):
    band = min(kh + NA_GROUP - 1, rows)
    plan, keys = [], []
    for g in range(rows // NA_GROUP):
        q0 = NA_GROUP * g
        win = [min(max(q0 + a - kh // 2, 0), rows - kh) for a in range(NA_GROUP)]
        ub = min(win[0], rows - band)
        key = (ub - q0, tuple(w - q0 for w in win))
        if key not in keys:
            keys.append(key)
        plan.append((ub, keys.index(key)))
    return band, plan, keys


def _na_kernel(plan_ref, q_ref, k_ref, v_ref, bias_ref, o_ref, *, T, nct, band, scale):
    i = pl.program_id(2)
    q = q_ref[...]
    lane = lax.broadcasted_iota(jnp.int32, q.shape, 1)
    kc = k_ref[0:T, :].astype(BF16)
    vc = v_ref[0:T, :].astype(BF16)
    qh = [jnp.where(lane < NA_DH, q, 0.0).astype(BF16), jnp.where(lane >= NA_DH, q, 0.0).astype(BF16)]

    @pl.when(i < nct)
    def _():
        outs = [_softmax_pv(lax.dot_general(qh[j], kc, _NT, preferred_element_type=F32) * scale, vc)
                for j in range(2)]
        o_ref[...] = jnp.where(lane < NA_DH, outs[0], outs[1]).astype(BF16)

    @pl.when(i >= nct)
    def _():
        t0 = pl.multiple_of(T + plan_ref[0, i - nct] * GRID_W, GRID_W)
        kb = k_ref[pl.ds(t0, band * GRID_W), :].astype(BF16)
        vb = v_ref[pl.ds(t0, band * GRID_W), :].astype(BF16)
        outs = []
        for j in range(2):
            s_w = lax.dot_general(qh[j], kb, _NT, preferred_element_type=F32) * scale + bias_ref[j]
            s_c = lax.dot_general(qh[j], kc, _NT, preferred_element_type=F32) * scale
            m = jnp.maximum(jnp.max(s_w, axis=-1, keepdims=True), jnp.max(s_c, axis=-1, keepdims=True))
            p_w = jnp.exp(s_w - m)
            p_c = jnp.exp(s_c - m)
            l = jnp.sum(p_w, axis=-1, keepdims=True) + jnp.sum(p_c, axis=-1, keepdims=True)
            outs.append((jnp.dot(p_w.astype(BF16), vb, preferred_element_type=F32)
                         + jnp.dot(p_c.astype(BF16), vc, preferred_element_type=F32)) / l)
        o_ref[...] = jnp.where(lane < NA_DH, outs[0], outs[1]).astype(BF16)


def _na_bias_tables(rpb, band, keys, kh):
    w = jnp.arange(GRID_W)
    c0 = jnp.clip(w - NA_KW // 2, 0, GRID_W - NA_KW)
    col_ok = (w[None, :] >= c0[:, None]) & (w[None, :] < c0[:, None] + NA_KW)
    col_off = jnp.clip(w[None, :] - w[:, None] + (NA_KW - 1), 0, 2 * NA_KW - 2)
    onehot = (col_off[:, :, None] == jnp.arange(2 * NA_KW - 1)).astype(F32)
    base = jnp.einsum('hrm,wcm->hrwc', rpb.astype(F32), onehot, precision=HIGHEST)
    base = jnp.where(col_ok[None, None], base, NEG_BIG)
    masked = jnp.full((NA_HEADS, GRID_W, GRID_W), NEG_BIG, F32)
    tabs = []
    for ub_off, win_offs in keys:
        per_row = []
        for a in range(NA_GROUP):
            blocks = []
            for u in range(band):
                rel = ub_off + u
                inside = win_offs[a] <= rel < win_offs[a] + kh
                blocks.append(base[:, rel - a + NA_KH - 1] if inside else masked)
            per_row.append(jnp.concatenate(blocks, axis=-1))
        tabs.append(jnp.concatenate(per_row, axis=1))
    return jnp.stack(tabs)


def _na(p, rpb, T):
    B, L, _ = p.shape
    rows = (L - T) // GRID_W
    kh = min(NA_KH, rows)
    tq = NA_GROUP * GRID_W
    assert T % tq == 0 and rows % NA_GROUP == 0
    nct = T // tq
    hp = NA_HEADS // 2
    band, plan, keys = _na_plan(rows, kh)
    bias = _na_bias_tables(rpb, band, keys, kh)
    plan = jnp.asarray(plan, jnp.int32).T
    return pl.pallas_call(
        functools.partial(_na_kernel, T=T, nct=nct, band=band, scale=NA_DH ** -0.5),
        grid_spec=pltpu.PrefetchScalarGridSpec(
            num_scalar_prefetch=1, grid=(B, hp, nct + rows // NA_GROUP),
            in_specs=[pl.BlockSpec((None, tq, LANES), lambda b, h, i, pr: (b, i, h)),
                      pl.BlockSpec((None, L, LANES), lambda b, h, i, pr: (b, 0, hp + h)),
                      pl.BlockSpec((None, L, LANES), lambda b, h, i, pr: (b, 0, 2 * hp + h)),
                      pl.BlockSpec((None, 2, tq, band * GRID_W),
                                   lambda b, h, i, pr: (pr[1, jnp.maximum(i - nct, 0)], h, 0, 0))],
            out_specs=pl.BlockSpec((None, tq, LANES), lambda b, h, i, pr: (b, i, h))),
        out_shape=jax.ShapeDtypeStruct((B, L, NA_HEADS * NA_DH), BF16),
        compiler_params=_cparams(("parallel", "parallel", "arbitrary")),
    )(plan, p, p, p, bias)


def _hgrn_kernel(q_ref, zf_ref, zb_ref, v_ref, gz_ref, llb_ref, l1m_ref, oml_ref, gain_ref,
                 o_ref,
                 qg_ref, ut_ref, oi_ref, el_ref, of_ref, ob_ref, *, T, L):
    n_chunks = L // CHUNK
    nc = T // CHUNK
    ri = lax.broadcasted_iota(jnp.int32, (CHUNK, CHUNK), 0)
    ci = lax.broadcasted_iota(jnp.int32, (CHUNK, CHUNK), 1)
    row = lax.broadcasted_iota(jnp.int32, (CHUNK, LANES), 0)
    eye = ri == ci
    top_bit = 31 - lax.clz(ri ^ ci)
    split_at = [jnp.where(ci < ri, top_bit, -1), jnp.where(ci > ri, top_bit, -1)]

    def shifted(x, k):
        return x if k == 0 else pltpu.roll(x, (-k) % CHUNK, 0)

    def scan_cumsum(x, d):
        k = 1
        while k < CHUNK:
            if d == 0:
                x = x + jnp.where(row >= k, shifted(x, -k), 0.0)
            else:
                x = x + jnp.where(row < CHUNK - k, shifted(x, k), 0.0)
            k *= 2
        return x

    def block_ref(g, size, r_in):
        if size >= 8:
            g3 = g.reshape(CHUNK // size, size, LANES)
            return jnp.broadcast_to(g3[:, r_in:r_in + 1, :], g3.shape).reshape(CHUNK, LANES)
        pos = row & (size - 1)
        out = shifted(g, r_in)
        for m in range(1, size):
            out = jnp.where(pos == m, shifted(g, r_in - m), out)
        return out

    def chunk_prep(n, carry):
        rows = pl.ds(pl.multiple_of(n * CHUNK, CHUNK), CHUNK)
        qs = _silu(q_ref[rows, :])
        v = v_ref[rows, :]
        vt = v.T
        for d in range(2):
            z = (zf_ref if d == 0 else zb_ref)[rows, :]
            t = jnp.exp(-jnp.abs(z))
            log_sig = jnp.minimum(z, 0.0) - jnp.log1p(t)
            logf = jnp.logaddexp(llb_ref[d:d + 1, :], l1m_ref[d:d + 1, :] + log_sig)
            kk = oml_ref[d:d + 1, :] * (jnp.where(z > 0, t, 1.0) / (1.0 + t))
            g = scan_cumsum(logf, d)
            g_tot = g[CHUNK - 1:CHUNK, :] if d == 0 else g[0:1, :]
            a = jnp.where(eye, jnp.sum(qs * kk, axis=1, keepdims=True), 0.0)
            half = CHUNK // 2
            while half >= 1:
                size = 2 * half
                e = jnp.exp(-jnp.abs(g - block_ref(g, size, half if d == 0 else half - 1)))
                a = jnp.where(split_at[d] == half.bit_length() - 1, _mm_nt(qs * e, kk * e), a)
                half //= 2
            oi_ref[d, rows, :] = _mm(a, v)
            qg_ref[d, rows, :] = (qs * jnp.exp(g)).astype(BF16)
            ut_ref[d * n_chunks + n] = _mm(vt, kk * jnp.exp(g_tot - g))
            el_ref[d * n_chunks + n] = jnp.broadcast_to(jnp.exp(g_tot), (8, LANES))
        return carry

    lax.fori_loop(0, n_chunks, chunk_prep, 0, unroll=2)

    def step(d, n, st):
        rows = pl.ds(pl.multiple_of(n * CHUNK, CHUNK), CHUNK)
        o = _mm_nt(qg_ref[d, rows, :], st) + oi_ref[d, rows, :]
        st = st * el_ref[d * n_chunks + n][0:1, :] + ut_ref[d * n_chunks + n]
        return st, o, rows

    def scan_body(i, carry):
        sf, sb = carry
        sf, o_f, rows_f = step(0, i, sf)
        sb, o_b, rows_b = step(1, _bwd_chunk(i, nc, n_chunks), sb)
        of_ref[rows_f, :] = o_f
        ob_ref[rows_b, :] = o_b
        return sf, sb

    zero = jnp.zeros((HG_DV, HG_DK), F32)
    lax.fori_loop(0, n_chunks, scan_body, (zero, zero), unroll=SCAN_UNROLL)
    o = of_ref[...] + ob_ref[...]
    o_ref[...] = (_rms(o) * gain_ref[...] * _silu(gz_ref[...])).astype(BF16)


def _hgrn(p, llb, l1m, oml, gain, T):
    B, L, _ = p.shape
    H = HG_HEADS
    n_chunks = L // CHUNK
    col = lambda off: pl.BlockSpec((None, L, LANES), lambda b, h: (b, 0, off + h))
    lbs = pl.BlockSpec((2, LANES), lambda b, h: (0, h))
    return pl.pallas_call(
        functools.partial(_hgrn_kernel, T=T, L=L),
        grid=(B, H),
        in_specs=[col(12), col(16), col(20), col(24), col(28), lbs, lbs, lbs,
                  pl.BlockSpec((1, LANES), lambda b, h: (0, 0))],
        out_specs=pl.BlockSpec((None, L, LANES), lambda b, h: (b, 0, h)),
        out_shape=jax.ShapeDtypeStruct((B, L, H * HG_DV), BF16),
        scratch_shapes=[pltpu.VMEM((2, L, LANES), BF16),
                        pltpu.VMEM((2 * n_chunks, HG_DV, HG_DK), F32),
                        pltpu.VMEM((2, L, LANES), F32),
                        pltpu.VMEM((2 * n_chunks, 8, LANES), F32),
                        pltpu.VMEM((L, LANES), F32), pltpu.VMEM((L, LANES), F32)],
        compiler_params=_cparams(("parallel", "parallel")),
    )(p, p, p, p, p, llb, l1m, oml, gain.reshape(1, LANES))


def _rot_cols(w):
    q = MLA_ROPE // 4
    return jnp.concatenate([-w[..., q:2 * q], w[..., 0:q], -w[..., 3 * q:4 * q], w[..., 2 * q:3 * q]], -1)


def _even_w_in(w):
    d = w.shape[0]
    qkv_z = GDN_HEADS * (2 * GDN_DK + GDN_DV) + GDN_HEADS * GDN_DV
    ab = w[:, qkv_z:qkv_z + 4 * GDN_HEADS]
    o = qkv_z + 4 * GDN_HEADS
    lora = w[:, o:o + MLA_Q_LORA + MLA_KV_LORA]
    kr = w[:, o + MLA_Q_LORA + MLA_KV_LORA:]
    pad = jnp.zeros((d, LANES - 4 * GDN_HEADS), w.dtype)
    return jnp.concatenate([w[:, :qkv_z], lora, kr, _rot_cols(kr), ab, pad], axis=1).astype(BF16)


def _mla_wq(w):
    w = w.reshape(w.shape[0], MLA_HEADS, MLA_NOPE + MLA_ROPE)
    rope = w[..., MLA_NOPE:]
    return jnp.concatenate([w[..., :MLA_NOPE], rope, _rot_cols(rope)], -1).reshape(w.shape[0], -1).astype(BF16)


def _rope_table(T, S):
    half = MLA_ROPE // 4
    inv = ROPE_THETA ** (-jnp.arange(half, dtype=F32) / half)
    pos = jnp.arange(S)
    ang_r = (pos // GRID_W).astype(F32)[:, None] * inv
    ang_c = (pos % GRID_W).astype(F32)[:, None] * inv
    ang = jnp.concatenate([ang_r, ang_r, ang_c, ang_c], -1)
    lat = jnp.concatenate([jnp.cos(ang), jnp.sin(ang)], -1)
    ctx = jnp.concatenate([jnp.ones((T, MLA_ROPE), F32), jnp.zeros((T, MLA_ROPE), F32)], -1)
    return jnp.concatenate([ctx, lat], 0)


def kernel(x, c, ctx, c_ctx, w_mod, b_mod, ln_g, ln_b, even_w_in, gdn_conv, gdn_a_log, gdn_dt_bias, gdn_norm,
           mla_q_norm, mla_kv_norm, mla_w_uq, mla_w_ukv, even_w_out, odd_w_in, na_rpb, hg_lb, hg_norm, odd_w_out,
           moe_router, moe_w_gate, moe_w_up, moe_w_down):
    B, S, D = x.shape
    T = ctx.shape[1]
    L = T + S
    tm = min(256, T)
    depth = w_mod.shape[0]

    xx = jnp.concatenate([ctx, x], axis=1)
    cc = jnp.concatenate([c, c_ctx[None, :], jnp.zeros((16 - B - 1, D), F32)], axis=0)
    mods = _modulation(cc, w_mod, b_mod)
    mods = mods.reshape(depth, 16, N_MOD, D)
    tabs = jnp.stack([jnp.broadcast_to(mods[:, B:B + 1], (depth, B, N_MOD, D)), mods[:, :B]], axis=2)
    tabs = jnp.transpose(tabs, (0, 3, 1, 2, 4))[:, :, :, :, None, :]

    lb_all = jnp.cumsum(jax.nn.softmax(hg_lb.astype(F32), axis=0), axis=0)
    lb_all = lb_all - lb_all[:1]
    cs = _rope_table(T, S)

    for layer in range(depth):
        j = layer // 2
        tab = tabs[layer]
        if layer % 2 == 0:
            p = _inproj(xx, tab[0], tab[1], _even_w_in(even_w_in[j]), tm, T)
            mix_a = _gdn(p, gdn_conv[j], (-jnp.exp(gdn_a_log[j].astype(F32))).reshape(-1),
                         gdn_dt_bias[j].astype(F32).reshape(-1), gdn_norm[j], T)
            q, k, v = _mlaproj(p, cs, mla_q_norm[j], mla_kv_norm[j], _mla_wq(mla_w_uq[j]),
                               mla_w_ukv[j].astype(BF16), tm)
            mix_b = _mla_attn(q, k, v, T, tm)
            w_out = even_w_out[j]
        else:
            p = _inproj(xx, tab[0], tab[1], odd_w_in[j].astype(BF16), tm, T)
            mix_a = _na(p, na_rpb[j], T)
            lb = lb_all[j]
            mix_b = _hgrn(p, jnp.log(lb), jnp.log1p(-lb), 1.0 - lb, hg_norm[j], T)
            w_out = odd_w_out[j]
        xx, aff = _outproj(mix_a, mix_b, w_out.astype(BF16), xx, tab[2], ln_g[layer, 0], ln_b[layer, 0],
                           tab[3], tab[4], moe_router[layer].T, tm, T)
        idx, gate, cap_c, cap = _topk(aff, T)
        xe = _gather(idx, xx, tab[3], tab[4], cap_c, cap)
        m_rows = B * cap
        ye = _experts(xe.reshape(N_EXPERTS, m_rows, D), moe_w_gate, moe_w_up, moe_w_down, layer,
                      m_rows // 4 if m_rows % 64 == 0 else m_rows)
        xx = _combine(idx, gate, ye.reshape(N_EXPERTS, B, cap, D), xx, tab[5], ln_g[layer, 1], ln_b[layer, 1],
                      tm, T)
    return xx[:, T:, :]
```

```python
import functools
import math

import jax
import jax.numpy as jnp
from jax import lax
from jax.experimental import pallas as pl
from jax.experimental.pallas import tpu as pltpu

F32 = jnp.float32
BF16 = jnp.bfloat16
HIGHEST = lax.Precision.HIGHEST

DEPTH = 4
GRID_W = 64
N_MOD = 6
EPS = 1e-6
DEEPNORM_ALPHA = (2.0 * DEPTH) ** 0.25

GDN_HEADS = 4
GDN_DK = 128
GDN_DV = 128
GDN_CONV = 5
MLA_HEADS = 4
MLA_Q_LORA = 256
MLA_KV_LORA = 256
MLA_NOPE = 128
MLA_ROPE = 64
MLA_DV = 128
ROPE_THETA = 10000.0
NA_HEADS = 8
NA_DH = 64
NA_KH = 8
NA_KW = 16
HG_HEADS = 4
HG_DK = 128
HG_DV = 128
N_EXPERTS = 16
EC_CAPACITY = 2

CHUNK = 64
NA_GROUP = 4
PREP_CHUNKS = 9
SCAN_UNROLL = 4
SCATTER_GROUP = 8
LANES = 128
MLA_QK_PAD = 256
NEG_BIG = -1e30
VMEM_LIMIT = 56 * 1024 * 1024

_NT = (((1,), (1,)), ((), ()))


def _cparams(sem):
    return pltpu.CompilerParams(dimension_semantics=sem, vmem_limit_bytes=VMEM_LIMIT)


def _mm(a, b):
    return jnp.dot(a.astype(BF16), b.astype(BF16), preferred_element_type=F32)


def _mm_nt(a, b):
    return lax.dot_general(a.astype(BF16), b.astype(BF16), _NT, preferred_element_type=F32)


def _mm32(a, b):
    return jnp.dot(a, b, precision=HIGHEST, preferred_element_type=F32)


def _mm32_nt(a, b):
    return lax.dot_general(a, b, _NT, precision=HIGHEST, preferred_element_type=F32)


def _silu(x):
    return x * jax.nn.sigmoid(x)


def _rms(x):
    return x * lax.rsqrt(jnp.mean(x * x, -1, keepdims=True) + EPS)


def _ln(y, g, b):
    yc = y - jnp.mean(y, -1, keepdims=True)
    return yc * lax.rsqrt(jnp.mean(yc * yc, -1, keepdims=True) + EPS) * g + b


def _mod_kernel(c_ref, w_ref, b_ref, o_ref):
    o_ref[...] = _mm32(_silu(c_ref[...]), w_ref[...]) + b_ref[...]


def _modulation(cc, w_mod, b_mod):
    depth, d, n = w_mod.shape
    tn = n // 4
    return pl.pallas_call(
        _mod_kernel,
        grid=(depth, n // tn),
        in_specs=[pl.BlockSpec((cc.shape[0], d), lambda l, j: (0, 0)),
                  pl.BlockSpec((None, d, tn), lambda l, j: (l, 0, j)),
                  pl.BlockSpec((None, 1, tn), lambda l, j: (l, 0, j))],
        out_specs=pl.BlockSpec((None, cc.shape[0], tn), lambda l, j: (l, 0, j)),
        out_shape=jax.ShapeDtypeStruct((depth, cc.shape[0], n), F32),
        compiler_params=_cparams(("parallel", "parallel")),
    )(cc, w_mod, b_mod.reshape(depth, 1, n))


def _inproj_kernel(x_ref, sh_ref, sc_ref, w_ref, o_ref):
    u = x_ref[...] * (1.0 + sc_ref[...]) + sh_ref[...]
    o_ref[...] = jnp.dot(u.astype(BF16), w_ref[...], preferred_element_type=F32)


def _tab_spec(d, tps, nct):
    return pl.BlockSpec((None, None, 1, d), lambda i: (i // tps, jnp.where(i % tps >= nct, 1, 0), 0, 0))


def _inproj(xx, sh, sc, w, tm, T):
    B, L, D = xx.shape
    n = w.shape[1]
    tps, nct = L // tm, T // tm
    out = pl.pallas_call(
        _inproj_kernel,
        grid=(B * tps,),
        in_specs=[pl.BlockSpec((tm, D), lambda i: (i, 0)),
                  _tab_spec(D, tps, nct), _tab_spec(D, tps, nct),
                  pl.BlockSpec((D, n), lambda i: (0, 0))],
        out_specs=pl.BlockSpec((tm, n), lambda i: (i, 0)),
        out_shape=jax.ShapeDtypeStruct((B * L, n), F32),
        compiler_params=_cparams(("parallel",)),
    )(xx.reshape(B * L, D), sh, sc, w)
    return out.reshape(B, L, n)


def _scan_masks(d):
    ri = lax.broadcasted_iota(jnp.int32, (CHUNK, CHUNK), 0)
    ci = lax.broadcasted_iota(jnp.int32, (CHUNK, CHUNK), 1)
    if d == 0:
        return ci <= ri, ci < ri, ri <= ci, ci == ri
    return ci >= ri, ci > ri, ri >= ci, ci == ri


def _bwd_chunk(i, nc, n):
    return jnp.where(i < nc, nc - 1 - i, n - 1 - (i - nc))


def _gdn_kernel(nega_ref, dtb_ref, q_ref, k_ref, v_ref, z_ref, ab_ref, cwq_ref, cwk_ref, cwv_ref, gain_ref,
                o_ref,
                pad_ref, qn_ref, kn_ref, vn_ref, ncs_ref, kw_ref, ou_ref, qp_ref, el_ref, of_ref, ob_ref,
                *, T, L):
    h = pl.program_id(1)
    n_chunks = L // CHUNK
    nc = T // CHUNK
    prep = max(c for c in range(1, PREP_CHUNKS + 1) if n_chunks % c == 0)

    def conv_prep(src_ref, cw_ref, dst_ref, post):
        zero8 = jnp.zeros((8, LANES), F32)
        pad_ref[0:8, :] = zero8
        pad_ref[8:8 + T, :] = src_ref[0:T, :]
        pad_ref[8 + T:16 + T, :] = zero8
        pad_ref[16 + T:16 + L, :] = src_ref[T:L, :]
        pad_ref[16 + L:24 + L, :] = zero8
        cw = cw_ref[...]

        def body(n, carry):
            r0 = pl.multiple_of(n * CHUNK, CHUNK)
            p0 = r0 + jnp.where(r0 < T, 8, 16) - GDN_CONV // 2
            acc = jnp.zeros((CHUNK, LANES), F32)
            for j in range(GDN_CONV):
                acc = acc + pad_ref[pl.ds(p0 + j, CHUNK), :] * cw[j:j + 1, :]
            dst_ref[pl.ds(r0, CHUNK), :] = post(_silu(acc))
            return carry

        lax.fori_loop(0, n_chunks, body, 0, unroll=prep)

    def l2n(y):
        return y * lax.rsqrt(jnp.sum(y * y, -1, keepdims=True) + EPS)

    conv_prep(q_ref, cwq_ref, qn_ref, lambda y: l2n(y) * (GDN_DK ** -0.5))
    conv_prep(k_ref, cwk_ref, kn_ref, l2n)
    conv_prep(v_ref, cwv_ref, vn_ref, lambda y: y)

    lane = lax.broadcasted_iota(jnp.int32, (CHUNK, LANES), 1)

    def chunk_prep(i, carry):
        chains = []
        for c in range(prep):
            n = prep * i + c
            rows = pl.ds(pl.multiple_of(n * CHUNK, CHUNK), CHUNK)
            q, k, v, ab = qn_ref[rows, :], kn_ref[rows, :], vn_ref[rows, :], ab_ref[rows, :]
            kq = _mm_nt(jnp.concatenate([k, q], axis=0), k)
            for d in range(2):
                chains.append(dict(n=n, d=d, rows=rows, q=q, k=k, v=v, ab=ab, kk=kq[:CHUNK], qk=kq[CHUNK:]))
        for ch in chains:
            d = ch["d"]
            incl, strict, incl_t, eye = _scan_masks(d)
            a_col = jnp.sum(jnp.where(lane == 4 * d + h, ch["ab"], 0.0), axis=1, keepdims=True)
            b_col = jnp.sum(jnp.where(lane == 8 + 4 * d + h, ch["ab"], 0.0), axis=1, keepdims=True)
            g_col = nega_ref[4 * d + h] * jax.nn.softplus(a_col + dtb_ref[4 * d + h])
            beta = jax.nn.sigmoid(b_col)
            g_row = jnp.sum(jnp.where(eye, g_col, 0.0), axis=0, keepdims=True)
            gc_col = jnp.sum(jnp.where(incl, g_row, 0.0), axis=1, keepdims=True)
            gc_row = jnp.sum(jnp.where(incl_t, g_col, 0.0), axis=0, keepdims=True)
            g_tot = jnp.sum(g_col, axis=0, keepdims=True)
            ch["decay"] = jnp.where(incl, jnp.exp(jnp.where(incl, gc_col - gc_row, 0.0)), 0.0)
            ch["kb"] = ch["k"] * beta
            ch["vb"] = ch["v"] * beta
            ch["beta"] = beta
            ch["e_col"] = jnp.exp(gc_col)
            ch["kd"] = ch["k"] * jnp.exp(g_tot - gc_col)
            ch["e_tot"] = jnp.exp(g_tot)
        for ch in chains:
            _, strict, _, eye = _scan_masks(ch["d"])
            a = jnp.where(strict, ch["beta"] * ch["kk"] * ch["decay"], 0.0)
            ch["p"] = jnp.where(eye, 1.0, 0.0) - a
            ch["bp"] = a
        for _ in range(int(math.log2(CHUNK)) - 1):
            for ch in chains:
                ch["bp"] = _mm(ch["bp"], ch["bp"])
            for ch in chains:
                ch["p"] = ch["p"] + _mm(ch["p"], ch["bp"])
        for ch in chains:
            incl = _scan_masks(ch["d"])[0]
            ch["uw"] = _mm(ch["p"], jnp.concatenate([ch["vb"], ch["kb"] * ch["e_col"]], axis=1)).astype(BF16)
            ch["qk"] = jnp.where(incl, ch["qk"] * ch["decay"], 0.0)
        for ch in chains:
            d, n, rows = ch["d"], ch["n"], ch["rows"]
            kd_uw = _mm(ch["kd"].T, ch["uw"])
            qk_uw = _mm(ch["qk"], ch["uw"])
            ncs_ref[d * n_chunks + n] = kd_uw[:, :GDN_DV]
            kw_ref[d * n_chunks + n] = kd_uw[:, GDN_DV:].astype(BF16)
            ou_ref[d, rows, :] = qk_uw[:, :GDN_DV]
            qp_ref[d, rows, :] = (ch["q"] * ch["e_col"] - qk_uw[:, GDN_DV:]).astype(BF16)
            el_ref[d * n_chunks + n] = jnp.broadcast_to(ch["e_tot"], (8, LANES))
        return carry

    lax.fori_loop(0, n_chunks // prep, chunk_prep, 0)

    def step(d, n, s):
        rows = pl.ds(pl.multiple_of(n * CHUNK, CHUNK), CHUNK)
        sb = s.astype(BF16)
        o = jnp.dot(qp_ref[d, rows, :], sb, preferred_element_type=F32) + ou_ref[d, rows, :]
        s = (s * el_ref[d * n_chunks + n][0:1, :] + ncs_ref[d * n_chunks + n]
             - jnp.dot(kw_ref[d * n_chunks + n], sb, preferred_element_type=F32))
        return s, o, rows

    def scan_body(i, carry):
        sf, sb = carry
        sf, o_f, rows_f = step(0, i, sf)
        sb, o_b, rows_b = step(1, _bwd_chunk(i, nc, n_chunks), sb)
        of_ref[rows_f, :] = o_f
        ob_ref[rows_b, :] = o_b
        return sf, sb

    zero = jnp.zeros((GDN_DK, GDN_DV), F32)
    lax.fori_loop(0, n_chunks, scan_body, (zero, zero), unroll=SCAN_UNROLL)

    o = of_ref[...] + ob_ref[...]
    o_ref[...] = (_rms(o) * gain_ref[...] * _silu(z_ref[...])).astype(BF16)


def _gdn(p, conv_w, neg_a, dt_bias, gain, T):
    B, L, _ = p.shape
    H = GDN_HEADS
    n_chunks = L // CHUNK
    col = lambda off: pl.BlockSpec((None, L, LANES), lambda b, h: (b, 0, off + h))
    cw = lambda off: pl.BlockSpec((GDN_CONV, LANES), lambda b, h: (0, off + h))
    smem = pl.BlockSpec(memory_space=pltpu.SMEM)
    return pl.pallas_call(
        functools.partial(_gdn_kernel, T=T, L=L),
        grid=(B, H),
        in_specs=[smem, smem, col(0), col(H), col(2 * H), col(3 * H),
                  pl.BlockSpec((None, L, LANES), lambda b, h: (b, 0, 21)),
                  cw(0), cw(H), cw(2 * H),
                  pl.BlockSpec((1, LANES), lambda b, h: (0, 0))],
        out_specs=pl.BlockSpec((None, L, LANES), lambda b, h: (b, 0, h)),
        out_shape=jax.ShapeDtypeStruct((B, L, H * GDN_DV), BF16),
        scratch_shapes=[pltpu.VMEM((L + 24, LANES), F32),
                        pltpu.VMEM((L, LANES), F32), pltpu.VMEM((L, LANES), F32), pltpu.VMEM((L, LANES), F32),
                        pltpu.VMEM((2 * n_chunks, GDN_DK, GDN_DV), F32),
                        pltpu.VMEM((2 * n_chunks, GDN_DK, GDN_DK), BF16),
                        pltpu.VMEM((2, L, LANES), F32), pltpu.VMEM((2, L, LANES), BF16),
                        pltpu.VMEM((2 * n_chunks, 8, LANES), F32),
                        pltpu.VMEM((L, LANES), F32), pltpu.VMEM((L, LANES), F32)],
        compiler_params=_cparams(("parallel", "parallel")),
    )(neg_a, dt_bias, p, p, p, p, p, conv_w, conv_w, conv_w, gain.reshape(1, LANES))


def _mlaproj_kernel(ql_ref, kvl_ref, kr_ref, cs_ref, qg_ref, kvg_ref, wq_ref, wkv_ref, q_ref, k_ref, v_ref):
    q = jnp.dot((_rms(ql_ref[...]) * qg_ref[...]).astype(BF16), wq_ref[...], preferred_element_type=F32)
    kv = jnp.dot((_rms(kvl_ref[...]) * kvg_ref[...]).astype(BF16), wkv_ref[...], preferred_element_type=F32)
    cs = cs_ref[...]
    lane = lax.broadcasted_iota(jnp.int32, cs.shape, 1)

    def rope(blk):
        t = blk * cs
        return jnp.where(lane < MLA_ROPE, t + pltpu.roll(t, MLA_ROPE, 1), 0.0)

    kr = rope(kr_ref[...]).astype(BF16)
    for h in range(MLA_HEADS):
        o = h * MLA_QK_PAD
        q_ref[:, o:o + MLA_NOPE] = q[:, o:o + MLA_NOPE].astype(BF16)
        q_ref[:, o + MLA_NOPE:o + MLA_QK_PAD] = rope(q[:, o + MLA_NOPE:o + MLA_QK_PAD]).astype(BF16)
        k_ref[:, o:o + MLA_NOPE] = kv[:, o:o + MLA_NOPE].astype(BF16)
        k_ref[:, o + MLA_NOPE:o + MLA_QK_PAD] = kr
        v_ref[:, h * MLA_DV:(h + 1) * MLA_DV] = kv[:, o + MLA_NOPE:o + MLA_QK_PAD].astype(BF16)


def _mlaproj(p, cs, q_gain, kv_gain, wq, wkv, tm):
    B, L, n = p.shape
    tps = L // tm
    p2 = p.reshape(B * L, n)
    hq = MLA_HEADS * MLA_QK_PAD
    q, k, v = pl.pallas_call(
        _mlaproj_kernel,
        grid=(B * tps,),
        in_specs=[pl.BlockSpec((tm, MLA_Q_LORA), lambda i: (i, 8)),
                  pl.BlockSpec((tm, MLA_KV_LORA), lambda i: (i, 9)),
                  pl.BlockSpec((tm, LANES), lambda i: (i, 20)),
                  pl.BlockSpec((tm, LANES), lambda i: (i % tps, 0)),
                  pl.BlockSpec((1, MLA_Q_LORA), lambda i: (0, 0)),
                  pl.BlockSpec((1, MLA_KV_LORA), lambda i: (0, 0)),
                  pl.BlockSpec((MLA_Q_LORA, hq), lambda i: (0, 0)),
                  pl.BlockSpec((MLA_KV_LORA, hq), lambda i: (0, 0))],
        out_specs=[pl.BlockSpec((tm, hq), lambda i: (i, 0)),
                   pl.BlockSpec((tm, hq), lambda i: (i, 0)),
                   pl.BlockSpec((tm, MLA_HEADS * MLA_DV), lambda i: (i, 0))],
        out_shape=[jax.ShapeDtypeStruct((B * L, hq), BF16),
                   jax.ShapeDtypeStruct((B * L, hq), BF16),
                   jax.ShapeDtypeStruct((B * L, MLA_HEADS * MLA_DV), BF16)],
        compiler_params=_cparams(("parallel",)),
    )(p2, p2, p2, cs, q_gain.reshape(1, -1), kv_gain.reshape(1, -1), wq, wkv)
    return q.reshape(B, L, hq), k.reshape(B, L, hq), v.reshape(B, L, -1)


def _softmax_pv(s, v):
    m = jnp.max(s, axis=-1, keepdims=True)
    p = jnp.exp(s - m)
    l = jnp.sum(p, axis=-1, keepdims=True)
    return jnp.dot(p.astype(BF16), v, preferred_element_type=F32) / l


def _mla_attn_kernel(q_ref, k_ref, v_ref, o_ref, *, T, tq, scale):
    qt = pl.program_id(2)
    q = q_ref[...]
    n_ctx = T // tq
    mixed = T - n_ctx * tq

    if n_ctx:
        @pl.when(qt < n_ctx)
        def _():
            s = lax.dot_general(q, k_ref[0:T, :], _NT, preferred_element_type=F32) * scale
            o_ref[...] = _softmax_pv(s, v_ref[0:T, :]).astype(BF16)

    if mixed:
        @pl.when(qt == n_ctx)
        def _():
            s = lax.dot_general(q, k_ref[...], _NT, preferred_element_type=F32) * scale
            r = lax.broadcasted_iota(jnp.int32, s.shape, 0)
            c = lax.broadcasted_iota(jnp.int32, s.shape, 1)
            s = jnp.where((r < mixed) & (c >= T), NEG_BIG, s)
            o_ref[...] = _softmax_pv(s, v_ref[...]).astype(BF16)

    @pl.when(qt >= n_ctx + (1 if mixed else 0))
    def _():
        s = lax.dot_general(q, k_ref[...], _NT, preferred_element_type=F32) * scale
        o_ref[...] = _softmax_pv(s, v_ref[...]).astype(BF16)


def _mla_attn(q, k, v, T):
    B, L, _ = q.shape
    H = MLA_HEADS
    scale = (MLA_NOPE + MLA_ROPE) ** -0.5
    tq = next(t for t in (768, 512, 256, 128, 64) if L % t == 0 and (t % T == 0 or T % t == 0))
    return pl.pallas_call(
        functools.partial(_mla_attn_kernel, T=T, tq=tq, scale=scale),
        grid=(B, H, L // tq),
        in_specs=[pl.BlockSpec((None, tq, MLA_QK_PAD), lambda b, h, i: (b, i, h)),
                  pl.BlockSpec((None, L, MLA_QK_PAD), lambda b, h, i: (b, 0, h)),
                  pl.BlockSpec((None, L, MLA_DV), lambda b, h, i: (b, 0, h))],
        out_specs=pl.BlockSpec((None, tq, MLA_DV), lambda b, h, i: (b, i, h)),
        out_shape=jax.ShapeDtypeStruct((B, L, H * MLA_DV), BF16),
        compiler_params=_cparams(("parallel", "parallel", "arbitrary")),
    )(q, k, v)


def _outproj_kernel(a_ref, b_ref, w_ref, x_ref, gate_ref, lng_ref, lnb_ref, sh_ref, sc_ref, wr_ref,
                    xo_ref, aff_ref):
    ka = a_ref.shape[1]
    o = (jnp.dot(a_ref[...], w_ref[0:ka, :], preferred_element_type=F32)
         + jnp.dot(b_ref[...], w_ref[ka:, :], preferred_element_type=F32))
    xn = _ln(DEEPNORM_ALPHA * x_ref[...] + gate_ref[...] * o, lng_ref[...], lnb_ref[...])
    xo_ref[...] = xn
    hmod = xn * (1.0 + sc_ref[...]) + sh_ref[...]
    logits = _mm_nt(wr_ref[...], hmod)
    e = jnp.exp(logits - jnp.max(logits, axis=0, keepdims=True))
    aff_ref[...] = e / jnp.sum(e, axis=0, keepdims=True)


def _outproj(a, b, w, xx, gate, ln_g, ln_b, sh, sc, wr_t, tm, T):
    B, L, D = xx.shape
    tps, nct = L // tm, T // tm
    ka, kb = a.shape[-1], b.shape[-1]
    row = lambda n: pl.BlockSpec((1, n), lambda i: (0, 0))
    xo, aff = pl.pallas_call(
        _outproj_kernel,
        grid=(B * tps,),
        in_specs=[pl.BlockSpec((tm, ka), lambda i: (i, 0)),
                  pl.BlockSpec((tm, kb), lambda i: (i, 0)),
                  pl.BlockSpec((ka + kb, D), lambda i: (0, 0)),
                  pl.BlockSpec((tm, D), lambda i: (i, 0)),
                  _tab_spec(D, tps, nct), row(D), row(D),
                  _tab_spec(D, tps, nct), _tab_spec(D, tps, nct),
                  pl.BlockSpec((N_EXPERTS, D), lambda i: (0, 0))],
        out_specs=[pl.BlockSpec((tm, D), lambda i: (i, 0)),
                   pl.BlockSpec((None, N_EXPERTS, tm), lambda i: (i // tps, 0, i % tps))],
        out_shape=[jax.ShapeDtypeStruct((B * L, D), F32),
                   jax.ShapeDtypeStruct((B, N_EXPERTS, L), F32)],
        compiler_params=_cparams(("parallel",)),
    )(a.reshape(B * L, ka), b.reshape(B * L, kb), w, xx.reshape(B * L, D), gate,
      ln_g.reshape(1, D), ln_b.reshape(1, D), sh, sc, wr_t)
    return xo.reshape(B, L, D), aff


def _topk_kernel(aff_ref, idx_ref, gate_ref, *, T, cap_c, cap_l):
    ri = lax.broadcasted_iota(jnp.int32, (LANES, LANES), 0)
    ci = lax.broadcasted_iota(jnp.int32, (LANES, LANES), 1)
    tri = jnp.where(ri <= ci, 1.0, 0.0).astype(BF16)

    def prefix_incl(x):
        carry = jnp.zeros((x.shape[0], 1), F32)
        outs = []
        for j in range(x.shape[1] // LANES):
            pj = jnp.dot(x[:, j * LANES:(j + 1) * LANES].astype(BF16), tri, preferred_element_type=F32) + carry
            outs.append(pj)
            carry = pj[:, LANES - 1:LANES]
        return jnp.concatenate(outs, axis=1)

    def select(aff, cap, base, row0):
        ne, n = aff.shape
        def bisect(_, lh):
            lo, hi = lh
            mid = lo + lax.shift_right_logical(hi - lo, 1)
            cnt = jnp.sum(jnp.where(aff >= pltpu.bitcast(mid, F32), 1.0, 0.0), axis=1, keepdims=True)
            ge = cnt >= float(cap)
            return jnp.where(ge, mid, lo), jnp.where(ge, hi, mid)

        lo0 = jnp.zeros((ne, 1), jnp.int32)
        hi0 = jnp.full((ne, 1), 0x7F800000, jnp.int32)
        thr_bits, _ = lax.fori_loop(0, 31, bisect, (lo0, hi0))
        thr = pltpu.bitcast(thr_bits, F32)
        gt = aff > thr
        eqf = jnp.where(aff == thr, 1.0, 0.0)
        need = float(cap) - jnp.sum(jnp.where(gt, 1.0, 0.0), axis=1, keepdims=True)
        eq_before = prefix_incl(eqf) - eqf
        self_ = jnp.where(gt, 1.0, jnp.where(eq_before < need, eqf, 0.0))
        cnt = prefix_incl(self_)
        slot = lax.broadcasted_iota(jnp.int32, (cap, 1), 0).astype(F32)
        lane = lax.broadcasted_iota(jnp.int32, (cap, LANES), 1)
        idx_acc = jnp.zeros((cap, LANES), F32)
        gate_acc = jnp.zeros((cap, LANES), F32)
        for e in range(ne):
            ce, se, ae = cnt[e:e + 1, :], self_[e:e + 1, :], aff[e:e + 1, :]
            idx_e = jnp.sum(jnp.where(ce <= slot, 1.0, 0.0), axis=1, keepdims=True)
            hit = jnp.where(ce == slot + 1.0, se, 0.0)
            gate_e = jnp.sum(hit * ae, axis=1, keepdims=True)
            idx_acc = jnp.where(lane == e, idx_e + float(base), idx_acc)
            gate_acc = jnp.where(lane == e, gate_e, gate_acc)
        idx_ref[row0:row0 + cap, :] = idx_acc.astype(jnp.int32)
        gate_ref[row0:row0 + cap, :] = gate_acc

    aff = aff_ref[...]
    select(aff[:, 0:T], cap_c, 0, 0)
    select(aff[:, T:], cap_l, T, cap_c)


def _topk(aff, T):
    B, E, L = aff.shape
    cap_c = EC_CAPACITY * T // E
    cap_l = EC_CAPACITY * (L - T) // E
    cap = cap_c + cap_l
    idx, gate = pl.pallas_call(
        functools.partial(_topk_kernel, T=T, cap_c=cap_c, cap_l=cap_l),
        grid=(B,),
        in_specs=[pl.BlockSpec((None, E, L), lambda b: (b, 0, 0))],
        out_specs=[pl.BlockSpec((None, cap, LANES), lambda b: (b, 0, 0)),
                   pl.BlockSpec((None, cap, LANES), lambda b: (b, 0, 0))],
        out_shape=[jax.ShapeDtypeStruct((B, cap, LANES), jnp.int32),
                   jax.ShapeDtypeStruct((B, cap, LANES), F32)],
        compiler_params=_cparams(("parallel",)),
    )(aff)
    idx = jnp.swapaxes(idx[:, :, :E], 1, 2).reshape(B * E * cap)
    gate = jnp.swapaxes(gate[:, :, :E], 1, 2).reshape(B * E * cap)
    return idx, gate, cap_c, cap


def _gather_kernel(idx_ref, x_ref, sh_ref, sc_ref, o_ref, stage_ref, *, cap_c, cap):
    base = (pl.program_id(0) * pl.num_programs(1) + pl.program_id(1)) * cap

    def body(s, carry):
        stage_ref[pl.ds(s, 1), :] = x_ref[pl.ds(idx_ref[base + s], 1), :]
        return carry

    lax.fori_loop(0, cap, body, 0, unroll=8)
    o_ref[0:cap_c, :] = (stage_ref[0:cap_c, :] * (1.0 + sc_ref[0]) + sh_ref[0]).astype(BF16)
    o_ref[cap_c:, :] = (stage_ref[cap_c:, :] * (1.0 + sc_ref[1]) + sh_ref[1]).astype(BF16)


def _gather(idx, xx, sh, sc, cap_c, cap):
    B, L, D = xx.shape
    E = N_EXPERTS
    tab = pl.BlockSpec((None, 2, 1, D), lambda b, e, idx: (b, 0, 0, 0))
    return pl.pallas_call(
        functools.partial(_gather_kernel, cap_c=cap_c, cap=cap),
        grid_spec=pltpu.PrefetchScalarGridSpec(
            num_scalar_prefetch=1, grid=(B, E),
            in_specs=[pl.BlockSpec((None, L, D), lambda b, e, idx: (b, 0, 0)), tab, tab],
            out_specs=pl.BlockSpec((None, None, cap, D), lambda b, e, idx: (e, b, 0, 0)),
            scratch_shapes=[pltpu.VMEM((cap, D), F32)]),
        out_shape=jax.ShapeDtypeStruct((E, B, cap, D), BF16),
        compiler_params=_cparams(("parallel", "arbitrary")),
    )(idx, xx, sh, sc)


def _expert_kernel(x_ref, wg_ref, wu_ref, wd_ref, o_ref):
    x = x_ref[...]
    g = jnp.dot(x, wg_ref[...].astype(BF16), preferred_element_type=F32)
    u = jnp.dot(x, wu_ref[...].astype(BF16), preferred_element_type=F32)
    hid = (_silu(g) * u).astype(BF16)
    o_ref[...] = jnp.dot(hid, wd_ref[...].astype(BF16), preferred_element_type=F32)


def _experts(xe, w_gate, w_up, w_down, layer, tm):
    E, M, D = xe.shape
    F = w_gate.shape[-1]
    return pl.pallas_call(
        _expert_kernel,
        grid=(E, M // tm),
        in_specs=[pl.BlockSpec((None, tm, D), lambda e, m: (e, m, 0)),
                  pl.BlockSpec((None, None, D, F), lambda e, m: (layer, e, 0, 0)),
                  pl.BlockSpec((None, None, D, F), lambda e, m: (layer, e, 0, 0)),
                  pl.BlockSpec((None, None, F, D), lambda e, m: (layer, e, 0, 0))],
        out_specs=pl.BlockSpec((None, tm, D), lambda e, m: (e, m, 0)),
        out_shape=jax.ShapeDtypeStruct((E, M, D), F32),
        compiler_params=_cparams(("parallel", "arbitrary")),
    )(xe, w_gate, w_up, w_down)


def _combine_kernel(idx_ref, gate_ref, ye_ref, x_ref, gt_ref, lng_ref, lnb_ref, o_ref, *, cap, T, tm):
    e = pl.program_id(1)
    n_e = pl.num_programs(1)
    base = (pl.program_id(0) * n_e + e) * cap

    @pl.when(e == 0)
    def _():
        o_ref[...] = jnp.zeros(o_ref.shape, F32)

    def body(i, carry):
        s0 = base + i * SCATTER_GROUP
        rs = [idx_ref[s0 + j] for j in range(SCATTER_GROUP)]
        new = [o_ref[pl.ds(rs[j], 1), :] + gate_ref[s0 + j] * ye_ref[pl.ds(i * SCATTER_GROUP + j, 1), :]
               for j in range(SCATTER_GROUP)]
        for j in range(SCATTER_GROUP):
            o_ref[pl.ds(rs[j], 1), :] = new[j]
        return carry

    lax.fori_loop(0, cap // SCATTER_GROUP, body, 0)

    @pl.when(e == n_e - 1)
    def _():
        def tile(t, carry):
            rows = pl.ds(pl.multiple_of(t * tm, tm), tm)
            gt = jnp.where(t < T // tm, gt_ref[0], gt_ref[1])
            o_ref[rows, :] = _ln(DEEPNORM_ALPHA * x_ref[rows, :] + gt * o_ref[rows, :], lng_ref[...], lnb_ref[...])
            return carry

        lax.fori_loop(0, o_ref.shape[0] // tm, tile, 0)


def _combine(idx, gate, ye, xx, gt, ln_g, ln_b, tm, T):
    E, B, cap, D = ye.shape
    L = xx.shape[1]
    row = pl.BlockSpec((1, D), lambda b, e, i, g: (0, 0))
    return pl.pallas_call(
        functools.partial(_combine_kernel, cap=cap, T=T, tm=tm),
        grid_spec=pltpu.PrefetchScalarGridSpec(
            num_scalar_prefetch=2, grid=(B, E),
            in_specs=[pl.BlockSpec((None, None, cap, D), lambda b, e, i, g: (e, b, 0, 0)),
                      pl.BlockSpec((None, L, D), lambda b, e, i, g: (b, 0, 0)),
                      pl.BlockSpec((None, 2, 1, D), lambda b, e, i, g: (b, 0, 0, 0)),
                      row, row],
            out_specs=pl.BlockSpec((None, L, D), lambda b, e, i, g: (b, 0, 0))),
        out_shape=jax.ShapeDtypeStruct((B, L, D), F32),
        compiler_params=_cparams(("parallel", "arbitrary")),
    )(idx, gate, ye, xx, gt, ln_g.reshape(1, D), ln_b.reshape(1, D))


def _na_plan(rows, kh):
    band = min(kh + NA_GROUP - 1, rows)
    plan, keys = [], []
    for g in range(rows // NA_GROUP):
        q0 = NA_GROUP * g
        win = [min(max(q0 + a - kh // 2, 0), rows - kh) for a in range(NA_GROUP)]
        ub = min(win[0], rows - band)
        key = (ub - q0, tuple(w - q0 for w in win))
        if key not in keys:
            keys.append(key)
        plan.append((ub, keys.index(key)))
    return band, plan, keys


def _na_kernel(plan_ref, q_ref, k_ref, v_ref, bias_ref, o_ref, *, T, nct, band, scale):
    i = pl.program_id(2)
    q = q_ref[...]
    lane = lax.broadcasted_iota(jnp.int32, q.shape, 1)
    kc = k_ref[0:T, :].astype(BF16)
    vc = v_ref[0:T, :].astype(BF16)
    tq = q.shape[0]
    q2 = jnp.concatenate([jnp.where(lane < NA_DH, q, 0.0), jnp.where(lane >= NA_DH, q, 0.0)], axis=0).astype(BF16)

    def unstack(o2):
        return jnp.where(lane < NA_DH, o2[:tq], o2[tq:]).astype(BF16)

    @pl.when(i < nct)
    def _():
        o_ref[...] = unstack(_softmax_pv(lax.dot_general(q2, kc, _NT, preferred_element_type=F32) * scale, vc))

    @pl.when(i >= nct)
    def _():
        t0 = pl.multiple_of(T + plan_ref[0, i - nct] * GRID_W, GRID_W)
        kb = k_ref[pl.ds(t0, band * GRID_W), :].astype(BF16)
        vb = v_ref[pl.ds(t0, band * GRID_W), :].astype(BF16)
        bias = bias_ref[...].reshape(2 * tq, band * GRID_W)
        s_w = lax.dot_general(q2, kb, _NT, preferred_element_type=F32) * scale + bias
        s_c = lax.dot_general(q2, kc, _NT, preferred_element_type=F32) * scale
        m = jnp.maximum(jnp.max(s_w, axis=-1, keepdims=True), jnp.max(s_c, axis=-1, keepdims=True))
        p_w = jnp.exp(s_w - m)
        p_c = jnp.exp(s_c - m)
        l = jnp.sum(p_w, axis=-1, keepdims=True) + jnp.sum(p_c, axis=-1, keepdims=True)
        o_ref[...] = unstack((jnp.dot(p_w.astype(BF16), vb, preferred_element_type=F32)
                              + jnp.dot(p_c.astype(BF16), vc, preferred_element_type=F32)) / l)


def _na_bias_tables(rpb, band, keys, kh):
    w = jnp.arange(GRID_W)
    c0 = jnp.clip(w - NA_KW // 2, 0, GRID_W - NA_KW)
    col_ok = (w[None, :] >= c0[:, None]) & (w[None, :] < c0[:, None] + NA_KW)
    col_off = jnp.clip(w[None, :] - w[:, None] + (NA_KW - 1), 0, 2 * NA_KW - 2)
    onehot = (col_off[:, :, None] == jnp.arange(2 * NA_KW - 1)).astype(F32)
    base = jnp.einsum('hrm,wcm->hrwc', rpb.astype(F32), onehot, precision=HIGHEST)
    base = jnp.where(col_ok[None, None], base, NEG_BIG)
    masked = jnp.full((NA_HEADS, GRID_W, GRID_W), NEG_BIG, F32)
    tabs = []
    for ub_off, win_offs in keys:
        per_row = []
        for a in range(NA_GROUP):
            blocks = []
            for u in range(band):
                rel = ub_off + u
                inside = win_offs[a] <= rel < win_offs[a] + kh
                blocks.append(base[:, rel - a + NA_KH - 1] if inside else masked)
            per_row.append(jnp.concatenate(blocks, axis=-1))
        tabs.append(jnp.concatenate(per_row, axis=1))
    return jnp.stack(tabs)


def _na(p, rpb, T):
    B, L, _ = p.shape
    rows = (L - T) // GRID_W
    kh = min(NA_KH, rows)
    tq = NA_GROUP * GRID_W
    assert T % tq == 0 and rows % NA_GROUP == 0
    nct = T // tq
    hp = NA_HEADS // 2
    band, plan, keys = _na_plan(rows, kh)
    bias = _na_bias_tables(rpb, band, keys, kh)
    plan = jnp.asarray(plan, jnp.int32).T
    return pl.pallas_call(
        functools.partial(_na_kernel, T=T, nct=nct, band=band, scale=NA_DH ** -0.5),
        grid_spec=pltpu.PrefetchScalarGridSpec(
            num_scalar_prefetch=1, grid=(B, hp, nct + rows // NA_GROUP),
            in_specs=[pl.BlockSpec((None, tq, LANES), lambda b, h, i, pr: (b, i, h)),
                      pl.BlockSpec((None, L, LANES), lambda b, h, i, pr: (b, 0, hp + h)),
                      pl.BlockSpec((None, L, LANES), lambda b, h, i, pr: (b, 0, 2 * hp + h)),
                      pl.BlockSpec((None, 2, tq, band * GRID_W),
                                   lambda b, h, i, pr: (pr[1, jnp.maximum(i - nct, 0)], h, 0, 0))],
            out_specs=pl.BlockSpec((None, tq, LANES), lambda b, h, i, pr: (b, i, h))),
        out_shape=jax.ShapeDtypeStruct((B, L, NA_HEADS * NA_DH), BF16),
        compiler_params=_cparams(("parallel", "parallel", "arbitrary")),
    )(plan, p, p, p, bias)


def _hgrn_kernel(q_ref, zf_ref, zb_ref, v_ref, gz_ref, llb_ref, l1m_ref, oml_ref, gain_ref,
                 o_ref,
                 qg_ref, ut_ref, oi_ref, el_ref, of_ref, ob_ref, *, T, L):
    n_chunks = L // CHUNK
    nc = T // CHUNK
    ri = lax.broadcasted_iota(jnp.int32, (CHUNK, CHUNK), 0)
    ci = lax.broadcasted_iota(jnp.int32, (CHUNK, CHUNK), 1)
    row = lax.broadcasted_iota(jnp.int32, (CHUNK, LANES), 0)
    eye = ri == ci
    top_bit = 31 - lax.clz(ri ^ ci)
    split_at = [jnp.where(ci < ri, top_bit, -1), jnp.where(ci > ri, top_bit, -1)]

    def shifted(x, k):
        return x if k == 0 else pltpu.roll(x, (-k) % CHUNK, 0)

    def scan_cumsum(x, d):
        k = 1
        while k < CHUNK:
            if d == 0:
                x = x + jnp.where(row >= k, shifted(x, -k), 0.0)
            else:
                x = x + jnp.where(row < CHUNK - k, shifted(x, k), 0.0)
            k *= 2
        return x

    def block_ref(g, size, r_in):
        if size >= 8:
            g3 = g.reshape(CHUNK // size, size, LANES)
            return jnp.broadcast_to(g3[:, r_in:r_in + 1, :], g3.shape).reshape(CHUNK, LANES)
        pos = row & (size - 1)
        out = shifted(g, r_in)
        for m in range(1, size):
            out = jnp.where(pos == m, shifted(g, r_in - m), out)
        return out

    def chunk_prep(n, carry):
        rows = pl.ds(pl.multiple_of(n * CHUNK, CHUNK), CHUNK)
        qs = _silu(q_ref[rows, :])
        v = v_ref[rows, :]
        vt = v.T
        for d in range(2):
            z = (zf_ref if d == 0 else zb_ref)[rows, :]
            t = jnp.exp(-jnp.abs(z))
            log_sig = jnp.minimum(z, 0.0) - jnp.log1p(t)
            logf = jnp.logaddexp(llb_ref[d:d + 1, :], l1m_ref[d:d + 1, :] + log_sig)
            kk = oml_ref[d:d + 1, :] * (jnp.where(z > 0, t, 1.0) / (1.0 + t))
            g = scan_cumsum(logf, d)
            g_tot = g[CHUNK - 1:CHUNK, :] if d == 0 else g[0:1, :]
            a = jnp.where(eye, jnp.sum(qs * kk, axis=1, keepdims=True), 0.0)
            half = CHUNK // 2
            while half >= 1:
                size = 2 * half
                e = jnp.exp(-jnp.abs(g - block_ref(g, size, half if d == 0 else half - 1)))
                a = jnp.where(split_at[d] == half.bit_length() - 1, _mm_nt(qs * e, kk * e), a)
                half //= 2
            oi_ref[d, rows, :] = _mm(a, v)
            qg_ref[d, rows, :] = (qs * jnp.exp(g)).astype(BF16)
            ut_ref[d * n_chunks + n] = _mm(vt, kk * jnp.exp(g_tot - g))
            el_ref[d * n_chunks + n] = jnp.broadcast_to(jnp.exp(g_tot), (8, LANES))
        return carry

    lax.fori_loop(0, n_chunks, chunk_prep, 0, unroll=2)

    def step(d, n, st):
        rows = pl.ds(pl.multiple_of(n * CHUNK, CHUNK), CHUNK)
        o = _mm_nt(qg_ref[d, rows, :], st) + oi_ref[d, rows, :]
        st = st * el_ref[d * n_chunks + n][0:1, :] + ut_ref[d * n_chunks + n]
        return st, o, rows

    def scan_body(i, carry):
        sf, sb = carry
        sf, o_f, rows_f = step(0, i, sf)
        sb, o_b, rows_b = step(1, _bwd_chunk(i, nc, n_chunks), sb)
        of_ref[rows_f, :] = o_f
        ob_ref[rows_b, :] = o_b
        return sf, sb

    zero = jnp.zeros((HG_DV, HG_DK), F32)
    lax.fori_loop(0, n_chunks, scan_body, (zero, zero), unroll=SCAN_UNROLL)
    o = of_ref[...] + ob_ref[...]
    o_ref[...] = (_rms(o) * gain_ref[...] * _silu(gz_ref[...])).astype(BF16)


def _hgrn(p, llb, l1m, oml, gain, T):
    B, L, _ = p.shape
    H = HG_HEADS
    n_chunks = L // CHUNK
    col = lambda off: pl.BlockSpec((None, L, LANES), lambda b, h: (b, 0, off + h))
    lbs = pl.BlockSpec((2, LANES), lambda b, h: (0, h))
    return pl.pallas_call(
        functools.partial(_hgrn_kernel, T=T, L=L),
        grid=(B, H),
        in_specs=[col(12), col(16), col(20), col(24), col(28), lbs, lbs, lbs,
                  pl.BlockSpec((1, LANES), lambda b, h: (0, 0))],
        out_specs=pl.BlockSpec((None, L, LANES), lambda b, h: (b, 0, h)),
        out_shape=jax.ShapeDtypeStruct((B, L, H * HG_DV), BF16),
        scratch_shapes=[pltpu.VMEM((2, L, LANES), BF16),
                        pltpu.VMEM((2 * n_chunks, HG_DV, HG_DK), F32),
                        pltpu.VMEM((2, L, LANES), F32),
                        pltpu.VMEM((2 * n_chunks, 8, LANES), F32),
                        pltpu.VMEM((L, LANES), F32), pltpu.VMEM((L, LANES), F32)],
        compiler_params=_cparams(("parallel", "parallel")),
    )(p, p, p, p, p, llb, l1m, oml, gain.reshape(1, LANES))


def _rot_cols(w):
    q = MLA_ROPE // 4
    return jnp.concatenate([-w[..., q:2 * q], w[..., 0:q], -w[..., 3 * q:4 * q], w[..., 2 * q:3 * q]], -1)


def _even_w_in(w):
    d = w.shape[0]
    qkv_z = GDN_HEADS * (2 * GDN_DK + GDN_DV) + GDN_HEADS * GDN_DV
    ab = w[:, qkv_z:qkv_z + 4 * GDN_HEADS]
    o = qkv_z + 4 * GDN_HEADS
    lora = w[:, o:o + MLA_Q_LORA + MLA_KV_LORA]
    kr = w[:, o + MLA_Q_LORA + MLA_KV_LORA:]
    pad = jnp.zeros((d, LANES - 4 * GDN_HEADS), w.dtype)
    return jnp.concatenate([w[:, :qkv_z], lora, kr, _rot_cols(kr), ab, pad], axis=1).astype(BF16)


def _mla_wq(w):
    w = w.reshape(w.shape[0], MLA_HEADS, MLA_NOPE + MLA_ROPE)
    rope = w[..., MLA_NOPE:]
    return jnp.concatenate([w[..., :MLA_NOPE], rope, _rot_cols(rope)], -1).reshape(w.shape[0], -1).astype(BF16)


def _rope_table(T, S):
    half = MLA_ROPE // 4
    inv = ROPE_THETA ** (-jnp.arange(half, dtype=F32) / half)
    pos = jnp.arange(S)
    ang_r = (pos // GRID_W).astype(F32)[:, None] * inv
    ang_c = (pos % GRID_W).astype(F32)[:, None] * inv
    ang = jnp.concatenate([ang_r, ang_r, ang_c, ang_c], -1)
    lat = jnp.concatenate([jnp.cos(ang), jnp.sin(ang)], -1)
    ctx = jnp.concatenate([jnp.ones((T, MLA_ROPE), F32), jnp.zeros((T, MLA_ROPE), F32)], -1)
    return jnp.concatenate([ctx, lat], 0)


def kernel(x, c, ctx, c_ctx, w_mod, b_mod, ln_g, ln_b, even_w_in, gdn_conv, gdn_a_log, gdn_dt_bias, gdn_norm,
           mla_q_norm, mla_kv_norm, mla_w_uq, mla_w_ukv, even_w_out, odd_w_in, na_rpb, hg_lb, hg_norm, odd_w_out,
           moe_router, moe_w_gate, moe_w_up, moe_w_down):
    B, S, D = x.shape
    T = ctx.shape[1]
    L = T + S
    tm = min(256, T)
    depth = w_mod.shape[0]

    xx = jnp.concatenate([ctx, x], axis=1)
    cc = jnp.concatenate([c, c_ctx[None, :], jnp.zeros((16 - B - 1, D), F32)], axis=0)
    mods = _modulation(cc, w_mod, b_mod)
    mods = mods.reshape(depth, 16, N_MOD, D)
    tabs = jnp.stack([jnp.broadcast_to(mods[:, B:B + 1], (depth, B, N_MOD, D)), mods[:, :B]], axis=2)
    tabs = jnp.transpose(tabs, (0, 3, 1, 2, 4))[:, :, :, :, None, :]

    lb_all = jnp.cumsum(jax.nn.softmax(hg_lb.astype(F32), axis=0), axis=0)
    lb_all = lb_all - lb_all[:1]
    cs = _rope_table(T, S)

    for layer in range(depth):
        j = layer // 2
        tab = tabs[layer]
        if layer % 2 == 0:
            p = _inproj(xx, tab[0], tab[1], _even_w_in(even_w_in[j]), tm, T)
            mix_a = _gdn(p, gdn_conv[j], (-jnp.exp(gdn_a_log[j].astype(F32))).reshape(-1),
                         gdn_dt_bias[j].astype(F32).reshape(-1), gdn_norm[j], T)
            q, k, v = _mlaproj(p, cs, mla_q_norm[j], mla_kv_norm[j], _mla_wq(mla_w_uq[j]),
                               mla_w_ukv[j].astype(BF16), tm)
            mix_b = _mla_attn(q, k, v, T)
            w_out = even_w_out[j]
        else:
            p = _inproj(xx, tab[0], tab[1], odd_w_in[j].astype(BF16), tm, T)
            mix_a = _na(p, na_rpb[j], T)
            lb = lb_all[j]
            mix_b = _hgrn(p, jnp.log(lb), jnp.log1p(-lb), 1.0 - lb, hg_norm[j], T)
            w_out = odd_w_out[j]
        xx, aff = _outproj(mix_a, mix_b, w_out.astype(BF16), xx, tab[2], ln_g[layer, 0], ln_b[layer, 0],
                           tab[3], tab[4], moe_router[layer].T, tm, T)
        idx, gate, cap_c, cap = _topk(aff, T)
        xe = _gather(idx, xx, tab[3], tab[4], cap_c, cap)
        m_rows = B * cap
        ye = _experts(xe.reshape(N_EXPERTS, m_rows, D), moe_w_gate, moe_w_up, moe_w_down, layer,
                      m_rows // 4 if m_rows % 64 == 0 else m_rows)
        xx = _combine(idx, gate, ye.reshape(N_EXPERTS, B, cap, D), xx, tab[5], ln_g[layer, 1], ln_b[layer, 1],
                      tm, T)
    return xx[:, T:, :]
```

```python
import functools
import math

import jax
import jax.numpy as jnp
from jax import lax
from jax.experimental import pallas as pl
from jax.experimental.pallas import tpu as pltpu

F32 = jnp.float32
BF16 = jnp.bfloat16
HIGHEST = lax.Precision.HIGHEST

DEPTH = 4
GRID_W = 64
N_MOD = 6
EPS = 1e-6
DEEPNORM_ALPHA = (2.0 * DEPTH) ** 0.25

GDN_HEADS = 4
GDN_DK = 128
GDN_DV = 128
GDN_CONV = 5
MLA_HEADS = 4
MLA_Q_LORA = 256
MLA_KV_LORA = 256
MLA_NOPE = 128
MLA_ROPE = 64
MLA_DV = 128
ROPE_THETA = 10000.0
NA_HEADS = 8
NA_DH = 64
NA_KH = 8
NA_KW = 16
HG_HEADS = 4
HG_DK = 128
HG_DV = 128
N_EXPERTS = 16
EC_CAPACITY = 2

CHUNK = 64
NA_GROUP = 4
ATTN_KEY_BLOCK = 256
PREP_CHUNKS = 9
SCAN_UNROLL = 4
SCATTER_GROUP = 8
LANES = 128
MLA_QK_PAD = 256
NEG_BIG = -1e30
VMEM_LIMIT = 56 * 1024 * 1024

_NT = (((1,), (1,)), ((), ()))


def _cparams(sem):
    return pltpu.CompilerParams(dimension_semantics=sem, vmem_limit_bytes=VMEM_LIMIT)


def _mm(a, b):
    return jnp.dot(a.astype(BF16), b.astype(BF16), preferred_element_type=F32)


def _mm_nt(a, b):
    return lax.dot_general(a.astype(BF16), b.astype(BF16), _NT, preferred_element_type=F32)


def _mm32(a, b):
    return jnp.dot(a, b, precision=HIGHEST, preferred_element_type=F32)


def _mm32_nt(a, b):
    return lax.dot_general(a, b, _NT, precision=HIGHEST, preferred_element_type=F32)


def _silu(x):
    return x * jax.nn.sigmoid(x)


def _rms(x):
    return x * lax.rsqrt(jnp.mean(x * x, -1, keepdims=True) + EPS)


def _ln(y, g, b):
    yc = y - jnp.mean(y, -1, keepdims=True)
    return yc * lax.rsqrt(jnp.mean(yc * yc, -1, keepdims=True) + EPS) * g + b


def _mod_kernel(c_ref, w_ref, b_ref, o_ref):
    o_ref[...] = _mm32(_silu(c_ref[...]), w_ref[...]) + b_ref[...]


def _modulation(cc, w_mod, b_mod):
    depth, d, n = w_mod.shape
    tn = n // 4
    return pl.pallas_call(
        _mod_kernel,
        grid=(depth, n // tn),
        in_specs=[pl.BlockSpec((cc.shape[0], d), lambda l, j: (0, 0)),
                  pl.BlockSpec((None, d, tn), lambda l, j: (l, 0, j)),
                  pl.BlockSpec((None, 1, tn), lambda l, j: (l, 0, j))],
        out_specs=pl.BlockSpec((None, cc.shape[0], tn), lambda l, j: (l, 0, j)),
        out_shape=jax.ShapeDtypeStruct((depth, cc.shape[0], n), F32),
        compiler_params=_cparams(("parallel", "parallel")),
    )(cc, w_mod, b_mod.reshape(depth, 1, n))


def _inproj_kernel(x_ref, sh_ref, sc_ref, w_ref, o_ref):
    u = x_ref[...] * (1.0 + sc_ref[...]) + sh_ref[...]
    o_ref[...] = jnp.dot(u.astype(BF16), w_ref[...], preferred_element_type=F32)


def _tab_spec(d, tps, nct):
    return pl.BlockSpec((None, None, 1, d), lambda i: (i // tps, jnp.where(i % tps >= nct, 1, 0), 0, 0))


def _inproj(xx, sh, sc, w, tm, T):
    B, L, D = xx.shape
    n = w.shape[1]
    tps, nct = L // tm, T // tm
    out = pl.pallas_call(
        _inproj_kernel,
        grid=(B * tps,),
        in_specs=[pl.BlockSpec((tm, D), lambda i: (i, 0)),
                  _tab_spec(D, tps, nct), _tab_spec(D, tps, nct),
                  pl.BlockSpec((D, n), lambda i: (0, 0))],
        out_specs=pl.BlockSpec((tm, n), lambda i: (i, 0)),
        out_shape=jax.ShapeDtypeStruct((B * L, n), F32),
        compiler_params=_cparams(("parallel",)),
    )(xx.reshape(B * L, D), sh, sc, w)
    return out.reshape(B, L, n)


def _scan_masks(d):
    ri = lax.broadcasted_iota(jnp.int32, (CHUNK, CHUNK), 0)
    ci = lax.broadcasted_iota(jnp.int32, (CHUNK, CHUNK), 1)
    if d == 0:
        return ci <= ri, ci < ri, ri <= ci, ci == ri
    return ci >= ri, ci > ri, ri >= ci, ci == ri


def _bwd_chunk(i, nc, n):
    return jnp.where(i < nc, nc - 1 - i, n - 1 - (i - nc))


def _gdn_kernel(nega_ref, dtb_ref, q_ref, k_ref, v_ref, z_ref, ab_ref, cwq_ref, cwk_ref, cwv_ref, gain_ref,
                o_ref,
                pad_ref, qn_ref, kn_ref, vn_ref, ncs_ref, kw_ref, ou_ref, qp_ref, el_ref, of_ref, ob_ref,
                *, T, L):
    h = pl.program_id(1)
    n_chunks = L // CHUNK
    nc = T // CHUNK
    prep = max(c for c in range(1, PREP_CHUNKS + 1) if n_chunks % c == 0)

    def conv_prep(src_ref, cw_ref, dst_ref, post):
        zero8 = jnp.zeros((8, LANES), F32)
        pad_ref[0:8, :] = zero8
        pad_ref[8:8 + T, :] = src_ref[0:T, :]
        pad_ref[8 + T:16 + T, :] = zero8
        pad_ref[16 + T:16 + L, :] = src_ref[T:L, :]
        pad_ref[16 + L:24 + L, :] = zero8
        cw = cw_ref[...]

        def body(n, carry):
            r0 = pl.multiple_of(n * CHUNK, CHUNK)
            p0 = r0 + jnp.where(r0 < T, 8, 16) - GDN_CONV // 2
            acc = jnp.zeros((CHUNK, LANES), F32)
            for j in range(GDN_CONV):
                acc = acc + pad_ref[pl.ds(p0 + j, CHUNK), :] * cw[j:j + 1, :]
            dst_ref[pl.ds(r0, CHUNK), :] = post(_silu(acc))
            return carry

        lax.fori_loop(0, n_chunks, body, 0, unroll=prep)

    def l2n(y):
        return y * lax.rsqrt(jnp.sum(y * y, -1, keepdims=True) + EPS)

    conv_prep(q_ref, cwq_ref, qn_ref, lambda y: l2n(y) * (GDN_DK ** -0.5))
    conv_prep(k_ref, cwk_ref, kn_ref, l2n)
    conv_prep(v_ref, cwv_ref, vn_ref, lambda y: y)

    lane = lax.broadcasted_iota(jnp.int32, (CHUNK, LANES), 1)

    def chunk_prep(i, carry):
        chains = []
        for c in range(prep):
            n = prep * i + c
            rows = pl.ds(pl.multiple_of(n * CHUNK, CHUNK), CHUNK)
            q, k, v, ab = qn_ref[rows, :], kn_ref[rows, :], vn_ref[rows, :], ab_ref[rows, :]
            kq = _mm_nt(jnp.concatenate([k, q], axis=0), k)
            for d in range(2):
                chains.append(dict(n=n, d=d, rows=rows, q=q, k=k, v=v, ab=ab, kk=kq[:CHUNK], qk=kq[CHUNK:]))
        for ch in chains:
            d = ch["d"]
            incl, strict, incl_t, eye = _scan_masks(d)
            a_col = jnp.sum(jnp.where(lane == 4 * d + h, ch["ab"], 0.0), axis=1, keepdims=True)
            b_col = jnp.sum(jnp.where(lane == 8 + 4 * d + h, ch["ab"], 0.0), axis=1, keepdims=True)
            g_col = nega_ref[4 * d + h] * jax.nn.softplus(a_col + dtb_ref[4 * d + h])
            beta = jax.nn.sigmoid(b_col)
            g_row = jnp.sum(jnp.where(eye, g_col, 0.0), axis=0, keepdims=True)
            gc_col = jnp.sum(jnp.where(incl, g_row, 0.0), axis=1, keepdims=True)
            gc_row = jnp.sum(jnp.where(incl_t, g_col, 0.0), axis=0, keepdims=True)
            g_tot = jnp.sum(g_col, axis=0, keepdims=True)
            ch["decay"] = jnp.where(incl, jnp.exp(jnp.where(incl, gc_col - gc_row, 0.0)), 0.0)
            ch["kb"] = ch["k"] * beta
            ch["vb"] = ch["v"] * beta
            ch["beta"] = beta
            ch["e_col"] = jnp.exp(gc_col)
            ch["kd"] = ch["k"] * jnp.exp(g_tot - gc_col)
            ch["e_tot"] = jnp.exp(g_tot)
        for ch in chains:
            _, strict, _, eye = _scan_masks(ch["d"])
            a = jnp.where(strict, ch["beta"] * ch["kk"] * ch["decay"], 0.0)
            ch["p"] = jnp.where(eye, 1.0, 0.0) - a
            ch["bp"] = a
        for _ in range(int(math.log2(CHUNK)) - 1):
            for ch in chains:
                ch["bp"] = _mm(ch["bp"], ch["bp"])
            for ch in chains:
                ch["p"] = ch["p"] + _mm(ch["p"], ch["bp"])
        for ch in chains:
            incl = _scan_masks(ch["d"])[0]
            ch["uw"] = _mm(ch["p"], jnp.concatenate([ch["vb"], ch["kb"] * ch["e_col"]], axis=1)).astype(BF16)
            ch["qk"] = jnp.where(incl, ch["qk"] * ch["decay"], 0.0)
        for ch in chains:
            d, n, rows = ch["d"], ch["n"], ch["rows"]
            kd_uw = _mm(ch["kd"].T, ch["uw"])
            qk_uw = _mm(ch["qk"], ch["uw"])
            ncs_ref[d * n_chunks + n] = kd_uw[:, :GDN_DV]
            kw_ref[d * n_chunks + n] = kd_uw[:, GDN_DV:].astype(BF16)
            ou_ref[d, rows, :] = qk_uw[:, :GDN_DV]
            qp_ref[d, rows, :] = (ch["q"] * ch["e_col"] - qk_uw[:, GDN_DV:]).astype(BF16)
            el_ref[d * n_chunks + n] = jnp.broadcast_to(ch["e_tot"], (8, LANES))
        return carry

    lax.fori_loop(0, n_chunks // prep, chunk_prep, 0)

    def step(d, n, s):
        rows = pl.ds(pl.multiple_of(n * CHUNK, CHUNK), CHUNK)
        sb = s.astype(BF16)
        o = jnp.dot(qp_ref[d, rows, :], sb, preferred_element_type=F32) + ou_ref[d, rows, :]
        s = (s * el_ref[d * n_chunks + n][0:1, :] + ncs_ref[d * n_chunks + n]
             - jnp.dot(kw_ref[d * n_chunks + n], sb, preferred_element_type=F32))
        return s, o, rows

    def scan_body(i, carry):
        sf, sb = carry
        sf, o_f, rows_f = step(0, i, sf)
        sb, o_b, rows_b = step(1, _bwd_chunk(i, nc, n_chunks), sb)
        of_ref[rows_f, :] = o_f
        ob_ref[rows_b, :] = o_b
        return sf, sb

    zero = jnp.zeros((GDN_DK, GDN_DV), F32)
    lax.fori_loop(0, n_chunks, scan_body, (zero, zero), unroll=SCAN_UNROLL)

    o = of_ref[...] + ob_ref[...]
    o_ref[...] = (_rms(o) * gain_ref[...] * _silu(z_ref[...])).astype(BF16)


def _gdn(p, conv_w, neg_a, dt_bias, gain, T):
    B, L, _ = p.shape
    H = GDN_HEADS
    n_chunks = L // CHUNK
    col = lambda off: pl.BlockSpec((None, L, LANES), lambda b, h: (b, 0, off + h))
    cw = lambda off: pl.BlockSpec((GDN_CONV, LANES), lambda b, h: (0, off + h))
    smem = pl.BlockSpec(memory_space=pltpu.SMEM)
    return pl.pallas_call(
        functools.partial(_gdn_kernel, T=T, L=L),
        grid=(B, H),
        in_specs=[smem, smem, col(0), col(H), col(2 * H), col(3 * H),
                  pl.BlockSpec((None, L, LANES), lambda b, h: (b, 0, 21)),
                  cw(0), cw(H), cw(2 * H),
                  pl.BlockSpec((1, LANES), lambda b, h: (0, 0))],
        out_specs=pl.BlockSpec((None, L, LANES), lambda b, h: (b, 0, h)),
        out_shape=jax.ShapeDtypeStruct((B, L, H * GDN_DV), BF16),
        scratch_shapes=[pltpu.VMEM((L + 24, LANES), F32),
                        pltpu.VMEM((L, LANES), F32), pltpu.VMEM((L, LANES), F32), pltpu.VMEM((L, LANES), F32),
                        pltpu.VMEM((2 * n_chunks, GDN_DK, GDN_DV), F32),
                        pltpu.VMEM((2 * n_chunks, GDN_DK, GDN_DK), BF16),
                        pltpu.VMEM((2, L, LANES), F32), pltpu.VMEM((2, L, LANES), BF16),
                        pltpu.VMEM((2 * n_chunks, 8, LANES), F32),
                        pltpu.VMEM((L, LANES), F32), pltpu.VMEM((L, LANES), F32)],
        compiler_params=_cparams(("parallel", "parallel")),
    )(neg_a, dt_bias, p, p, p, p, p, conv_w, conv_w, conv_w, gain.reshape(1, LANES))


def _mlaproj_kernel(ql_ref, kvl_ref, kr_ref, cs_ref, qg_ref, kvg_ref, wq_ref, wkv_ref, q_ref, k_ref, v_ref):
    q = jnp.dot((_rms(ql_ref[...]) * qg_ref[...]).astype(BF16), wq_ref[...], preferred_element_type=F32)
    kv = jnp.dot((_rms(kvl_ref[...]) * kvg_ref[...]).astype(BF16), wkv_ref[...], preferred_element_type=F32)
    cs = cs_ref[...]
    lane = lax.broadcasted_iota(jnp.int32, cs.shape, 1)

    def rope(blk):
        t = blk * cs
        return jnp.where(lane < MLA_ROPE, t + pltpu.roll(t, MLA_ROPE, 1), 0.0)

    kr = rope(kr_ref[...]).astype(BF16)
    for h in range(MLA_HEADS):
        o = h * MLA_QK_PAD
        q_ref[:, o:o + MLA_NOPE] = q[:, o:o + MLA_NOPE].astype(BF16)
        q_ref[:, o + MLA_NOPE:o + MLA_QK_PAD] = rope(q[:, o + MLA_NOPE:o + MLA_QK_PAD]).astype(BF16)
        k_ref[:, o:o + MLA_NOPE] = kv[:, o:o + MLA_NOPE].astype(BF16)
        k_ref[:, o + MLA_NOPE:o + MLA_QK_PAD] = kr
        v_ref[:, h * MLA_DV:(h + 1) * MLA_DV] = kv[:, o + MLA_NOPE:o + MLA_QK_PAD].astype(BF16)


def _mlaproj(p, cs, q_gain, kv_gain, wq, wkv, tm):
    B, L, n = p.shape
    tps = L // tm
    p2 = p.reshape(B * L, n)
    hq = MLA_HEADS * MLA_QK_PAD
    q, k, v = pl.pallas_call(
        _mlaproj_kernel,
        grid=(B * tps,),
        in_specs=[pl.BlockSpec((tm, MLA_Q_LORA), lambda i: (i, 8)),
                  pl.BlockSpec((tm, MLA_KV_LORA), lambda i: (i, 9)),
                  pl.BlockSpec((tm, LANES), lambda i: (i, 20)),
                  pl.BlockSpec((tm, LANES), lambda i: (i % tps, 0)),
                  pl.BlockSpec((1, MLA_Q_LORA), lambda i: (0, 0)),
                  pl.BlockSpec((1, MLA_KV_LORA), lambda i: (0, 0)),
                  pl.BlockSpec((MLA_Q_LORA, hq), lambda i: (0, 0)),
                  pl.BlockSpec((MLA_KV_LORA, hq), lambda i: (0, 0))],
        out_specs=[pl.BlockSpec((tm, hq), lambda i: (i, 0)),
                   pl.BlockSpec((tm, hq), lambda i: (i, 0)),
                   pl.BlockSpec((tm, MLA_HEADS * MLA_DV), lambda i: (i, 0))],
        out_shape=[jax.ShapeDtypeStruct((B * L, hq), BF16),
                   jax.ShapeDtypeStruct((B * L, hq), BF16),
                   jax.ShapeDtypeStruct((B * L, MLA_HEADS * MLA_DV), BF16)],
        compiler_params=_cparams(("parallel",)),
    )(p2, p2, p2, cs, q_gain.reshape(1, -1), kv_gain.reshape(1, -1), wq, wkv)
    return q.reshape(B, L, hq), k.reshape(B, L, hq), v.reshape(B, L, -1)


def _attend(scores, values):
    m = l = acc = None
    for score, value in zip(scores, values):
        s = score()
        bm = jnp.max(s, axis=-1, keepdims=True)
        if m is None:
            m, p = bm, jnp.exp(s - bm)
            l = jnp.sum(p, axis=-1, keepdims=True)
            acc = jnp.dot(p.astype(BF16), value(), preferred_element_type=F32)
        else:
            m_new = jnp.maximum(m, bm)
            alpha = jnp.exp(m - m_new)
            p = jnp.exp(s - m_new)
            l = alpha * l + jnp.sum(p, axis=-1, keepdims=True)
            acc = alpha * acc + jnp.dot(p.astype(BF16), value(), preferred_element_type=F32)
            m = m_new
    return acc / l


def _key_blocks(n):
    return [(k0, min(ATTN_KEY_BLOCK, n - k0)) for k0 in range(0, n, ATTN_KEY_BLOCK)]


def _mla_attn_kernel(q_ref, k_ref, v_ref, o_ref, *, T, nct, scale):
    qt = pl.program_id(2)
    q = q_ref[...]

    def attend(n_keys):
        blocks = _key_blocks(n_keys)
        scores = [lambda k0=k0, kn=kn: lax.dot_general(q, k_ref[k0:k0 + kn, :], _NT,
                                                       preferred_element_type=F32) * scale for k0, kn in blocks]
        values = [lambda k0=k0, kn=kn: v_ref[k0:k0 + kn, :] for k0, kn in blocks]
        o_ref[...] = _attend(scores, values).astype(BF16)

    @pl.when(qt < nct)
    def _():
        attend(T)

    @pl.when(qt >= nct)
    def _():
        attend(k_ref.shape[0])


def _mla_attn(q, k, v, T, tq):
    B, L, _ = q.shape
    H = MLA_HEADS
    scale = (MLA_NOPE + MLA_ROPE) ** -0.5
    return pl.pallas_call(
        functools.partial(_mla_attn_kernel, T=T, nct=T // tq, scale=scale),
        grid=(B, H, L // tq),
        in_specs=[pl.BlockSpec((None, tq, MLA_QK_PAD), lambda b, h, i: (b, i, h)),
                  pl.BlockSpec((None, L, MLA_QK_PAD), lambda b, h, i: (b, 0, h)),
                  pl.BlockSpec((None, L, MLA_DV), lambda b, h, i: (b, 0, h))],
        out_specs=pl.BlockSpec((None, tq, MLA_DV), lambda b, h, i: (b, i, h)),
        out_shape=jax.ShapeDtypeStruct((B, L, H * MLA_DV), BF16),
        compiler_params=_cparams(("parallel", "parallel", "arbitrary")),
    )(q, k, v)


def _outproj_kernel(a_ref, b_ref, w_ref, x_ref, gate_ref, lng_ref, lnb_ref, sh_ref, sc_ref, wr_ref,
                    xo_ref, aff_ref):
    ka = a_ref.shape[1]
    o = (jnp.dot(a_ref[...], w_ref[0:ka, :], preferred_element_type=F32)
         + jnp.dot(b_ref[...], w_ref[ka:, :], preferred_element_type=F32))
    xn = _ln(DEEPNORM_ALPHA * x_ref[...] + gate_ref[...] * o, lng_ref[...], lnb_ref[...])
    xo_ref[...] = xn
    hmod = xn * (1.0 + sc_ref[...]) + sh_ref[...]
    logits = _mm_nt(wr_ref[...], hmod)
    e = jnp.exp(logits - jnp.max(logits, axis=0, keepdims=True))
    aff_ref[...] = e / jnp.sum(e, axis=0, keepdims=True)


def _outproj(a, b, w, xx, gate, ln_g, ln_b, sh, sc, wr_t, tm, T):
    B, L, D = xx.shape
    tps, nct = L // tm, T // tm
    ka, kb = a.shape[-1], b.shape[-1]
    row = lambda n: pl.BlockSpec((1, n), lambda i: (0, 0))
    xo, aff = pl.pallas_call(
        _outproj_kernel,
        grid=(B * tps,),
        in_specs=[pl.BlockSpec((tm, ka), lambda i: (i, 0)),
                  pl.BlockSpec((tm, kb), lambda i: (i, 0)),
                  pl.BlockSpec((ka + kb, D), lambda i: (0, 0)),
                  pl.BlockSpec((tm, D), lambda i: (i, 0)),
                  _tab_spec(D, tps, nct), row(D), row(D),
                  _tab_spec(D, tps, nct), _tab_spec(D, tps, nct),
                  pl.BlockSpec((N_EXPERTS, D), lambda i: (0, 0))],
        out_specs=[pl.BlockSpec((tm, D), lambda i: (i, 0)),
                   pl.BlockSpec((None, N_EXPERTS, tm), lambda i: (i // tps, 0, i % tps))],
        out_shape=[jax.ShapeDtypeStruct((B * L, D), F32),
                   jax.ShapeDtypeStruct((B, N_EXPERTS, L), F32)],
        compiler_params=_cparams(("parallel",)),
    )(a.reshape(B * L, ka), b.reshape(B * L, kb), w, xx.reshape(B * L, D), gate,
      ln_g.reshape(1, D), ln_b.reshape(1, D), sh, sc, wr_t)
    return xo.reshape(B, L, D), aff


def _topk_kernel(aff_ref, idx_ref, gate_ref, *, T, cap_c, cap_l):
    ri = lax.broadcasted_iota(jnp.int32, (LANES, LANES), 0)
    ci = lax.broadcasted_iota(jnp.int32, (LANES, LANES), 1)
    tri = jnp.where(ri <= ci, 1.0, 0.0).astype(BF16)

    def prefix_incl(x):
        carry = jnp.zeros((x.shape[0], 1), F32)
        outs = []
        for j in range(x.shape[1] // LANES):
            pj = jnp.dot(x[:, j * LANES:(j + 1) * LANES].astype(BF16), tri, preferred_element_type=F32) + carry
            outs.append(pj)
            carry = pj[:, LANES - 1:LANES]
        return jnp.concatenate(outs, axis=1)

    def select(aff, cap, base, row0):
        ne, n = aff.shape
        def bisect(_, lh):
            lo, hi = lh
            mid = lo + lax.shift_right_logical(hi - lo, 1)
            cnt = jnp.sum(jnp.where(aff >= pltpu.bitcast(mid, F32), 1.0, 0.0), axis=1, keepdims=True)
            ge = cnt >= float(cap)
            return jnp.where(ge, mid, lo), jnp.where(ge, hi, mid)

        lo0 = jnp.zeros((ne, 1), jnp.int32)
        hi0 = jnp.full((ne, 1), 0x7F800000, jnp.int32)
        thr_bits, _ = lax.fori_loop(0, 31, bisect, (lo0, hi0))
        thr = pltpu.bitcast(thr_bits, F32)
        gt = aff > thr
        eqf = jnp.where(aff == thr, 1.0, 0.0)
        need = float(cap) - jnp.sum(jnp.where(gt, 1.0, 0.0), axis=1, keepdims=True)
        eq_before = prefix_incl(eqf) - eqf
        self_ = jnp.where(gt, 1.0, jnp.where(eq_before < need, eqf, 0.0))
        cnt = prefix_incl(self_)
        slot = lax.broadcasted_iota(jnp.int32, (cap, 1), 0).astype(F32)
        lane = lax.broadcasted_iota(jnp.int32, (cap, LANES), 1)
        idx_acc = jnp.zeros((cap, LANES), F32)
        gate_acc = jnp.zeros((cap, LANES), F32)
        for e in range(ne):
            ce, se, ae = cnt[e:e + 1, :], self_[e:e + 1, :], aff[e:e + 1, :]
            idx_e = jnp.sum(jnp.where(ce <= slot, 1.0, 0.0), axis=1, keepdims=True)
            hit = jnp.where(ce == slot + 1.0, se, 0.0)
            gate_e = jnp.sum(hit * ae, axis=1, keepdims=True)
            idx_acc = jnp.where(lane == e, idx_e + float(base), idx_acc)
            gate_acc = jnp.where(lane == e, gate_e, gate_acc)
        idx_ref[row0:row0 + cap, :] = idx_acc.astype(jnp.int32)
        gate_ref[row0:row0 + cap, :] = gate_acc

    aff = aff_ref[...]
    select(aff[:, 0:T], cap_c, 0, 0)
    select(aff[:, T:], cap_l, T, cap_c)


def _topk(aff, T):
    B, E, L = aff.shape
    cap_c = EC_CAPACITY * T // E
    cap_l = EC_CAPACITY * (L - T) // E
    cap = cap_c + cap_l
    idx, gate = pl.pallas_call(
        functools.partial(_topk_kernel, T=T, cap_c=cap_c, cap_l=cap_l),
        grid=(B,),
        in_specs=[pl.BlockSpec((None, E, L), lambda b: (b, 0, 0))],
        out_specs=[pl.BlockSpec((None, cap, LANES), lambda b: (b, 0, 0)),
                   pl.BlockSpec((None, cap, LANES), lambda b: (b, 0, 0))],
        out_shape=[jax.ShapeDtypeStruct((B, cap, LANES), jnp.int32),
                   jax.ShapeDtypeStruct((B, cap, LANES), F32)],
        compiler_params=_cparams(("parallel",)),
    )(aff)
    idx = jnp.swapaxes(idx[:, :, :E], 1, 2).reshape(B * E * cap)
    gate = jnp.swapaxes(gate[:, :, :E], 1, 2).reshape(B * E * cap)
    return idx, gate, cap_c, cap


def _gather_kernel(idx_ref, x_ref, sh_ref, sc_ref, o_ref, stage_ref, *, cap_c, cap):
    base = (pl.program_id(0) * pl.num_programs(1) + pl.program_id(1)) * cap

    def body(s, carry):
        stage_ref[pl.ds(s, 1), :] = x_ref[pl.ds(idx_ref[base + s], 1), :]
        return carry

    lax.fori_loop(0, cap, body, 0, unroll=8)
    o_ref[0:cap_c, :] = (stage_ref[0:cap_c, :] * (1.0 + sc_ref[0]) + sh_ref[0]).astype(BF16)
    o_ref[cap_c:, :] = (stage_ref[cap_c:, :] * (1.0 + sc_ref[1]) + sh_ref[1]).astype(BF16)


def _gather(idx, xx, sh, sc, cap_c, cap):
    B, L, D = xx.shape
    E = N_EXPERTS
    tab = pl.BlockSpec((None, 2, 1, D), lambda b, e, idx: (b, 0, 0, 0))
    return pl.pallas_call(
        functools.partial(_gather_kernel, cap_c=cap_c, cap=cap),
        grid_spec=pltpu.PrefetchScalarGridSpec(
            num_scalar_prefetch=1, grid=(B, E),
            in_specs=[pl.BlockSpec((None, L, D), lambda b, e, idx: (b, 0, 0)), tab, tab],
            out_specs=pl.BlockSpec((None, None, cap, D), lambda b, e, idx: (e, b, 0, 0)),
            scratch_shapes=[pltpu.VMEM((cap, D), F32)]),
        out_shape=jax.ShapeDtypeStruct((E, B, cap, D), BF16),
        compiler_params=_cparams(("parallel", "arbitrary")),
    )(idx, xx, sh, sc)


def _expert_kernel(x_ref, wg_ref, wu_ref, wd_ref, o_ref):
    x = x_ref[...]
    g = jnp.dot(x, wg_ref[...].astype(BF16), preferred_element_type=F32)
    u = jnp.dot(x, wu_ref[...].astype(BF16), preferred_element_type=F32)
    hid = (_silu(g) * u).astype(BF16)
    o_ref[...] = jnp.dot(hid, wd_ref[...].astype(BF16), preferred_element_type=F32)


def _experts(xe, w_gate, w_up, w_down, layer, tm):
    E, M, D = xe.shape
    F = w_gate.shape[-1]
    return pl.pallas_call(
        _expert_kernel,
        grid=(E, M // tm),
        in_specs=[pl.BlockSpec((None, tm, D), lambda e, m: (e, m, 0)),
                  pl.BlockSpec((None, None, D, F), lambda e, m: (layer, e, 0, 0)),
                  pl.BlockSpec((None, None, D, F), lambda e, m: (layer, e, 0, 0)),
                  pl.BlockSpec((None, None, F, D), lambda e, m: (layer, e, 0, 0))],
        out_specs=pl.BlockSpec((None, tm, D), lambda e, m: (e, m, 0)),
        out_shape=jax.ShapeDtypeStruct((E, M, D), F32),
        compiler_params=_cparams(("parallel", "arbitrary")),
    )(xe, w_gate, w_up, w_down)


def _combine_kernel(idx_ref, gate_ref, ye_ref, x_ref, gt_ref, lng_ref, lnb_ref, o_ref, *, cap, T, tm):
    e = pl.program_id(1)
    n_e = pl.num_programs(1)
    base = (pl.program_id(0) * n_e + e) * cap

    @pl.when(e == 0)
    def _():
        o_ref[...] = jnp.zeros(o_ref.shape, F32)

    def body(i, carry):
        s0 = base + i * SCATTER_GROUP
        rs = [idx_ref[s0 + j] for j in range(SCATTER_GROUP)]
        new = [o_ref[pl.ds(rs[j], 1), :] + gate_ref[s0 + j] * ye_ref[pl.ds(i * SCATTER_GROUP + j, 1), :]
               for j in range(SCATTER_GROUP)]
        for j in range(SCATTER_GROUP):
            o_ref[pl.ds(rs[j], 1), :] = new[j]
        return carry

    lax.fori_loop(0, cap // SCATTER_GROUP, body, 0)

    @pl.when(e == n_e - 1)
    def _():
        def tile(t, carry):
            rows = pl.ds(pl.multiple_of(t * tm, tm), tm)
            gt = jnp.where(t < T // tm, gt_ref[0], gt_ref[1])
            o_ref[rows, :] = _ln(DEEPNORM_ALPHA * x_ref[rows, :] + gt * o_ref[rows, :], lng_ref[...], lnb_ref[...])
            return carry

        lax.fori_loop(0, o_ref.shape[0] // tm, tile, 0)


def _combine(idx, gate, ye, xx, gt, ln_g, ln_b, tm, T):
    E, B, cap, D = ye.shape
    L = xx.shape[1]
    row = pl.BlockSpec((1, D), lambda b, e, i, g: (0, 0))
    return pl.pallas_call(
        functools.partial(_combine_kernel, cap=cap, T=T, tm=tm),
        grid_spec=pltpu.PrefetchScalarGridSpec(
            num_scalar_prefetch=2, grid=(B, E),
            in_specs=[pl.BlockSpec((None, None, cap, D), lambda b, e, i, g: (e, b, 0, 0)),
                      pl.BlockSpec((None, L, D), lambda b, e, i, g: (b, 0, 0)),
                      pl.BlockSpec((None, 2, 1, D), lambda b, e, i, g: (b, 0, 0, 0)),
                      row, row],
            out_specs=pl.BlockSpec((None, L, D), lambda b, e, i, g: (b, 0, 0))),
        out_shape=jax.ShapeDtypeStruct((B, L, D), F32),
        compiler_params=_cparams(("parallel", "arbitrary")),
    )(idx, gate, ye, xx, gt, ln_g.reshape(1, D), ln_b.reshape(1, D))


def _na_plan(rows, kh):
    band = min(kh + NA_GROUP - 1, rows)
    plan, keys = [], []
    for g in range(rows // NA_GROUP):
        q0 = NA_GROUP * g
        win = [min(max(q0 + a - kh // 2, 0), rows - kh) for a in range(NA_GROUP)]
        ub = min(win[0], rows - band)
        key = (ub - q0, tuple(w - q0 for w in win))
        if key not in keys:
            keys.append(key)
        plan.append((ub, keys.index(key)))
    return band, plan, keys


def _na_kernel(plan_ref, q_ref, k_ref, v_ref, bias_ref, o_ref, *, T, nct, band, scale):
    i = pl.program_id(2)
    q = q_ref[...]
    lane = lax.broadcasted_iota(jnp.int32, q.shape, 1)
    kc = k_ref[0:T, :].astype(BF16)
    vc = v_ref[0:T, :].astype(BF16)
    qh = [jnp.where(lane < NA_DH, q, 0.0).astype(BF16), jnp.where(lane >= NA_DH, q, 0.0).astype(BF16)]

    def ctx_block(j):
        return (lambda: lax.dot_general(qh[j], kc, _NT, preferred_element_type=F32) * scale), (lambda: vc)

    @pl.when(i < nct)
    def _():
        outs = []
        for j in range(2):
            score, value = ctx_block(j)
            outs.append(_attend([score], [value]))
        o_ref[...] = jnp.where(lane < NA_DH, outs[0], outs[1]).astype(BF16)

    @pl.when(i >= nct)
    def _():
        t0 = pl.multiple_of(T + plan_ref[0, i - nct] * GRID_W, GRID_W)
        kb = k_ref[pl.ds(t0, band * GRID_W), :].astype(BF16)
        vb = v_ref[pl.ds(t0, band * GRID_W), :].astype(BF16)
        outs = []
        for j in range(2):
            score, value = ctx_block(j)
            scores, values = [score], [value]
            for k0, kn in _key_blocks(band * GRID_W):
                scores.append(lambda j=j, k0=k0, kn=kn: lax.dot_general(
                    qh[j], kb[k0:k0 + kn, :], _NT, preferred_element_type=F32) * scale + bias_ref[j, :, k0:k0 + kn])
                values.append(lambda k0=k0, kn=kn: vb[k0:k0 + kn, :])
            outs.append(_attend(scores, values))
        o_ref[...] = jnp.where(lane < NA_DH, outs[0], outs[1]).astype(BF16)


def _na_bias_tables(rpb, band, keys, kh):
    w = jnp.arange(GRID_W)
    c0 = jnp.clip(w - NA_KW // 2, 0, GRID_W - NA_KW)
    col_ok = (w[None, :] >= c0[:, None]) & (w[None, :] < c0[:, None] + NA_KW)
    col_off = jnp.clip(w[None, :] - w[:, None] + (NA_KW - 1), 0, 2 * NA_KW - 2)
    onehot = (col_off[:, :, None] == jnp.arange(2 * NA_KW - 1)).astype(F32)
    base = jnp.einsum('hrm,wcm->hrwc', rpb.astype(F32), onehot, precision=HIGHEST)
    base = jnp.where(col_ok[None, None], base, NEG_BIG)
    masked = jnp.full((NA_HEADS, GRID_W, GRID_W), NEG_BIG, F32)
    tabs = []
    for ub_off, win_offs in keys:
        per_row = []
        for a in range(NA_GROUP):
            blocks = []
            for u in range(band):
                rel = ub_off + u
                inside = win_offs[a] <= rel < win_offs[a] + kh
                blocks.append(base[:, rel - a + NA_KH - 1] if inside else masked)
            per_row.append(jnp.concatenate(blocks, axis=-1))
        tabs.append(jnp.concatenate(per_row, axis=1))
    return jnp.stack(tabs)


def _na(p, rpb, T):
    B, L, _ = p.shape
    rows = (L - T) // GRID_W
    kh = min(NA_KH, rows)
    tq = NA_GROUP * GRID_W
    assert T % tq == 0 and rows % NA_GROUP == 0
    nct = T // tq
    hp = NA_HEADS // 2
    band, plan, keys = _na_plan(rows, kh)
    bias = _na_bias_tables(rpb, band, keys, kh)
    plan = jnp.asarray(plan, jnp.int32).T
    return pl.pallas_call(
        functools.partial(_na_kernel, T=T, nct=nct, band=band, scale=NA_DH ** -0.5),
        grid_spec=pltpu.PrefetchScalarGridSpec(
            num_scalar_prefetch=1, grid=(B, hp, nct + rows // NA_GROUP),
            in_specs=[pl.BlockSpec((None, tq, LANES), lambda b, h, i, pr: (b, i, h)),
                      pl.BlockSpec((None, L, LANES), lambda b, h, i, pr: (b, 0, hp + h)),
                      pl.BlockSpec((None, L, LANES), lambda b, h, i, pr: (b, 0, 2 * hp + h)),
                      pl.BlockSpec((None, 2, tq, band * GRID_W),
                                   lambda b, h, i, pr: (pr[1, jnp.maximum(i - nct, 0)], h, 0, 0))],
            out_specs=pl.BlockSpec((None, tq, LANES), lambda b, h, i, pr: (b, i, h))),
        out_shape=jax.ShapeDtypeStruct((B, L, NA_HEADS * NA_DH), BF16),
        compiler_params=_cparams(("parallel", "parallel", "arbitrary")),
    )(plan, p, p, p, bias)


def _hgrn_kernel(q_ref, zf_ref, zb_ref, v_ref, gz_ref, llb_ref, l1m_ref, oml_ref, gain_ref,
                 o_ref,
                 qg_ref, ut_ref, oi_ref, el_ref, of_ref, ob_ref, *, T, L):
    n_chunks = L // CHUNK
    nc = T // CHUNK
    ri = lax.broadcasted_iota(jnp.int32, (CHUNK, CHUNK), 0)
    ci = lax.broadcasted_iota(jnp.int32, (CHUNK, CHUNK), 1)
    row = lax.broadcasted_iota(jnp.int32, (CHUNK, LANES), 0)
    eye = ri == ci
    top_bit = 31 - lax.clz(ri ^ ci)
    split_at = [jnp.where(ci < ri, top_bit, -1), jnp.where(ci > ri, top_bit, -1)]

    def shifted(x, k):
        return x if k == 0 else pltpu.roll(x, (-k) % CHUNK, 0)

    def scan_cumsum(x, d):
        k = 1
        while k < CHUNK:
            if d == 0:
                x = x + jnp.where(row >= k, shifted(x, -k), 0.0)
            else:
                x = x + jnp.where(row < CHUNK - k, shifted(x, k), 0.0)
            k *= 2
        return x

    def block_ref(g, size, r_in):
        if size >= 8:
            g3 = g.reshape(CHUNK // size, size, LANES)
            return jnp.broadcast_to(g3[:, r_in:r_in + 1, :], g3.shape).reshape(CHUNK, LANES)
        pos = row & (size - 1)
        out = shifted(g, r_in)
        for m in range(1, size):
            out = jnp.where(pos == m, shifted(g, r_in - m), out)
        return out

    def chunk_prep(n, carry):
        rows = pl.ds(pl.multiple_of(n * CHUNK, CHUNK), CHUNK)
        qs = _silu(q_ref[rows, :])
        v = v_ref[rows, :]
        vt = v.T
        for d in range(2):
            z = (zf_ref if d == 0 else zb_ref)[rows, :]
            t = jnp.exp(-jnp.abs(z))
            log_sig = jnp.minimum(z, 0.0) - jnp.log1p(t)
            logf = jnp.logaddexp(llb_ref[d:d + 1, :], l1m_ref[d:d + 1, :] + log_sig)
            kk = oml_ref[d:d + 1, :] * (jnp.where(z > 0, t, 1.0) / (1.0 + t))
            g = scan_cumsum(logf, d)
            g_tot = g[CHUNK - 1:CHUNK, :] if d == 0 else g[0:1, :]
            a = jnp.where(eye, jnp.sum(qs * kk, axis=1, keepdims=True), 0.0)
            half = CHUNK // 2
            while half >= 1:
                size = 2 * half
                e = jnp.exp(-jnp.abs(g - block_ref(g, size, half if d == 0 else half - 1)))
                a = jnp.where(split_at[d] == half.bit_length() - 1, _mm_nt(qs * e, kk * e), a)
                half //= 2
            oi_ref[d, rows, :] = _mm(a, v)
            qg_ref[d, rows, :] = (qs * jnp.exp(g)).astype(BF16)
            ut_ref[d * n_chunks + n] = _mm(vt, kk * jnp.exp(g_tot - g))
            el_ref[d * n_chunks + n] = jnp.broadcast_to(jnp.exp(g_tot), (8, LANES))
        return carry

    lax.fori_loop(0, n_chunks, chunk_prep, 0, unroll=4)

    def step(d, n, st):
        rows = pl.ds(pl.multiple_of(n * CHUNK, CHUNK), CHUNK)
        o = _mm_nt(qg_ref[d, rows, :], st) + oi_ref[d, rows, :]
        st = st * el_ref[d * n_chunks + n][0:1, :] + ut_ref[d * n_chunks + n]
        return st, o, rows

    def scan_body(i, carry):
        sf, sb = carry
        sf, o_f, rows_f = step(0, i, sf)
        sb, o_b, rows_b = step(1, _bwd_chunk(i, nc, n_chunks), sb)
        of_ref[rows_f, :] = o_f
        ob_ref[rows_b, :] = o_b
        return sf, sb

    zero = jnp.zeros((HG_DV, HG_DK), F32)
    lax.fori_loop(0, n_chunks, scan_body, (zero, zero), unroll=SCAN_UNROLL)
    o = of_ref[...] + ob_ref[...]
    o_ref[...] = (_rms(o) * gain_ref[...] * _silu(gz_ref[...])).astype(BF16)


def _hgrn(p, llb, l1m, oml, gain, T):
    B, L, _ = p.shape
    H = HG_HEADS
    n_chunks = L // CHUNK
    col = lambda off: pl.BlockSpec((None, L, LANES), lambda b, h: (b, 0, off + h))
    lbs = pl.BlockSpec((2, LANES), lambda b, h: (0, h))
    return pl.pallas_call(
        functools.partial(_hgrn_kernel, T=T, L=L),
        grid=(B, H),
        in_specs=[col(12), col(16), col(20), col(24), col(28), lbs, lbs, lbs,
                  pl.BlockSpec((1, LANES), lambda b, h: (0, 0))],
        out_specs=pl.BlockSpec((None, L, LANES), lambda b, h: (b, 0, h)),
        out_shape=jax.ShapeDtypeStruct((B, L, H * HG_DV), BF16),
        scratch_shapes=[pltpu.VMEM((2, L, LANES), BF16),
                        pltpu.VMEM((2 * n_chunks, HG_DV, HG_DK), F32),
                        pltpu.VMEM((2, L, LANES), F32),
                        pltpu.VMEM((2 * n_chunks, 8, LANES), F32),
                        pltpu.VMEM((L, LANES), F32), pltpu.VMEM((L, LANES), F32)],
        compiler_params=_cparams(("parallel", "parallel")),
    )(p, p, p, p, p, llb, l1m, oml, gain.reshape(1, LANES))


def _rot_cols(w):
    q = MLA_ROPE // 4
    return jnp.concatenate([-w[..., q:2 * q], w[..., 0:q], -w[..., 3 * q:4 * q], w[..., 2 * q:3 * q]], -1)


def _even_w_in(w):
    d = w.shape[0]
    qkv_z = GDN_HEADS * (2 * GDN_DK + GDN_DV) + GDN_HEADS * GDN_DV
    ab = w[:, qkv_z:qkv_z + 4 * GDN_HEADS]
    o = qkv_z + 4 * GDN_HEADS
    lora = w[:, o:o + MLA_Q_LORA + MLA_KV_LORA]
    kr = w[:, o + MLA_Q_LORA + MLA_KV_LORA:]
    pad = jnp.zeros((d, LANES - 4 * GDN_HEADS), w.dtype)
    return jnp.concatenate([w[:, :qkv_z], lora, kr, _rot_cols(kr), ab, pad], axis=1).astype(BF16)


def _mla_wq(w):
    w = w.reshape(w.shape[0], MLA_HEADS, MLA_NOPE + MLA_ROPE)
    rope = w[..., MLA_NOPE:]
    return jnp.concatenate([w[..., :MLA_NOPE], rope, _rot_cols(rope)], -1).reshape(w.shape[0], -1).astype(BF16)


def _rope_table(T, S):
    half = MLA_ROPE // 4
    inv = ROPE_THETA ** (-jnp.arange(half, dtype=F32) / half)
    pos = jnp.arange(S)
    ang_r = (pos // GRID_W).astype(F32)[:, None] * inv
    ang_c = (pos % GRID_W).astype(F32)[:, None] * inv
    ang = jnp.concatenate([ang_r, ang_r, ang_c, ang_c], -1)
    lat = jnp.concatenate([jnp.cos(ang), jnp.sin(ang)], -1)
    ctx = jnp.concatenate([jnp.ones((T, MLA_ROPE), F32), jnp.zeros((T, MLA_ROPE), F32)], -1)
    return jnp.concatenate([ctx, lat], 0)


def kernel(x, c, ctx, c_ctx, w_mod, b_mod, ln_g, ln_b, even_w_in, gdn_conv, gdn_a_log, gdn_dt_bias, gdn_norm,
           mla_q_norm, mla_kv_norm, mla_w_uq, mla_w_ukv, even_w_out, odd_w_in, na_rpb, hg_lb, hg_norm, odd_w_out,
           moe_router, moe_w_gate, moe_w_up, moe_w_down):
    B, S, D = x.shape
    T = ctx.shape[1]
    L = T + S
    tm = min(256, T)
    depth = w_mod.shape[0]

    xx = jnp.concatenate([ctx, x], axis=1)
    cc = jnp.concatenate([c, c_ctx[None, :], jnp.zeros((16 - B - 1, D), F32)], axis=0)
    mods = _modulation(cc, w_mod, b_mod)
    mods = mods.reshape(depth, 16, N_MOD, D)
    tabs = jnp.stack([jnp.broadcast_to(mods[:, B:B + 1], (depth, B, N_MOD, D)), mods[:, :B]], axis=2)
    tabs = jnp.transpose(tabs, (0, 3, 1, 2, 4))[:, :, :, :, None, :]

    lb_all = jnp.cumsum(jax.nn.softmax(hg_lb.astype(F32), axis=0), axis=0)
    lb_all = lb_all - lb_all[:1]
    cs = _rope_table(T, S)

    for layer in range(depth):
        j = layer // 2
        tab = tabs[layer]
        if layer % 2 == 0:
            p = _inproj(xx, tab[0], tab[1], _even_w_in(even_w_in[j]), tm, T)
            mix_a = _gdn(p, gdn_conv[j], (-jnp.exp(gdn_a_log[j].astype(F32))).reshape(-1),
                         gdn_dt_bias[j].astype(F32).reshape(-1), gdn_norm[j], T)
            q, k, v = _mlaproj(p, cs, mla_q_norm[j], mla_kv_norm[j], _mla_wq(mla_w_uq[j]),
                               mla_w_ukv[j].astype(BF16), tm)
            mix_b = _mla_attn(q, k, v, T, tm)
            w_out = even_w_out[j]
        else:
            p = _inproj(xx, tab[0], tab[1], odd_w_in[j].astype(BF16), tm, T)
            mix_a = _na(p, na_rpb[j], T)
            lb = lb_all[j]
            mix_b = _hgrn(p, jnp.log(lb), jnp.log1p(-lb), 1.0 - lb, hg_norm[j], T)
            w_out = odd_w_out[j]
        xx, aff = _outproj(mix_a, mix_b, w_out.astype(BF16), xx, tab[2], ln_g[layer, 0], ln_b[layer, 0],
                           tab[3], tab[4], moe_router[layer].T, tm, T)
        idx, gate, cap_c, cap = _topk(aff, T)
        xe = _gather(idx, xx, tab[3], tab[4], cap_c, cap)
        m_rows = B * cap
        ye = _experts(xe.reshape(N_EXPERTS, m_rows, D), moe_w_gate, moe_w_up, moe_w_down, layer,
                      m_rows // 4 if m_rows % 64 == 0 else m_rows)
        xx = _combine(idx, gate, ye.reshape(N_EXPERTS, B, cap, D), xx, tab[5], ln_g[layer, 1], ln_b[layer, 1],
                      tm, T)
    return xx[:, T:, :]
```

```python
import functools
import math

import jax
import jax.numpy as jnp
from jax import lax
from jax.experimental import pallas as pl
from jax.experimental.pallas import tpu as pltpu

F32 = jnp.float32
BF16 = jnp.bfloat16
HIGHEST = lax.Precision.HIGHEST

DEPTH = 4
GRID_W = 64
N_MOD = 6
EPS = 1e-6
DEEPNORM_ALPHA = (2.0 * DEPTH) ** 0.25

GDN_HEADS = 4
GDN_DK = 128
GDN_DV = 128
GDN_CONV = 5
MLA_HEADS = 4
MLA_Q_LORA = 256
MLA_KV_LORA = 256
MLA_NOPE = 128
MLA_ROPE = 64
MLA_DV = 128
ROPE_THETA = 10000.0
NA_HEADS = 8
NA_DH = 64
NA_KH = 8
NA_KW = 16
HG_HEADS = 4
HG_DK = 128
HG_DV = 128
N_EXPERTS = 16
EC_CAPACITY = 2

CHUNK = 64
NA_GROUP = 4
PREP_CHUNKS = 9
SCAN_UNROLL = 4
SCATTER_GROUP = 8
LANES = 128
MLA_QK_PAD = 256
NEG_BIG = -1e30
VMEM_LIMIT = 56 * 1024 * 1024

_NT = (((1,), (1,)), ((), ()))


def _cparams(sem):
    return pltpu.CompilerParams(dimension_semantics=sem, vmem_limit_bytes=VMEM_LIMIT)


def _mm(a, b):
    return jnp.dot(a.astype(BF16), b.astype(BF16), preferred_element_type=F32)


def _mm_nt(a, b):
    return lax.dot_general(a.astype(BF16), b.astype(BF16), _NT, preferred_element_type=F32)


def _mm32(a, b):
    return jnp.dot(a, b, precision=HIGHEST, preferred_element_type=F32)


def _mm32_nt(a, b):
    return lax.dot_general(a, b, _NT, precision=HIGHEST, preferred_element_type=F32)


def _silu(x):
    return x * jax.nn.sigmoid(x)


def _rms(x):
    return x * lax.rsqrt(jnp.mean(x * x, -1, keepdims=True) + EPS)


def _ln(y, g, b):
    yc = y - jnp.mean(y, -1, keepdims=True)
    return yc * lax.rsqrt(jnp.mean(yc * yc, -1, keepdims=True) + EPS) * g + b


def _mod_kernel(c_ref, w_ref, b_ref, o_ref):
    o_ref[...] = _mm32(_silu(c_ref[...]), w_ref[...]) + b_ref[...]


def _modulation(cc, w_mod, b_mod):
    depth, d, n = w_mod.shape
    tn = n // 4
    return pl.pallas_call(
        _mod_kernel,
        grid=(depth, n // tn),
        in_specs=[pl.BlockSpec((cc.shape[0], d), lambda l, j: (0, 0)),
                  pl.BlockSpec((None, d, tn), lambda l, j: (l, 0, j)),
                  pl.BlockSpec((None, 1, tn), lambda l, j: (l, 0, j))],
        out_specs=pl.BlockSpec((None, cc.shape[0], tn), lambda l, j: (l, 0, j)),
        out_shape=jax.ShapeDtypeStruct((depth, cc.shape[0], n), F32),
        compiler_params=_cparams(("parallel", "parallel")),
    )(cc, w_mod, b_mod.reshape(depth, 1, n))


def _inproj_kernel(x_ref, sh_ref, sc_ref, w_ref, o_ref):
    u = x_ref[...] * (1.0 + sc_ref[...]) + sh_ref[...]
    o_ref[...] = jnp.dot(u.astype(BF16), w_ref[...], preferred_element_type=F32)


def _tab_spec(d, tps, nct):
    return pl.BlockSpec((None, None, 1, d), lambda i: (i // tps, jnp.where(i % tps >= nct, 1, 0), 0, 0))


def _inproj(xx, sh, sc, w, tm, T):
    B, L, D = xx.shape
    n = w.shape[1]
    tps, nct = L // tm, T // tm
    out = pl.pallas_call(
        _inproj_kernel,
        grid=(B * tps,),
        in_specs=[pl.BlockSpec((tm, D), lambda i: (i, 0)),
                  _tab_spec(D, tps, nct), _tab_spec(D, tps, nct),
                  pl.BlockSpec((D, n), lambda i: (0, 0))],
        out_specs=pl.BlockSpec((tm, n), lambda i: (i, 0)),
        out_shape=jax.ShapeDtypeStruct((B * L, n), F32),
        compiler_params=_cparams(("parallel",)),
    )(xx.reshape(B * L, D), sh, sc, w)
    return out.reshape(B, L, n)


def _scan_masks(d):
    ri = lax.broadcasted_iota(jnp.int32, (CHUNK, CHUNK), 0)
    ci = lax.broadcasted_iota(jnp.int32, (CHUNK, CHUNK), 1)
    if d == 0:
        return ci <= ri, ci < ri, ri <= ci, ci == ri
    return ci >= ri, ci > ri, ri >= ci, ci == ri


def _bwd_chunk(i, nc, n):
    return jnp.where(i < nc, nc - 1 - i, n - 1 - (i - nc))


def _gdn_kernel(nega_ref, dtb_ref, q_ref, k_ref, v_ref, z_ref, ab_ref, cwq_ref, cwk_ref, cwv_ref, gain_ref,
                o_ref,
                pad_ref, qn_ref, kn_ref, vn_ref, ncs_ref, kw_ref, ou_ref, qp_ref, el_ref, of_ref, ob_ref,
                *, T, L):
    h = pl.program_id(1)
    n_chunks = L // CHUNK
    nc = T // CHUNK
    prep = max(c for c in range(1, PREP_CHUNKS + 1) if n_chunks % c == 0)

    def conv_prep(src_ref, cw_ref, dst_ref, post):
        zero8 = jnp.zeros((8, LANES), F32)
        pad_ref[0:8, :] = zero8
        pad_ref[8:8 + T, :] = src_ref[0:T, :]
        pad_ref[8 + T:16 + T, :] = zero8
        pad_ref[16 + T:16 + L, :] = src_ref[T:L, :]
        pad_ref[16 + L:24 + L, :] = zero8
        cw = cw_ref[...]

        def body(n, carry):
            r0 = pl.multiple_of(n * CHUNK, CHUNK)
            p0 = r0 + jnp.where(r0 < T, 8, 16) - GDN_CONV // 2
            acc = jnp.zeros((CHUNK, LANES), F32)
            for j in range(GDN_CONV):
                acc = acc + pad_ref[pl.ds(p0 + j, CHUNK), :] * cw[j:j + 1, :]
            dst_ref[pl.ds(r0, CHUNK), :] = post(_silu(acc))
            return carry

        lax.fori_loop(0, n_chunks, body, 0, unroll=prep)

    def l2n(y):
        return y * lax.rsqrt(jnp.sum(y * y, -1, keepdims=True) + EPS)

    conv_prep(q_ref, cwq_ref, qn_ref, lambda y: l2n(y) * (GDN_DK ** -0.5))
    conv_prep(k_ref, cwk_ref, kn_ref, l2n)
    conv_prep(v_ref, cwv_ref, vn_ref, lambda y: y)

    lane = lax.broadcasted_iota(jnp.int32, (CHUNK, LANES), 1)

    def chunk_prep(i, carry):
        chains = []
        for c in range(prep):
            n = prep * i + c
            rows = pl.ds(pl.multiple_of(n * CHUNK, CHUNK), CHUNK)
            q, k, v, ab = qn_ref[rows, :], kn_ref[rows, :], vn_ref[rows, :], ab_ref[rows, :]
            kq = _mm_nt(jnp.concatenate([k, q], axis=0), k)
            for d in range(2):
                chains.append(dict(n=n, d=d, rows=rows, q=q, k=k, v=v, ab=ab, kk=kq[:CHUNK], qk=kq[CHUNK:]))
        for ch in chains:
            d = ch["d"]
            incl, strict, incl_t, eye = _scan_masks(d)
            a_col = jnp.sum(jnp.where(lane == 4 * d + h, ch["ab"], 0.0), axis=1, keepdims=True)
            b_col = jnp.sum(jnp.where(lane == 8 + 4 * d + h, ch["ab"], 0.0), axis=1, keepdims=True)
            g_col = nega_ref[4 * d + h] * jax.nn.softplus(a_col + dtb_ref[4 * d + h])
            beta = jax.nn.sigmoid(b_col)
            g_row = jnp.sum(jnp.where(eye, g_col, 0.0), axis=0, keepdims=True)
            gc_col = jnp.sum(jnp.where(incl, g_row, 0.0), axis=1, keepdims=True)
            gc_row = jnp.sum(jnp.where(incl_t, g_col, 0.0), axis=0, keepdims=True)
            g_tot = jnp.sum(g_col, axis=0, keepdims=True)
            ch["decay"] = jnp.where(incl, jnp.exp(jnp.where(incl, gc_col - gc_row, 0.0)), 0.0)
            ch["kb"] = ch["k"] * beta
            ch["vb"] = ch["v"] * beta
            ch["beta"] = beta
            ch["e_col"] = jnp.exp(gc_col)
            ch["kd"] = ch["k"] * jnp.exp(g_tot - gc_col)
            ch["e_tot"] = jnp.exp(g_tot)
        for ch in chains:
            _, strict, _, eye = _scan_masks(ch["d"])
            a = jnp.where(strict, ch["beta"] * ch["kk"] * ch["decay"], 0.0)
            ch["p"] = jnp.where(eye, 1.0, 0.0) - a
            ch["bp"] = a
        for _ in range(int(math.log2(CHUNK)) - 1):
            for ch in chains:
                ch["bp"] = _mm(ch["bp"], ch["bp"])
            for ch in chains:
                ch["p"] = ch["p"] + _mm(ch["p"], ch["bp"])
        for ch in chains:
            incl = _scan_masks(ch["d"])[0]
            ch["uw"] = _mm(ch["p"], jnp.concatenate([ch["vb"], ch["kb"] * ch["e_col"]], axis=1)).astype(BF16)
            ch["qk"] = jnp.where(incl, ch["qk"] * ch["decay"], 0.0)
        for ch in chains:
            d, n, rows = ch["d"], ch["n"], ch["rows"]
            kd_uw = _mm(ch["kd"].T, ch["uw"])
            qk_uw = _mm(ch["qk"], ch["uw"])
            ncs_ref[d * n_chunks + n] = kd_uw[:, :GDN_DV]
            kw_ref[d * n_chunks + n] = kd_uw[:, GDN_DV:].astype(BF16)
            ou_ref[d, rows, :] = qk_uw[:, :GDN_DV]
            qp_ref[d, rows, :] = (ch["q"] * ch["e_col"] - qk_uw[:, GDN_DV:]).astype(BF16)
            el_ref[d * n_chunks + n] = jnp.broadcast_to(ch["e_tot"], (8, LANES))
        return carry

    lax.fori_loop(0, n_chunks // prep, chunk_prep, 0)

    def step(d, n, s):
        rows = pl.ds(pl.multiple_of(n * CHUNK, CHUNK), CHUNK)
        sb = s.astype(BF16)
        o = jnp.dot(qp_ref[d, rows, :], sb, preferred_element_type=F32) + ou_ref[d, rows, :]
        s = (s * el_ref[d * n_chunks + n][0:1, :] + ncs_ref[d * n_chunks + n]
             - jnp.dot(kw_ref[d * n_chunks + n], sb, preferred_element_type=F32))
        return s, o, rows

    def scan_body(i, carry):
        sf, sb = carry
        sf, o_f, rows_f = step(0, i, sf)
        sb, o_b, rows_b = step(1, _bwd_chunk(i, nc, n_chunks), sb)
        of_ref[rows_f, :] = o_f
        ob_ref[rows_b, :] = o_b
        return sf, sb

    zero = jnp.zeros((GDN_DK, GDN_DV), F32)
    lax.fori_loop(0, n_chunks, scan_body, (zero, zero), unroll=SCAN_UNROLL)

    o = of_ref[...] + ob_ref[...]
    o_ref[...] = (_rms(o) * gain_ref[...] * _silu(z_ref[...])).astype(BF16)


def _gdn(p, conv_w, neg_a, dt_bias, gain, T):
    B, L, _ = p.shape
    H = GDN_HEADS
    n_chunks = L // CHUNK
    col = lambda off: pl.BlockSpec((None, L, LANES), lambda b, h: (b, 0, off + h))
    cw = lambda off: pl.BlockSpec((GDN_CONV, LANES), lambda b, h: (0, off + h))
    smem = pl.BlockSpec(memory_space=pltpu.SMEM)
    return pl.pallas_call(
        functools.partial(_gdn_kernel, T=T, L=L),
        grid=(B, H),
        in_specs=[smem, smem, col(0), col(H), col(2 * H), col(3 * H),
                  pl.BlockSpec((None, L, LANES), lambda b, h: (b, 0, 21)),
                  cw(0), cw(H), cw(2 * H),
                  pl.BlockSpec((1, LANES), lambda b, h: (0, 0))],
        out_specs=pl.BlockSpec((None, L, LANES), lambda b, h: (b, 0, h)),
        out_shape=jax.ShapeDtypeStruct((B, L, H * GDN_DV), BF16),
        scratch_shapes=[pltpu.VMEM((L + 24, LANES), F32),
                        pltpu.VMEM((L, LANES), F32), pltpu.VMEM((L, LANES), F32), pltpu.VMEM((L, LANES), F32),
                        pltpu.VMEM((2 * n_chunks, GDN_DK, GDN_DV), F32),
                        pltpu.VMEM((2 * n_chunks, GDN_DK, GDN_DK), BF16),
                        pltpu.VMEM((2, L, LANES), F32), pltpu.VMEM((2, L, LANES), BF16),
                        pltpu.VMEM((2 * n_chunks, 8, LANES), F32),
                        pltpu.VMEM((L, LANES), F32), pltpu.VMEM((L, LANES), F32)],
        compiler_params=_cparams(("parallel", "parallel")),
    )(neg_a, dt_bias, p, p, p, p, p, conv_w, conv_w, conv_w, gain.reshape(1, LANES))


def _mlaproj_kernel(ql_ref, kvl_ref, kr_ref, cs_ref, qg_ref, kvg_ref, wq_ref, wkv_ref, q_ref, k_ref, v_ref):
    q = jnp.dot((_rms(ql_ref[...]) * qg_ref[...]).astype(BF16), wq_ref[...], preferred_element_type=F32)
    kv = jnp.dot((_rms(kvl_ref[...]) * kvg_ref[...]).astype(BF16), wkv_ref[...], preferred_element_type=F32)
    cs = cs_ref[...]
    lane = lax.broadcasted_iota(jnp.int32, cs.shape, 1)

    def rope(blk):
        t = blk * cs
        return jnp.where(lane < MLA_ROPE, t + pltpu.roll(t, MLA_ROPE, 1), 0.0)

    kr = rope(kr_ref[...]).astype(BF16)
    for h in range(MLA_HEADS):
        o = h * MLA_QK_PAD
        q_ref[:, o:o + MLA_NOPE] = q[:, o:o + MLA_NOPE].astype(BF16)
        q_ref[:, o + MLA_NOPE:o + MLA_QK_PAD] = rope(q[:, o + MLA_NOPE:o + MLA_QK_PAD]).astype(BF16)
        k_ref[:, o:o + MLA_NOPE] = kv[:, o:o + MLA_NOPE].astype(BF16)
        k_ref[:, o + MLA_NOPE:o + MLA_QK_PAD] = kr
        v_ref[h * MLA_DV:(h + 1) * MLA_DV, :] = kv[:, o + MLA_NOPE:o + MLA_QK_PAD].T.astype(BF16)


def _mlaproj(p, cs, q_gain, kv_gain, wq, wkv, tm):
    B, L, n = p.shape
    tps = L // tm
    p2 = p.reshape(B * L, n)
    hq = MLA_HEADS * MLA_QK_PAD
    q, k, v = pl.pallas_call(
        _mlaproj_kernel,
        grid=(B * tps,),
        in_specs=[pl.BlockSpec((tm, MLA_Q_LORA), lambda i: (i, 8)),
                  pl.BlockSpec((tm, MLA_KV_LORA), lambda i: (i, 9)),
                  pl.BlockSpec((tm, LANES), lambda i: (i, 20)),
                  pl.BlockSpec((tm, LANES), lambda i: (i % tps, 0)),
                  pl.BlockSpec((1, MLA_Q_LORA), lambda i: (0, 0)),
                  pl.BlockSpec((1, MLA_KV_LORA), lambda i: (0, 0)),
                  pl.BlockSpec((MLA_Q_LORA, hq), lambda i: (0, 0)),
                  pl.BlockSpec((MLA_KV_LORA, hq), lambda i: (0, 0))],
        out_specs=[pl.BlockSpec((tm, hq), lambda i: (i, 0)),
                   pl.BlockSpec((tm, hq), lambda i: (i, 0)),
                   pl.BlockSpec((None, MLA_HEADS * MLA_DV, tm), lambda i: (i // tps, 0, i % tps))],
        out_shape=[jax.ShapeDtypeStruct((B * L, hq), BF16),
                   jax.ShapeDtypeStruct((B * L, hq), BF16),
                   jax.ShapeDtypeStruct((B, MLA_HEADS * MLA_DV, L), BF16)],
        compiler_params=_cparams(("parallel",)),
    )(p2, p2, p2, cs, q_gain.reshape(1, -1), kv_gain.reshape(1, -1), wq, wkv)
    return q.reshape(B, L, hq), k.reshape(B, L, hq), v


def _mla_attn_kernel(q_ref, k_ref, vt_ref, o_ref, *, T, nct, scale):
    qt = pl.program_id(2)
    q = q_ref[...]

    def attend(n):
        st = lax.dot_general(k_ref[0:n, :], q, _NT, preferred_element_type=F32) * scale
        p = jnp.exp(st - jnp.max(st, axis=0, keepdims=True))
        l = jnp.sum(p, axis=0, keepdims=True)
        ot = jnp.dot(vt_ref[:, 0:n], p.astype(BF16), preferred_element_type=F32)
        o_ref[...] = (ot / l).T.astype(BF16)

    @pl.when(qt < nct)
    def _():
        attend(T)

    @pl.when(qt >= nct)
    def _():
        attend(k_ref.shape[0])


def _mla_attn(q, k, v, T, tq):
    B, L, _ = q.shape
    H = MLA_HEADS
    scale = (MLA_NOPE + MLA_ROPE) ** -0.5
    return pl.pallas_call(
        functools.partial(_mla_attn_kernel, T=T, nct=T // tq, scale=scale),
        grid=(B, H, L // tq),
        in_specs=[pl.BlockSpec((None, tq, MLA_QK_PAD), lambda b, h, i: (b, i, h)),
                  pl.BlockSpec((None, L, MLA_QK_PAD), lambda b, h, i: (b, 0, h)),
                  pl.BlockSpec((None, MLA_DV, L), lambda b, h, i: (b, h, 0))],
        out_specs=pl.BlockSpec((None, tq, MLA_DV), lambda b, h, i: (b, i, h)),
        out_shape=jax.ShapeDtypeStruct((B, L, H * MLA_DV), BF16),
        compiler_params=_cparams(("parallel", "parallel", "arbitrary")),
    )(q, k, v)


def _outproj_kernel(a_ref, b_ref, w_ref, x_ref, gate_ref, lng_ref, lnb_ref, sh_ref, sc_ref, wr_ref,
                    xo_ref, aff_ref):
    ka = a_ref.shape[1]
    o = (jnp.dot(a_ref[...], w_ref[0:ka, :], preferred_element_type=F32)
         + jnp.dot(b_ref[...], w_ref[ka:, :], preferred_element_type=F32))
    xn = _ln(DEEPNORM_ALPHA * x_ref[...] + gate_ref[...] * o, lng_ref[...], lnb_ref[...])
    xo_ref[...] = xn
    hmod = xn * (1.0 + sc_ref[...]) + sh_ref[...]
    logits = _mm_nt(wr_ref[...], hmod)
    e = jnp.exp(logits - jnp.max(logits, axis=0, keepdims=True))
    aff_ref[...] = e / jnp.sum(e, axis=0, keepdims=True)


def _outproj(a, b, w, xx, gate, ln_g, ln_b, sh, sc, wr_t, tm, T):
    B, L, D = xx.shape
    tps, nct = L // tm, T // tm
    ka, kb = a.shape[-1], b.shape[-1]
    row = lambda n: pl.BlockSpec((1, n), lambda i: (0, 0))
    xo, aff = pl.pallas_call(
        _outproj_kernel,
        grid=(B * tps,),
        in_specs=[pl.BlockSpec((tm, ka), lambda i: (i, 0)),
                  pl.BlockSpec((tm, kb), lambda i: (i, 0)),
                  pl.BlockSpec((ka + kb, D), lambda i: (0, 0)),
                  pl.BlockSpec((tm, D), lambda i: (i, 0)),
                  _tab_spec(D, tps, nct), row(D), row(D),
                  _tab_spec(D, tps, nct), _tab_spec(D, tps, nct),
                  pl.BlockSpec((N_EXPERTS, D), lambda i: (0, 0))],
        out_specs=[pl.BlockSpec((tm, D), lambda i: (i, 0)),
                   pl.BlockSpec((None, N_EXPERTS, tm), lambda i: (i // tps, 0, i % tps))],
        out_shape=[jax.ShapeDtypeStruct((B * L, D), F32),
                   jax.ShapeDtypeStruct((B, N_EXPERTS, L), F32)],
        compiler_params=_cparams(("parallel",)),
    )(a.reshape(B * L, ka), b.reshape(B * L, kb), w, xx.reshape(B * L, D), gate,
      ln_g.reshape(1, D), ln_b.reshape(1, D), sh, sc, wr_t)
    return xo.reshape(B, L, D), aff


def _topk_kernel(aff_ref, idx_ref, gate_ref, *, T, cap_c, cap_l):
    ri = lax.broadcasted_iota(jnp.int32, (LANES, LANES), 0)
    ci = lax.broadcasted_iota(jnp.int32, (LANES, LANES), 1)
    tri = jnp.where(ri <= ci, 1.0, 0.0).astype(BF16)

    def prefix_incl(x):
        carry = jnp.zeros((x.shape[0], 1), F32)
        outs = []
        for j in range(x.shape[1] // LANES):
            pj = jnp.dot(x[:, j * LANES:(j + 1) * LANES].astype(BF16), tri, preferred_element_type=F32) + carry
            outs.append(pj)
            carry = pj[:, LANES - 1:LANES]
        return jnp.concatenate(outs, axis=1)

    def thresholds(segs):
        half = lambda a, b: a + lax.shift_right_logical(b - a, 1)

        def ge_cap(a, cap, bits):
            cnt = jnp.sum(jnp.where(a >= pltpu.bitcast(bits, F32), 1.0, 0.0), axis=1, keepdims=True)
            return cnt >= float(cap)

        def bisect2(_, state):
            out = []
            for (a, cap), (lo, hi) in zip(segs, state):
                mid = half(lo, hi)
                ml, mh = half(lo, mid), half(mid, hi)
                ge_m, ge_l, ge_h = ge_cap(a, cap, mid), ge_cap(a, cap, ml), ge_cap(a, cap, mh)
                out.append((jnp.where(ge_m, jnp.where(ge_h, mh, mid), jnp.where(ge_l, ml, lo)),
                            jnp.where(ge_m, jnp.where(ge_h, hi, mh), jnp.where(ge_l, mid, ml))))
            return tuple(out)

        ne = segs[0][0].shape[0]
        init = tuple((jnp.zeros((ne, 1), jnp.int32), jnp.full((ne, 1), 0x7F800000, jnp.int32)) for _ in segs)
        return [pltpu.bitcast(lo, F32) for lo, _ in lax.fori_loop(0, 16, bisect2, init)]

    def select(aff, thr, cap, base, row0):
        ne, n = aff.shape
        gt = aff > thr
        eqf = jnp.where(aff == thr, 1.0, 0.0)
        need = float(cap) - jnp.sum(jnp.where(gt, 1.0, 0.0), axis=1, keepdims=True)
        eq_before = prefix_incl(eqf) - eqf
        self_ = jnp.where(gt, 1.0, jnp.where(eq_before < need, eqf, 0.0))
        cnt = prefix_incl(self_)
        slot = lax.broadcasted_iota(jnp.int32, (cap, 1), 0).astype(F32)
        lane = lax.broadcasted_iota(jnp.int32, (cap, LANES), 1)
        idx_acc = jnp.zeros((cap, LANES), F32)
        gate_acc = jnp.zeros((cap, LANES), F32)
        for e in range(ne):
            ce, picked = cnt[e:e + 1, :], self_[e:e + 1, :] * aff[e:e + 1, :]
            idx_e = jnp.sum(jnp.where(ce <= slot, 1.0, 0.0), axis=1, keepdims=True)
            gate_e = jnp.sum(jnp.where(ce == slot + 1.0, picked, 0.0), axis=1, keepdims=True)
            idx_acc = jnp.where(lane == e, idx_e + float(base), idx_acc)
            gate_acc = jnp.where(lane == e, gate_e, gate_acc)
        idx_ref[row0:row0 + cap, :] = idx_acc.astype(jnp.int32)
        gate_ref[row0:row0 + cap, :] = gate_acc

    aff = aff_ref[...]
    aff_c, aff_l = aff[:, 0:T], aff[:, T:]
    thr_c, thr_l = thresholds([(aff_c, cap_c), (aff_l, cap_l)])
    select(aff_c, thr_c, cap_c, 0, 0)
    select(aff_l, thr_l, cap_l, T, cap_c)


def _topk(aff, T):
    B, E, L = aff.shape
    cap_c = EC_CAPACITY * T // E
    cap_l = EC_CAPACITY * (L - T) // E
    cap = cap_c + cap_l
    idx, gate = pl.pallas_call(
        functools.partial(_topk_kernel, T=T, cap_c=cap_c, cap_l=cap_l),
        grid=(B,),
        in_specs=[pl.BlockSpec((None, E, L), lambda b: (b, 0, 0))],
        out_specs=[pl.BlockSpec((None, cap, LANES), lambda b: (b, 0, 0)),
                   pl.BlockSpec((None, cap, LANES), lambda b: (b, 0, 0))],
        out_shape=[jax.ShapeDtypeStruct((B, cap, LANES), jnp.int32),
                   jax.ShapeDtypeStruct((B, cap, LANES), F32)],
        compiler_params=_cparams(("parallel",)),
    )(aff)
    idx = jnp.swapaxes(idx[:, :, :E], 1, 2).reshape(B * E * cap)
    gate = jnp.swapaxes(gate[:, :, :E], 1, 2).reshape(B * E * cap)
    return idx, gate, cap_c, cap


def _gather_kernel(idx_ref, x_ref, sh_ref, sc_ref, o_ref, stage_ref, *, cap_c, cap):
    base = (pl.program_id(0) * pl.num_programs(1) + pl.program_id(1)) * cap

    def body(s, carry):
        stage_ref[pl.ds(s, 1), :] = x_ref[pl.ds(idx_ref[base + s], 1), :]
        return carry

    lax.fori_loop(0, cap, body, 0, unroll=8)
    o_ref[0:cap_c, :] = (stage_ref[0:cap_c, :] * (1.0 + sc_ref[0]) + sh_ref[0]).astype(BF16)
    o_ref[cap_c:, :] = (stage_ref[cap_c:, :] * (1.0 + sc_ref[1]) + sh_ref[1]).astype(BF16)


def _gather(idx, xx, sh, sc, cap_c, cap):
    B, L, D = xx.shape
    E = N_EXPERTS
    tab = pl.BlockSpec((None, 2, 1, D), lambda b, e, idx: (b, 0, 0, 0))
    return pl.pallas_call(
        functools.partial(_gather_kernel, cap_c=cap_c, cap=cap),
        grid_spec=pltpu.PrefetchScalarGridSpec(
            num_scalar_prefetch=1, grid=(B, E),
            in_specs=[pl.BlockSpec((None, L, D), lambda b, e, idx: (b, 0, 0)), tab, tab],
            out_specs=pl.BlockSpec((None, None, cap, D), lambda b, e, idx: (e, b, 0, 0)),
            scratch_shapes=[pltpu.VMEM((cap, D), F32)]),
        out_shape=jax.ShapeDtypeStruct((E, B, cap, D), BF16),
        compiler_params=_cparams(("parallel", "arbitrary")),
    )(idx, xx, sh, sc)


def _expert_kernel(x_ref, wg_ref, wu_ref, wd_ref, o_ref):
    x = x_ref[...]
    g = jnp.dot(x, wg_ref[...].astype(BF16), preferred_element_type=F32)
    u = jnp.dot(x, wu_ref[...].astype(BF16), preferred_element_type=F32)
    hid = (_silu(g) * u).astype(BF16)
    o_ref[...] = jnp.dot(hid, wd_ref[...].astype(BF16), preferred_element_type=F32)


def _experts(xe, w_gate, w_up, w_down, layer, tm):
    E, M, D = xe.shape
    F = w_gate.shape[-1]
    return pl.pallas_call(
        _expert_kernel,
        grid=(E, M // tm),
        in_specs=[pl.BlockSpec((None, tm, D), lambda e, m: (e, m, 0)),
                  pl.BlockSpec((None, None, D, F), lambda e, m: (layer, e, 0, 0)),
                  pl.BlockSpec((None, None, D, F), lambda e, m: (layer, e, 0, 0)),
                  pl.BlockSpec((None, None, F, D), lambda e, m: (layer, e, 0, 0))],
        out_specs=pl.BlockSpec((None, tm, D), lambda e, m: (e, m, 0)),
        out_shape=jax.ShapeDtypeStruct((E, M, D), F32),
        compiler_params=_cparams(("parallel", "arbitrary")),
    )(xe, w_gate, w_up, w_down)


def _combine_kernel(idx_ref, gate_ref, ye_ref, x_ref, gt_ref, lng_ref, lnb_ref, o_ref, *, cap, T, tm):
    e = pl.program_id(1)
    n_e = pl.num_programs(1)
    base = (pl.program_id(0) * n_e + e) * cap

    @pl.when(e == 0)
    def _():
        o_ref[...] = jnp.zeros(o_ref.shape, F32)

    def body(i, carry):
        s0 = base + i * SCATTER_GROUP
        rs = [idx_ref[s0 + j] for j in range(SCATTER_GROUP)]
        new = [o_ref[pl.ds(rs[j], 1), :] + gate_ref[s0 + j] * ye_ref[pl.ds(i * SCATTER_GROUP + j, 1), :]
               for j in range(SCATTER_GROUP)]
        for j in range(SCATTER_GROUP):
            o_ref[pl.ds(rs[j], 1), :] = new[j]
        return carry

    lax.fori_loop(0, cap // SCATTER_GROUP, body, 0)

    @pl.when(e == n_e - 1)
    def _():
        def tile(t, carry):
            rows = pl.ds(pl.multiple_of(t * tm, tm), tm)
            gt = jnp.where(t < T // tm, gt_ref[0], gt_ref[1])
            o_ref[rows, :] = _ln(DEEPNORM_ALPHA * x_ref[rows, :] + gt * o_ref[rows, :], lng_ref[...], lnb_ref[...])
            return carry

        lax.fori_loop(0, o_ref.shape[0] // tm, tile, 0)


def _combine(idx, gate, ye, xx, gt, ln_g, ln_b, tm, T):
    E, B, cap, D = ye.shape
    L = xx.shape[1]
    row = pl.BlockSpec((1, D), lambda b, e, i, g: (0, 0))
    return pl.pallas_call(
        functools.partial(_combine_kernel, cap=cap, T=T, tm=tm),
        grid_spec=pltpu.PrefetchScalarGridSpec(
            num_scalar_prefetch=2, grid=(B, E),
            in_specs=[pl.BlockSpec((None, None, cap, D), lambda b, e, i, g: (e, b, 0, 0)),
                      pl.BlockSpec((None, L, D), lambda b, e, i, g: (b, 0, 0)),
                      pl.BlockSpec((None, 2, 1, D), lambda b, e, i, g: (b, 0, 0, 0)),
                      row, row],
            out_specs=pl.BlockSpec((None, L, D), lambda b, e, i, g: (b, 0, 0))),
        out_shape=jax.ShapeDtypeStruct((B, L, D), F32),
        compiler_params=_cparams(("parallel", "arbitrary")),
    )(idx, gate, ye, xx, gt, ln_g.reshape(1, D), ln_b.reshape(1, D))


def _na_plan(rows, kh):
    band = min(kh + NA_GROUP - 1, rows)
    plan, keys = [], []
    for g in range(rows // NA_GROUP):
        q0 = NA_GROUP * g
        win = [min(max(q0 + a - kh // 2, 0), rows - kh) for a in range(NA_GROUP)]
        ub = min(win[0], rows - band)
        key = (ub - q0, tuple(w - q0 for w in win))
        if key not in keys:
            keys.append(key)
        plan.append((ub, keys.index(key)))
    return band, plan, keys


def _na_kernel(plan_ref, q_ref, k_ref, v_ref, bias_ref, o_ref, *, T, nct, band, scale):
    i = pl.program_id(2)
    q = q_ref[...]
    lane = lax.broadcasted_iota(jnp.int32, q.shape, 1)
    kc = k_ref[0:T, :].astype(BF16)
    vc = v_ref[0:T, :].astype(BF16)
    qh = [jnp.where(lane < NA_DH, q, 0.0).astype(BF16), jnp.where(lane >= NA_DH, q, 0.0).astype(BF16)]

    @pl.when(i < nct)
    def _():
        outs = []
        for j in range(2):
            s_c = lax.dot_general(qh[j], kc, _NT, preferred_element_type=F32) * scale
            p_c = jnp.exp(s_c - jnp.max(s_c, axis=-1, keepdims=True))
            outs.append(jnp.dot(p_c.astype(BF16), vc, preferred_element_type=F32)
                        / jnp.sum(p_c, axis=-1, keepdims=True))
        o_ref[...] = jnp.where(lane < NA_DH, outs[0], outs[1]).astype(BF16)

    @pl.when(i >= nct)
    def _():
        t0 = pl.multiple_of(T + plan_ref[0, i - nct] * GRID_W, GRID_W)
        kb = k_ref[pl.ds(t0, band * GRID_W), :].astype(BF16)
        vb = v_ref[pl.ds(t0, band * GRID_W), :].astype(BF16)
        outs = []
        for j in range(2):
            s_w = lax.dot_general(qh[j], kb, _NT, preferred_element_type=F32) * scale + bias_ref[j]
            s_c = lax.dot_general(qh[j], kc, _NT, preferred_element_type=F32) * scale
            m = jnp.maximum(jnp.max(s_w, axis=-1, keepdims=True), jnp.max(s_c, axis=-1, keepdims=True))
            p_w = jnp.exp(s_w - m)
            p_c = jnp.exp(s_c - m)
            l = jnp.sum(p_w, axis=-1, keepdims=True) + jnp.sum(p_c, axis=-1, keepdims=True)
            outs.append((jnp.dot(p_w.astype(BF16), vb, preferred_element_type=F32)
                         + jnp.dot(p_c.astype(BF16), vc, preferred_element_type=F32)) / l)
        o_ref[...] = jnp.where(lane < NA_DH, outs[0], outs[1]).astype(BF16)


def _na_bias_tables(rpb, band, keys, kh):
    w = jnp.arange(GRID_W)
    c0 = jnp.clip(w - NA_KW // 2, 0, GRID_W - NA_KW)
    col_ok = (w[None, :] >= c0[:, None]) & (w[None, :] < c0[:, None] + NA_KW)
    col_off = jnp.clip(w[None, :] - w[:, None] + (NA_KW - 1), 0, 2 * NA_KW - 2)
    onehot = (col_off[:, :, None] == jnp.arange(2 * NA_KW - 1)).astype(F32)
    base = jnp.einsum('hrm,wcm->hrwc', rpb.astype(F32), onehot, precision=HIGHEST)
    base = jnp.where(col_ok[None, None], base, NEG_BIG)
    masked = jnp.full((NA_HEADS, GRID_W, GRID_W), NEG_BIG, F32)
    tabs = []
    for ub_off, win_offs in keys:
        per_row = []
        for a in range(NA_GROUP):
            blocks = []
            for u in range(band):
                rel = ub_off + u
                inside = win_offs[a] <= rel < win_offs[a] + kh
                blocks.append(base[:, rel - a + NA_KH - 1] if inside else masked)
            per_row.append(jnp.concatenate(blocks, axis=-1))
        tabs.append(jnp.concatenate(per_row, axis=1))
    return jnp.stack(tabs)


def _na(p, rpb, T):
    B, L, _ = p.shape
    rows = (L - T) // GRID_W
    kh = min(NA_KH, rows)
    tq = NA_GROUP * GRID_W
    assert T % tq == 0 and rows % NA_GROUP == 0
    nct = T // tq
    hp = NA_HEADS // 2
    band, plan, keys = _na_plan(rows, kh)
    bias = _na_bias_tables(rpb, band, keys, kh)
    plan = jnp.asarray(plan, jnp.int32).T
    return pl.pallas_call(
        functools.partial(_na_kernel, T=T, nct=nct, band=band, scale=NA_DH ** -0.5),
        grid_spec=pltpu.PrefetchScalarGridSpec(
            num_scalar_prefetch=1, grid=(B, hp, nct + rows // NA_GROUP),
            in_specs=[pl.BlockSpec((None, tq, LANES), lambda b, h, i, pr: (b, i, h)),
                      pl.BlockSpec((None, L, LANES), lambda b, h, i, pr: (b, 0, hp + h)),
                      pl.BlockSpec((None, L, LANES), lambda b, h, i, pr: (b, 0, 2 * hp + h)),
                      pl.BlockSpec((None, 2, tq, band * GRID_W),
                                   lambda b, h, i, pr: (pr[1, jnp.maximum(i - nct, 0)], h, 0, 0))],
            out_specs=pl.BlockSpec((None, tq, LANES), lambda b, h, i, pr: (b, i, h))),
        out_shape=jax.ShapeDtypeStruct((B, L, NA_HEADS * NA_DH), BF16),
        compiler_params=_cparams(("parallel", "parallel", "arbitrary")),
    )(plan, p, p, p, bias)


def _hgrn_kernel(q_ref, zf_ref, zb_ref, v_ref, gz_ref, llb_ref, l1m_ref, oml_ref, gain_ref,
                 o_ref,
                 qg_ref, ut_ref, oi_ref, el_ref, of_ref, ob_ref, *, T, L):
    n_chunks = L // CHUNK
    nc = T // CHUNK
    ri = lax.broadcasted_iota(jnp.int32, (CHUNK, CHUNK), 0)
    ci = lax.broadcasted_iota(jnp.int32, (CHUNK, CHUNK), 1)
    row = lax.broadcasted_iota(jnp.int32, (CHUNK, LANES), 0)
    eye = ri == ci
    top_bit = 31 - lax.clz(ri ^ ci)
    split_at = [jnp.where(ci < ri, top_bit, -1), jnp.where(ci > ri, top_bit, -1)]

    def shifted(x, k):
        return x if k == 0 else pltpu.roll(x, (-k) % CHUNK, 0)

    def scan_cumsum(x, d):
        k = 1
        while k < CHUNK:
            if d == 0:
                x = x + jnp.where(row >= k, shifted(x, -k), 0.0)
            else:
                x = x + jnp.where(row < CHUNK - k, shifted(x, k), 0.0)
            k *= 2
        return x

    def block_ref(g, size, r_in):
        if size >= 8:
            g3 = g.reshape(CHUNK // size, size, LANES)
            return jnp.broadcast_to(g3[:, r_in:r_in + 1, :], g3.shape).reshape(CHUNK, LANES)
        pos = row & (size - 1)
        out = shifted(g, r_in)
        for m in range(1, size):
            out = jnp.where(pos == m, shifted(g, r_in - m), out)
        return out

    def chunk_prep(n, carry):
        rows = pl.ds(pl.multiple_of(n * CHUNK, CHUNK), CHUNK)
        qs = _silu(q_ref[rows, :])
        v = v_ref[rows, :]
        vt = v.T
        for d in range(2):
            z = (zf_ref if d == 0 else zb_ref)[rows, :]
            t = jnp.exp(-jnp.abs(z))
            log_sig = jnp.minimum(z, 0.0) - jnp.log1p(t)
            logf = jnp.logaddexp(llb_ref[d:d + 1, :], l1m_ref[d:d + 1, :] + log_sig)
            kk = oml_ref[d:d + 1, :] * (jnp.where(z > 0, t, 1.0) / (1.0 + t))
            g = scan_cumsum(logf, d)
            g_tot = g[CHUNK - 1:CHUNK, :] if d == 0 else g[0:1, :]
            a = jnp.where(eye, jnp.sum(qs * kk, axis=1, keepdims=True), 0.0)
            half = CHUNK // 2
            while half >= 1:
                size = 2 * half
                e = jnp.exp(-jnp.abs(g - block_ref(g, size, half if d == 0 else half - 1)))
                a = jnp.where(split_at[d] == half.bit_length() - 1, _mm_nt(qs * e, kk * e), a)
                half //= 2
            oi_ref[d, rows, :] = _mm(a, v)
            qg_ref[d, rows, :] = (qs * jnp.exp(g)).astype(BF16)
            ut_ref[d * n_chunks + n] = _mm(vt, kk * jnp.exp(g_tot - g))
            el_ref[d * n_chunks + n] = jnp.broadcast_to(jnp.exp(g_tot), (8, LANES))
        return carry

    lax.fori_loop(0, n_chunks, chunk_prep, 0, unroll=4)

    def step(d, n, st):
        rows = pl.ds(pl.multiple_of(n * CHUNK, CHUNK), CHUNK)
        o = _mm_nt(qg_ref[d, rows, :], st) + oi_ref[d, rows, :]
        st = st * el_ref[d * n_chunks + n][0:1, :] + ut_ref[d * n_chunks + n]
        return st, o, rows

    def scan_body(i, carry):
        sf, sb = carry
        sf, o_f, rows_f = step(0, i, sf)
        sb, o_b, rows_b = step(1, _bwd_chunk(i, nc, n_chunks), sb)
        of_ref[rows_f, :] = o_f
        ob_ref[rows_b, :] = o_b
        return sf, sb

    zero = jnp.zeros((HG_DV, HG_DK), F32)
    lax.fori_loop(0, n_chunks, scan_body, (zero, zero), unroll=SCAN_UNROLL)
    o = of_ref[...] + ob_ref[...]
    o_ref[...] = (_rms(o) * gain_ref[...] * _silu(gz_ref[...])).astype(BF16)


def _hgrn(p, llb, l1m, oml, gain, T):
    B, L, _ = p.shape
    H = HG_HEADS
    n_chunks = L // CHUNK
    col = lambda off: pl.BlockSpec((None, L, LANES), lambda b, h: (b, 0, off + h))
    lbs = pl.BlockSpec((2, LANES), lambda b, h: (0, h))
    return pl.pallas_call(
        functools.partial(_hgrn_kernel, T=T, L=L),
        grid=(B, H),
        in_specs=[col(12), col(16), col(20), col(24), col(28), lbs, lbs, lbs,
                  pl.BlockSpec((1, LANES), lambda b, h: (0, 0))],
        out_specs=pl.BlockSpec((None, L, LANES), lambda b, h: (b, 0, h)),
        out_shape=jax.ShapeDtypeStruct((B, L, H * HG_DV), BF16),
        scratch_shapes=[pltpu.VMEM((2, L, LANES), BF16),
                        pltpu.VMEM((2 * n_chunks, HG_DV, HG_DK), F32),
                        pltpu.VMEM((2, L, LANES), F32),
                        pltpu.VMEM((2 * n_chunks, 8, LANES), F32),
                        pltpu.VMEM((L, LANES), F32), pltpu.VMEM((L, LANES), F32)],
        compiler_params=_cparams(("parallel", "parallel")),
    )(p, p, p, p, p, llb, l1m, oml, gain.reshape(1, LANES))


def _rot_cols(w):
    q = MLA_ROPE // 4
    return jnp.concatenate([-w[..., q:2 * q], w[..., 0:q], -w[..., 3 * q:4 * q], w[..., 2 * q:3 * q]], -1)


def _even_w_in(w):
    d = w.shape[0]
    qkv_z = GDN_HEADS * (2 * GDN_DK + GDN_DV) + GDN_HEADS * GDN_DV
    ab = w[:, qkv_z:qkv_z + 4 * GDN_HEADS]
    o = qkv_z + 4 * GDN_HEADS
    lora = w[:, o:o + MLA_Q_LORA + MLA_KV_LORA]
    kr = w[:, o + MLA_Q_LORA + MLA_KV_LORA:]
    pad = jnp.zeros((d, LANES - 4 * GDN_HEADS), w.dtype)
    return jnp.concatenate([w[:, :qkv_z], lora, kr, _rot_cols(kr), ab, pad], axis=1).astype(BF16)


def _mla_wq(w):
    w = w.reshape(w.shape[0], MLA_HEADS, MLA_NOPE + MLA_ROPE)
    rope = w[..., MLA_NOPE:]
    return jnp.concatenate([w[..., :MLA_NOPE], rope, _rot_cols(rope)], -1).reshape(w.shape[0], -1).astype(BF16)


def _rope_table(T, S):
    half = MLA_ROPE // 4
    inv = ROPE_THETA ** (-jnp.arange(half, dtype=F32) / half)
    pos = jnp.arange(S)
    ang_r = (pos // GRID_W).astype(F32)[:, None] * inv
    ang_c = (pos % GRID_W).astype(F32)[:, None] * inv
    ang = jnp.concatenate([ang_r, ang_r, ang_c, ang_c], -1)
    lat = jnp.concatenate([jnp.cos(ang), jnp.sin(ang)], -1)
    ctx = jnp.concatenate([jnp.ones((T, MLA_ROPE), F32), jnp.zeros((T, MLA_ROPE), F32)], -1)
    return jnp.concatenate([ctx, lat], 0)


def kernel(x, c, ctx, c_ctx, w_mod, b_mod, ln_g, ln_b, even_w_in, gdn_conv, gdn_a_log, gdn_dt_bias, gdn_norm,
           mla_q_norm, mla_kv_norm, mla_w_uq, mla_w_ukv, even_w_out, odd_w_in, na_rpb, hg_lb, hg_norm, odd_w_out,
           moe_router, moe_w_gate, moe_w_up, moe_w_down):
    B, S, D = x.shape
    T = ctx.shape[1]
    L = T + S
    tm = min(256, T)
    depth = w_mod.shape[0]

    xx = jnp.concatenate([ctx, x], axis=1)
    cc = jnp.concatenate([c, c_ctx[None, :], jnp.zeros((16 - B - 1, D), F32)], axis=0)
    mods = _modulation(cc, w_mod, b_mod)
    mods = mods.reshape(depth, 16, N_MOD, D)
    tabs = jnp.stack([jnp.broadcast_to(mods[:, B:B + 1], (depth, B, N_MOD, D)), mods[:, :B]], axis=2)
    tabs = jnp.transpose(tabs, (0, 3, 1, 2, 4))[:, :, :, :, None, :]

    lb_all = jnp.cumsum(jax.nn.softmax(hg_lb.astype(F32), axis=0), axis=0)
    lb_all = lb_all - lb_all[:1]
    cs = _rope_table(T, S)

    for layer in range(depth):
        j = layer // 2
        tab = tabs[layer]
        if layer % 2 == 0:
            p = _inproj(xx, tab[0], tab[1], _even_w_in(even_w_in[j]), tm, T)
            mix_a = _gdn(p, gdn_conv[j], (-jnp.exp(gdn_a_log[j].astype(F32))).reshape(-1),
                         gdn_dt_bias[j].astype(F32).reshape(-1), gdn_norm[j], T)
            q, k, v = _mlaproj(p, cs, mla_q_norm[j], mla_kv_norm[j], _mla_wq(mla_w_uq[j]),
                               mla_w_ukv[j].astype(BF16), tm)
            mix_b = _mla_attn(q, k, v, T, tm)
            w_out = even_w_out[j]
        else:
            p = _inproj(xx, tab[0], tab[1], odd_w_in[j].astype(BF16), tm, T)
            mix_a = _na(p, na_rpb[j], T)
            lb = lb_all[j]
            mix_b = _hgrn(p, jnp.log(lb), jnp.log1p(-lb), 1.0 - lb, hg_norm[j], T)
            w_out = odd_w_out[j]
        xx, aff = _outproj(mix_a, mix_b, w_out.astype(BF16), xx, tab[2], ln_g[layer, 0], ln_b[layer, 0],
                           tab[3], tab[4], moe_router[layer].T, tm, T)
        idx, gate, cap_c, cap = _topk(aff, T)
        xe = _gather(idx, xx, tab[3], tab[4], cap_c, cap)
        m_rows = B * cap
        ye = _experts(xe.reshape(N_EXPERTS, m_rows, D), moe_w_gate, moe_w_up, moe_w_down, layer,
                      next(t for t in (768, 576, m_rows) if m_rows % t == 0))
        xx = _combine(idx, gate, ye.reshape(N_EXPERTS, B, cap, D), xx, tab[5], ln_g[layer, 1], ln_b[layer, 1],
                      tm, T)
    return xx[:, T:, :]
```

```python
import functools
import math

import jax
import jax.numpy as jnp
from jax import lax
from jax.experimental import pallas as pl
from jax.experimental.pallas import tpu as pltpu

F32 = jnp.float32
BF16 = jnp.bfloat16
HIGHEST = lax.Precision.HIGHEST

DEPTH = 4
GRID_W = 64
N_MOD = 6
EPS = 1e-6
DEEPNORM_ALPHA = (2.0 * DEPTH) ** 0.25

GDN_HEADS = 4
GDN_DK = 128
GDN_DV = 128
GDN_CONV = 5
MLA_HEADS = 4
MLA_Q_LORA = 256
MLA_KV_LORA = 256
MLA_NOPE = 128
MLA_ROPE = 64
MLA_DV = 128
ROPE_THETA = 10000.0
NA_HEADS = 8
NA_DH = 64
NA_KH = 8
NA_KW = 16
HG_HEADS = 4
HG_DK = 128
HG_DV = 128
N_EXPERTS = 16
EC_CAPACITY = 2

CHUNK = 64
NA_GROUP = 4
PREP_CHUNKS = 9
SCAN_UNROLL = 4
SCATTER_GROUP = 8
LANES = 128
MLA_QK_PAD = 256
NEG_BIG = -1e30
VMEM_LIMIT = 56 * 1024 * 1024

_NT = (((1,), (1,)), ((), ()))


def _cparams(sem):
    return pltpu.CompilerParams(dimension_semantics=sem, vmem_limit_bytes=VMEM_LIMIT)


def _mm(a, b):
    return jnp.dot(a.astype(BF16), b.astype(BF16), preferred_element_type=F32)


def _mm_nt(a, b):
    return lax.dot_general(a.astype(BF16), b.astype(BF16), _NT, preferred_element_type=F32)


def _mm32(a, b):
    return jnp.dot(a, b, precision=HIGHEST, preferred_element_type=F32)


def _silu(x):
    return x * jax.nn.sigmoid(x)


def _rms(x):
    return x * lax.rsqrt(jnp.mean(x * x, -1, keepdims=True) + EPS)


def _ln(y, g, b):
    yc = y - jnp.mean(y, -1, keepdims=True)
    return yc * lax.rsqrt(jnp.mean(yc * yc, -1, keepdims=True) + EPS) * g + b


def _mod_kernel(c_ref, w_ref, b_ref, o_ref):
    o_ref[...] = _mm32(_silu(c_ref[...]), w_ref[...]) + b_ref[...]


def _modulation(cc, w_mod, b_mod):
    depth, d, n = w_mod.shape
    tn = n // 4
    return pl.pallas_call(
        _mod_kernel,
        grid=(depth, n // tn),
        in_specs=[pl.BlockSpec((cc.shape[0], d), lambda l, j: (0, 0)),
                  pl.BlockSpec((None, d, tn), lambda l, j: (l, 0, j)),
                  pl.BlockSpec((None, 1, tn), lambda l, j: (l, 0, j))],
        out_specs=pl.BlockSpec((None, cc.shape[0], tn), lambda l, j: (l, 0, j)),
        out_shape=jax.ShapeDtypeStruct((depth, cc.shape[0], n), F32),
        compiler_params=_cparams(("parallel", "parallel")),
    )(cc, w_mod, b_mod.reshape(depth, 1, n))


def _inproj_kernel(x_ref, sh_ref, sc_ref, w_ref, o_ref):
    u = x_ref[...] * (1.0 + sc_ref[...]) + sh_ref[...]
    o_ref[...] = jnp.dot(u.astype(BF16), w_ref[...], preferred_element_type=F32)


def _tab_spec(d, tps, nct):
    return pl.BlockSpec((None, None, 1, d), lambda i: (i // tps, jnp.where(i % tps >= nct, 1, 0), 0, 0))


def _inproj(xx, sh, sc, w, tm, T):
    B, L, D = xx.shape
    n = w.shape[1]
    tps, nct = L // tm, T // tm
    out = pl.pallas_call(
        _inproj_kernel,
        grid=(B * tps,),
        in_specs=[pl.BlockSpec((tm, D), lambda i: (i, 0)),
                  _tab_spec(D, tps, nct), _tab_spec(D, tps, nct),
                  pl.BlockSpec((D, n), lambda i: (0, 0))],
        out_specs=pl.BlockSpec((tm, n), lambda i: (i, 0)),
        out_shape=jax.ShapeDtypeStruct((B * L, n), F32),
        compiler_params=_cparams(("parallel",)),
    )(xx.reshape(B * L, D), sh, sc, w)
    return out.reshape(B, L, n)


def _scan_masks(d):
    ri = lax.broadcasted_iota(jnp.int32, (CHUNK, CHUNK), 0)
    ci = lax.broadcasted_iota(jnp.int32, (CHUNK, CHUNK), 1)
    if d == 0:
        return ci <= ri, ci < ri, ri <= ci, ci == ri
    return ci >= ri, ci > ri, ri >= ci, ci == ri


def _bwd_chunk(i, nc, n):
    return jnp.where(i < nc, nc - 1 - i, n - 1 - (i - nc))


def _gdn_kernel(nega_ref, dtb_ref, q_ref, k_ref, v_ref, z_ref, ab_ref, cwq_ref, cwk_ref, cwv_ref, gain_ref,
                o_ref,
                pad_ref, qn_ref, kn_ref, vn_ref, ncs_ref, kw_ref, ou_ref, qp_ref, el_ref, of_ref, ob_ref,
                *, T, L):
    h = pl.program_id(1)
    n_chunks = L // CHUNK
    nc = T // CHUNK
    prep = max(c for c in range(1, PREP_CHUNKS + 1) if n_chunks % c == 0)

    def conv_prep(src_ref, cw_ref, dst_ref, post):
        zero8 = jnp.zeros((8, LANES), F32)
        pad_ref[0:8, :] = zero8
        pad_ref[8:8 + T, :] = src_ref[0:T, :]
        pad_ref[8 + T:16 + T, :] = zero8
        pad_ref[16 + T:16 + L, :] = src_ref[T:L, :]
        pad_ref[16 + L:24 + L, :] = zero8
        cw = cw_ref[...]

        def body(n, carry):
            r0 = pl.multiple_of(n * CHUNK, CHUNK)
            p0 = r0 + jnp.where(r0 < T, 8, 16) - GDN_CONV // 2
            acc = jnp.zeros((CHUNK, LANES), F32)
            for j in range(GDN_CONV):
                acc = acc + pad_ref[pl.ds(p0 + j, CHUNK), :] * cw[j:j + 1, :]
            dst_ref[pl.ds(r0, CHUNK), :] = post(_silu(acc))
            return carry

        lax.fori_loop(0, n_chunks, body, 0, unroll=prep)

    def l2n(y):
        return y * lax.rsqrt(jnp.sum(y * y, -1, keepdims=True) + EPS)

    conv_prep(q_ref, cwq_ref, qn_ref, lambda y: l2n(y) * (GDN_DK ** -0.5))
    conv_prep(k_ref, cwk_ref, kn_ref, l2n)
    conv_prep(v_ref, cwv_ref, vn_ref, lambda y: y)

    lane = lax.broadcasted_iota(jnp.int32, (CHUNK, LANES), 1)

    def chunk_prep(i, carry):
        chains = []
        for c in range(prep):
            n = prep * i + c
            rows = pl.ds(pl.multiple_of(n * CHUNK, CHUNK), CHUNK)
            q, k, v, ab = qn_ref[rows, :], kn_ref[rows, :], vn_ref[rows, :], ab_ref[rows, :]
            kq = _mm_nt(jnp.concatenate([k, q], axis=0), k)
            for d in range(2):
                chains.append(dict(n=n, d=d, rows=rows, q=q, k=k, v=v, ab=ab, kk=kq[:CHUNK], qk=kq[CHUNK:]))
        for ch in chains:
            d = ch["d"]
            incl, strict, incl_t, eye = _scan_masks(d)
            a_col = jnp.sum(jnp.where(lane == 4 * d + h, ch["ab"], 0.0), axis=1, keepdims=True)
            b_col = jnp.sum(jnp.where(lane == 8 + 4 * d + h, ch["ab"], 0.0), axis=1, keepdims=True)
            g_col = nega_ref[4 * d + h] * jax.nn.softplus(a_col + dtb_ref[4 * d + h])
            beta = jax.nn.sigmoid(b_col)
            g_row = jnp.sum(jnp.where(eye, g_col, 0.0), axis=0, keepdims=True)
            gc_col = jnp.sum(jnp.where(incl, g_row, 0.0), axis=1, keepdims=True)
            gc_row = jnp.sum(jnp.where(incl_t, g_col, 0.0), axis=0, keepdims=True)
            g_tot = jnp.sum(g_col, axis=0, keepdims=True)
            ch["decay"] = jnp.where(incl, jnp.exp(jnp.where(incl, gc_col - gc_row, 0.0)), 0.0)
            ch["kb"] = ch["k"] * beta
            ch["vb"] = ch["v"] * beta
            ch["beta"] = beta
            ch["e_col"] = jnp.exp(gc_col)
            ch["kd"] = ch["k"] * jnp.exp(g_tot - gc_col)
            ch["e_tot"] = jnp.exp(g_tot)
        for ch in chains:
            _, strict, _, eye = _scan_masks(ch["d"])
            a = jnp.where(strict, ch["beta"] * ch["kk"] * ch["decay"], 0.0)
            ch["p"] = jnp.where(eye, 1.0, 0.0) - a
            ch["bp"] = a
        for _ in range(int(math.log2(CHUNK)) - 1):
            for ch in chains:
                ch["bp"] = _mm(ch["bp"], ch["bp"])
            for ch in chains:
                ch["p"] = ch["p"] + _mm(ch["p"], ch["bp"])
        for ch in chains:
            incl = _scan_masks(ch["d"])[0]
            ch["uw"] = _mm(ch["p"], jnp.concatenate([ch["vb"], ch["kb"] * ch["e_col"]], axis=1)).astype(BF16)
            ch["qk"] = jnp.where(incl, ch["qk"] * ch["decay"], 0.0)
        for ch in chains:
            d, n, rows = ch["d"], ch["n"], ch["rows"]
            kd_uw = _mm(ch["kd"].T, ch["uw"])
            qk_uw = _mm(ch["qk"], ch["uw"])
            ncs_ref[d * n_chunks + n] = kd_uw[:, :GDN_DV]
            kw_ref[d * n_chunks + n] = kd_uw[:, GDN_DV:].astype(BF16)
            ou_ref[d, rows, :] = qk_uw[:, :GDN_DV]
            qp_ref[d, rows, :] = (ch["q"] * ch["e_col"] - qk_uw[:, GDN_DV:]).astype(BF16)
            el_ref[d * n_chunks + n] = jnp.broadcast_to(ch["e_tot"], (8, LANES))
        return carry

    lax.fori_loop(0, n_chunks // prep, chunk_prep, 0)

    def step(d, n, s):
        rows = pl.ds(pl.multiple_of(n * CHUNK, CHUNK), CHUNK)
        sb = s.astype(BF16)
        o = jnp.dot(qp_ref[d, rows, :], sb, preferred_element_type=F32) + ou_ref[d, rows, :]
        s = (s * el_ref[d * n_chunks + n][0:1, :] + ncs_ref[d * n_chunks + n]
             - jnp.dot(kw_ref[d * n_chunks + n], sb, preferred_element_type=F32))
        return s, o, rows

    def scan_body(i, carry):
        sf, sb = carry
        sf, o_f, rows_f = step(0, i, sf)
        sb, o_b, rows_b = step(1, _bwd_chunk(i, nc, n_chunks), sb)
        of_ref[rows_f, :] = o_f
        ob_ref[rows_b, :] = o_b
        return sf, sb

    zero = jnp.zeros((GDN_DK, GDN_DV), F32)
    lax.fori_loop(0, n_chunks, scan_body, (zero, zero), unroll=SCAN_UNROLL)

    o = of_ref[...] + ob_ref[...]
    o_ref[...] = (_rms(o) * gain_ref[...] * _silu(z_ref[...])).astype(BF16)


def _gdn(p, conv_w, neg_a, dt_bias, gain, T):
    B, L, _ = p.shape
    H = GDN_HEADS
    n_chunks = L // CHUNK
    col = lambda off: pl.BlockSpec((None, L, LANES), lambda b, h: (b, 0, off + h))
    cw = lambda off: pl.BlockSpec((GDN_CONV, LANES), lambda b, h: (0, off + h))
    smem = pl.BlockSpec(memory_space=pltpu.SMEM)
    return pl.pallas_call(
        functools.partial(_gdn_kernel, T=T, L=L),
        grid=(B, H),
        in_specs=[smem, smem, col(0), col(H), col(2 * H), col(3 * H),
                  pl.BlockSpec((None, L, LANES), lambda b, h: (b, 0, 21)),
                  cw(0), cw(H), cw(2 * H),
                  pl.BlockSpec((1, LANES), lambda b, h: (0, 0))],
        out_specs=pl.BlockSpec((None, L, LANES), lambda b, h: (b, 0, h)),
        out_shape=jax.ShapeDtypeStruct((B, L, H * GDN_DV), BF16),
        scratch_shapes=[pltpu.VMEM((L + 24, LANES), F32),
                        pltpu.VMEM((L, LANES), F32), pltpu.VMEM((L, LANES), F32), pltpu.VMEM((L, LANES), F32),
                        pltpu.VMEM((2 * n_chunks, GDN_DK, GDN_DV), F32),
                        pltpu.VMEM((2 * n_chunks, GDN_DK, GDN_DK), BF16),
                        pltpu.VMEM((2, L, LANES), F32), pltpu.VMEM((2, L, LANES), BF16),
                        pltpu.VMEM((2 * n_chunks, 8, LANES), F32),
                        pltpu.VMEM((L, LANES), F32), pltpu.VMEM((L, LANES), F32)],
        compiler_params=_cparams(("parallel", "parallel")),
    )(neg_a, dt_bias, p, p, p, p, p, conv_w, conv_w, conv_w, gain.reshape(1, LANES))


def _mlaproj_kernel(ql_ref, kvl_ref, kr_ref, cs_ref, qg_ref, kvg_ref, wq_ref, wkv_ref, q_ref, k_ref, v_ref):
    q = jnp.dot((_rms(ql_ref[...]) * qg_ref[...]).astype(BF16), wq_ref[...], preferred_element_type=F32)
    kv = jnp.dot((_rms(kvl_ref[...]) * kvg_ref[...]).astype(BF16), wkv_ref[...], preferred_element_type=F32)
    cs = cs_ref[...]
    lane = lax.broadcasted_iota(jnp.int32, cs.shape, 1)

    def rope(blk):
        t = blk * cs
        return jnp.where(lane < MLA_ROPE, t + pltpu.roll(t, MLA_ROPE, 1), 0.0)

    kr = rope(kr_ref[...]).astype(BF16)
    for h in range(MLA_HEADS):
        o = h * MLA_QK_PAD
        q_ref[:, o:o + MLA_NOPE] = q[:, o:o + MLA_NOPE].astype(BF16)
        q_ref[:, o + MLA_NOPE:o + MLA_QK_PAD] = rope(q[:, o + MLA_NOPE:o + MLA_QK_PAD]).astype(BF16)
        k_ref[:, o:o + MLA_NOPE] = kv[:, o:o + MLA_NOPE].astype(BF16)
        k_ref[:, o + MLA_NOPE:o + MLA_QK_PAD] = kr
        v_ref[:, h * MLA_DV:(h + 1) * MLA_DV] = kv[:, o + MLA_NOPE:o + MLA_QK_PAD].astype(BF16)


def _mlaproj(p, cs, q_gain, kv_gain, wq, wkv, tm):
    B, L, n = p.shape
    tps = L // tm
    p2 = p.reshape(B * L, n)
    hq = MLA_HEADS * MLA_QK_PAD
    q, k, v = pl.pallas_call(
        _mlaproj_kernel,
        grid=(B * tps,),
        in_specs=[pl.BlockSpec((tm, MLA_Q_LORA), lambda i: (i, 8)),
                  pl.BlockSpec((tm, MLA_KV_LORA), lambda i: (i, 9)),
                  pl.BlockSpec((tm, LANES), lambda i: (i, 20)),
                  pl.BlockSpec((tm, LANES), lambda i: (i % tps, 0)),
                  pl.BlockSpec((1, MLA_Q_LORA), lambda i: (0, 0)),
                  pl.BlockSpec((1, MLA_KV_LORA), lambda i: (0, 0)),
                  pl.BlockSpec((MLA_Q_LORA, hq), lambda i: (0, 0)),
                  pl.BlockSpec((MLA_KV_LORA, hq), lambda i: (0, 0))],
        out_specs=[pl.BlockSpec((tm, hq), lambda i: (i, 0)),
                   pl.BlockSpec((tm, hq), lambda i: (i, 0)),
                   pl.BlockSpec((tm, MLA_HEADS * MLA_DV), lambda i: (i, 0))],
        out_shape=[jax.ShapeDtypeStruct((B * L, hq), BF16),
                   jax.ShapeDtypeStruct((B * L, hq), BF16),
                   jax.ShapeDtypeStruct((B * L, MLA_HEADS * MLA_DV), BF16)],
        compiler_params=_cparams(("parallel",)),
    )(p2, p2, p2, cs, q_gain.reshape(1, -1), kv_gain.reshape(1, -1), wq, wkv)
    return q.reshape(B, L, hq), k.reshape(B, L, hq), v.reshape(B, L, -1)


def _mla_attn_kernel(q_ref, k_ref, v_ref, o_ref, *, T, nct, scale):
    qt = pl.program_id(2)
    q = q_ref[...]

    def attend(n):
        s = lax.dot_general(q, k_ref[0:n, :], _NT, preferred_element_type=F32) * scale
        p = jnp.exp(s - jnp.max(s, axis=-1, keepdims=True))
        l = jnp.sum(p, axis=-1, keepdims=True)
        o_ref[...] = (jnp.dot(p.astype(BF16), v_ref[0:n, :], preferred_element_type=F32) / l).astype(BF16)

    @pl.when(qt < nct)
    def _():
        attend(T)

    @pl.when(qt >= nct)
    def _():
        attend(k_ref.shape[0])


def _mla_attn(q, k, v, T, tq):
    B, L, _ = q.shape
    H = MLA_HEADS
    scale = (MLA_NOPE + MLA_ROPE) ** -0.5
    return pl.pallas_call(
        functools.partial(_mla_attn_kernel, T=T, nct=T // tq, scale=scale),
        grid=(B, H, L // tq),
        in_specs=[pl.BlockSpec((None, tq, MLA_QK_PAD), lambda b, h, i: (b, i, h)),
                  pl.BlockSpec((None, L, MLA_QK_PAD), lambda b, h, i: (b, 0, h)),
                  pl.BlockSpec((None, L, MLA_DV), lambda b, h, i: (b, 0, h))],
        out_specs=pl.BlockSpec((None, tq, MLA_DV), lambda b, h, i: (b, i, h)),
        out_shape=jax.ShapeDtypeStruct((B, L, H * MLA_DV), BF16),
        compiler_params=_cparams(("parallel", "parallel", "arbitrary")),
    )(q, k, v)


def _outproj_kernel(a_ref, b_ref, w_ref, x_ref, gate_ref, lng_ref, lnb_ref, sh_ref, sc_ref, wr_ref,
                    xo_ref, aff_ref):
    ka = a_ref.shape[1]
    o = (jnp.dot(a_ref[...], w_ref[0:ka, :], preferred_element_type=F32)
         + jnp.dot(b_ref[...], w_ref[ka:, :], preferred_element_type=F32))
    xn = _ln(DEEPNORM_ALPHA * x_ref[...] + gate_ref[...] * o, lng_ref[...], lnb_ref[...])
    xo_ref[...] = xn
    hmod = xn * (1.0 + sc_ref[...]) + sh_ref[...]
    logits = _mm_nt(wr_ref[...], hmod)
    e = jnp.exp(logits - jnp.max(logits, axis=0, keepdims=True))
    aff_ref[...] = e / jnp.sum(e, axis=0, keepdims=True)


def _outproj(a, b, w, xx, gate, ln_g, ln_b, sh, sc, wr_t, tm, T):
    B, L, D = xx.shape
    tps, nct = L // tm, T // tm
    ka, kb = a.shape[-1], b.shape[-1]
    row = lambda n: pl.BlockSpec((1, n), lambda i: (0, 0))
    xo, aff = pl.pallas_call(
        _outproj_kernel,
        grid=(B * tps,),
        in_specs=[pl.BlockSpec((tm, ka), lambda i: (i, 0)),
                  pl.BlockSpec((tm, kb), lambda i: (i, 0)),
                  pl.BlockSpec((ka + kb, D), lambda i: (0, 0)),
                  pl.BlockSpec((tm, D), lambda i: (i, 0)),
                  _tab_spec(D, tps, nct), row(D), row(D),
                  _tab_spec(D, tps, nct), _tab_spec(D, tps, nct),
                  pl.BlockSpec((N_EXPERTS, D), lambda i: (0, 0))],
        out_specs=[pl.BlockSpec((tm, D), lambda i: (i, 0)),
                   pl.BlockSpec((None, N_EXPERTS, tm), lambda i: (i // tps, 0, i % tps))],
        out_shape=[jax.ShapeDtypeStruct((B * L, D), F32),
                   jax.ShapeDtypeStruct((B, N_EXPERTS, L), F32)],
        compiler_params=_cparams(("parallel",)),
    )(a.reshape(B * L, ka), b.reshape(B * L, kb), w, xx.reshape(B * L, D), gate,
      ln_g.reshape(1, D), ln_b.reshape(1, D), sh, sc, wr_t)
    return xo.reshape(B, L, D), aff


def _topk_kernel(aff_ref, idx_ref, gate_ref, *, T, cap_c, cap_l):
    ri = lax.broadcasted_iota(jnp.int32, (LANES, LANES), 0)
    ci = lax.broadcasted_iota(jnp.int32, (LANES, LANES), 1)
    tri = jnp.where(ri <= ci, 1.0, 0.0).astype(BF16)

    def prefix_incl(x):
        carry = jnp.zeros((x.shape[0], 1), F32)
        outs = []
        for j in range(x.shape[1] // LANES):
            pj = jnp.dot(x[:, j * LANES:(j + 1) * LANES].astype(BF16), tri, preferred_element_type=F32) + carry
            outs.append(pj)
            carry = pj[:, LANES - 1:LANES]
        return jnp.concatenate(outs, axis=1)

    def thresholds(segs):
        half = lambda a, b: a + lax.shift_right_logical(b - a, 1)

        def ge_cap(a, cap, bits):
            cnt = jnp.sum(jnp.where(a >= pltpu.bitcast(bits, F32), 1.0, 0.0), axis=1, keepdims=True)
            return cnt >= float(cap)

        def bisect2(_, state):
            out = []
            for (a, cap), (lo, hi) in zip(segs, state):
                mid = half(lo, hi)
                ml, mh = half(lo, mid), half(mid, hi)
                ge_m, ge_l, ge_h = ge_cap(a, cap, mid), ge_cap(a, cap, ml), ge_cap(a, cap, mh)
                out.append((jnp.where(ge_m, jnp.where(ge_h, mh, mid), jnp.where(ge_l, ml, lo)),
                            jnp.where(ge_m, jnp.where(ge_h, hi, mh), jnp.where(ge_l, mid, ml))))
            return tuple(out)

        ne = segs[0][0].shape[0]
        init = tuple((jnp.zeros((ne, 1), jnp.int32), jnp.full((ne, 1), 0x7F800000, jnp.int32)) for _ in segs)
        return [pltpu.bitcast(lo, F32) for lo, _ in lax.fori_loop(0, 16, bisect2, init)]

    def select(aff, thr, cap, base, row0):
        ne, n = aff.shape
        gt = aff > thr
        eqf = jnp.where(aff == thr, 1.0, 0.0)
        need = float(cap) - jnp.sum(jnp.where(gt, 1.0, 0.0), axis=1, keepdims=True)
        eq_before = prefix_incl(eqf) - eqf
        self_ = jnp.where(gt, 1.0, jnp.where(eq_before < need, eqf, 0.0))
        cnt = prefix_incl(self_)
        slot = lax.broadcasted_iota(jnp.int32, (cap, 1), 0).astype(F32)
        lane = lax.broadcasted_iota(jnp.int32, (cap, LANES), 1)
        idx_acc = jnp.zeros((cap, LANES), F32)
        gate_acc = jnp.zeros((cap, LANES), F32)
        for e in range(ne):
            ce, picked = cnt[e:e + 1, :], self_[e:e + 1, :] * aff[e:e + 1, :]
            idx_e = jnp.sum(jnp.where(ce <= slot, 1.0, 0.0), axis=1, keepdims=True)
            gate_e = jnp.sum(jnp.where(ce == slot + 1.0, picked, 0.0), axis=1, keepdims=True)
            idx_acc = jnp.where(lane == e, idx_e + float(base), idx_acc)
            gate_acc = jnp.where(lane == e, gate_e, gate_acc)
        idx_ref[row0:row0 + cap, :] = idx_acc.astype(jnp.int32)
        gate_ref[row0:row0 + cap, :] = gate_acc

    aff = aff_ref[...]
    aff_c, aff_l = aff[:, 0:T], aff[:, T:]
    thr_c, thr_l = thresholds([(aff_c, cap_c), (aff_l, cap_l)])
    select(aff_c, thr_c, cap_c, 0, 0)
    select(aff_l, thr_l, cap_l, T, cap_c)


def _topk(aff, T):
    B, E, L = aff.shape
    cap_c = EC_CAPACITY * T // E
    cap_l = EC_CAPACITY * (L - T) // E
    cap = cap_c + cap_l
    idx, gate = pl.pallas_call(
        functools.partial(_topk_kernel, T=T, cap_c=cap_c, cap_l=cap_l),
        grid=(B,),
        in_specs=[pl.BlockSpec((None, E, L), lambda b: (b, 0, 0))],
        out_specs=[pl.BlockSpec((None, cap, LANES), lambda b: (b, 0, 0)),
                   pl.BlockSpec((None, cap, LANES), lambda b: (b, 0, 0))],
        out_shape=[jax.ShapeDtypeStruct((B, cap, LANES), jnp.int32),
                   jax.ShapeDtypeStruct((B, cap, LANES), F32)],
        compiler_params=_cparams(("parallel",)),
    )(aff)
    idx = jnp.swapaxes(idx[:, :, :E], 1, 2).reshape(B * E * cap)
    gate = jnp.swapaxes(gate[:, :, :E], 1, 2).reshape(B * E * cap)
    return idx, gate, cap_c, cap


def _gather_kernel(idx_ref, x_ref, sh_ref, sc_ref, o_ref, stage_ref, *, cap_c, cap):
    base = (pl.program_id(0) * pl.num_programs(1) + pl.program_id(1)) * cap

    def body(s, carry):
        stage_ref[pl.ds(s, 1), :] = x_ref[pl.ds(idx_ref[base + s], 1), :]
        return carry

    lax.fori_loop(0, cap, body, 0, unroll=8)
    o_ref[0:cap_c, :] = (stage_ref[0:cap_c, :] * (1.0 + sc_ref[0]) + sh_ref[0]).astype(BF16)
    o_ref[cap_c:, :] = (stage_ref[cap_c:, :] * (1.0 + sc_ref[1]) + sh_ref[1]).astype(BF16)


def _gather(idx, xx, sh, sc, cap_c, cap):
    B, L, D = xx.shape
    E = N_EXPERTS
    tab = pl.BlockSpec((None, 2, 1, D), lambda b, e, idx: (b, 0, 0, 0))
    return pl.pallas_call(
        functools.partial(_gather_kernel, cap_c=cap_c, cap=cap),
        grid_spec=pltpu.PrefetchScalarGridSpec(
            num_scalar_prefetch=1, grid=(B, E),
            in_specs=[pl.BlockSpec((None, L, D), lambda b, e, idx: (b, 0, 0)), tab, tab],
            out_specs=pl.BlockSpec((None, None, cap, D), lambda b, e, idx: (e, b, 0, 0)),
            scratch_shapes=[pltpu.VMEM((cap, D), F32)]),
        out_shape=jax.ShapeDtypeStruct((E, B, cap, D), BF16),
        compiler_params=_cparams(("parallel", "arbitrary")),
    )(idx, xx, sh, sc)


def _expert_kernel(x_ref, wg_ref, wu_ref, wd_ref, o_ref):
    x = x_ref[...]
    g = jnp.dot(x, wg_ref[...].astype(BF16), preferred_element_type=F32)
    u = jnp.dot(x, wu_ref[...].astype(BF16), preferred_element_type=F32)
    hid = (_silu(g) * u).astype(BF16)
    o_ref[...] = jnp.dot(hid, wd_ref[...].astype(BF16), preferred_element_type=F32)


def _experts(xe, w_gate, w_up, w_down, layer, tm):
    E, M, D = xe.shape
    F = w_gate.shape[-1]
    return pl.pallas_call(
        _expert_kernel,
        grid=(E, M // tm),
        in_specs=[pl.BlockSpec((None, tm, D), lambda e, m: (e, m, 0)),
                  pl.BlockSpec((None, None, D, F), lambda e, m: (layer, e, 0, 0)),
                  pl.BlockSpec((None, None, D, F), lambda e, m: (layer, e, 0, 0)),
                  pl.BlockSpec((None, None, F, D), lambda e, m: (layer, e, 0, 0))],
        out_specs=pl.BlockSpec((None, tm, D), lambda e, m: (e, m, 0)),
        out_shape=jax.ShapeDtypeStruct((E, M, D), F32),
        compiler_params=_cparams(("parallel", "arbitrary")),
    )(xe, w_gate, w_up, w_down)


def _combine_kernel(idx_ref, gate_ref, ye_ref, x_ref, gt_ref, lng_ref, lnb_ref, o_ref, *, cap, T, tm):
    e = pl.program_id(1)
    n_e = pl.num_programs(1)
    base = (pl.program_id(0) * n_e + e) * cap

    @pl.when(e == 0)
    def _():
        o_ref[...] = jnp.zeros(o_ref.shape, F32)

    def body(i, carry):
        s0 = base + i * SCATTER_GROUP
        rs = [idx_ref[s0 + j] for j in range(SCATTER_GROUP)]
        new = [o_ref[pl.ds(rs[j], 1), :] + gate_ref[s0 + j] * ye_ref[pl.ds(i * SCATTER_GROUP + j, 1), :]
               for j in range(SCATTER_GROUP)]
        for j in range(SCATTER_GROUP):
            o_ref[pl.ds(rs[j], 1), :] = new[j]
        return carry

    lax.fori_loop(0, cap // SCATTER_GROUP, body, 0)

    @pl.when(e == n_e - 1)
    def _():
        def tile(t, carry):
            rows = pl.ds(pl.multiple_of(t * tm, tm), tm)
            gt = jnp.where(t < T // tm, gt_ref[0], gt_ref[1])
            o_ref[rows, :] = _ln(DEEPNORM_ALPHA * x_ref[rows, :] + gt * o_ref[rows, :], lng_ref[...], lnb_ref[...])
            return carry

        lax.fori_loop(0, o_ref.shape[0] // tm, tile, 0)


def _combine(idx, gate, ye, xx, gt, ln_g, ln_b, tm, T):
    E, B, cap, D = ye.shape
    L = xx.shape[1]
    row = pl.BlockSpec((1, D), lambda b, e, i, g: (0, 0))
    return pl.pallas_call(
        functools.partial(_combine_kernel, cap=cap, T=T, tm=tm),
        grid_spec=pltpu.PrefetchScalarGridSpec(
            num_scalar_prefetch=2, grid=(B, E),
            in_specs=[pl.BlockSpec((None, None, cap, D), lambda b, e, i, g: (e, b, 0, 0)),
                      pl.BlockSpec((None, L, D), lambda b, e, i, g: (b, 0, 0)),
                      pl.BlockSpec((None, 2, 1, D), lambda b, e, i, g: (b, 0, 0, 0)),
                      row, row],
            out_specs=pl.BlockSpec((None, L, D), lambda b, e, i, g: (b, 0, 0))),
        out_shape=jax.ShapeDtypeStruct((B, L, D), F32),
        compiler_params=_cparams(("parallel", "arbitrary")),
    )(idx, gate, ye, xx, gt, ln_g.reshape(1, D), ln_b.reshape(1, D))


def _na_plan(rows, kh):
    band = min(kh + NA_GROUP - 1, rows)
    plan, keys = [], []
    for g in range(rows // NA_GROUP):
        q0 = NA_GROUP * g
        win = [min(max(q0 + a - kh // 2, 0), rows - kh) for a in range(NA_GROUP)]
        ub = min(win[0], rows - band)
        key = (ub - q0, tuple(w - q0 for w in win))
        if key not in keys:
            keys.append(key)
        plan.append((ub, keys.index(key)))
    return band, plan, keys


def _na_kernel(plan_ref, q_ref, k_ref, v_ref, bias_ref, o_ref, *, T, nct, band, scale):
    i = pl.program_id(2)
    q = q_ref[...]
    lane = lax.broadcasted_iota(jnp.int32, q.shape, 1)
    kc = k_ref[0:T, :].astype(BF16)
    vc = v_ref[0:T, :].astype(BF16)
    qh = [jnp.where(lane < NA_DH, q, 0.0).astype(BF16), jnp.where(lane >= NA_DH, q, 0.0).astype(BF16)]

    @pl.when(i < nct)
    def _():
        outs = []
        for j in range(2):
            s_c = lax.dot_general(qh[j], kc, _NT, preferred_element_type=F32) * scale
            p_c = jnp.exp(s_c - jnp.max(s_c, axis=-1, keepdims=True))
            outs.append(jnp.dot(p_c.astype(BF16), vc, preferred_element_type=F32)
                        / jnp.sum(p_c, axis=-1, keepdims=True))
        o_ref[...] = jnp.where(lane < NA_DH, outs[0], outs[1]).astype(BF16)

    @pl.when(i >= nct)
    def _():
        t0 = pl.multiple_of(T + plan_ref[0, i - nct] * GRID_W, GRID_W)
        kb = k_ref[pl.ds(t0, band * GRID_W), :].astype(BF16)
        vb = v_ref[pl.ds(t0, band * GRID_W), :].astype(BF16)
        outs = []
        for j in range(2):
            s_w = lax.dot_general(qh[j], kb, _NT, preferred_element_type=F32) * scale + bias_ref[j]
            s_c = lax.dot_general(qh[j], kc, _NT, preferred_element_type=F32) * scale
            m = jnp.maximum(jnp.max(s_w, axis=-1, keepdims=True), jnp.max(s_c, axis=-1, keepdims=True))
            p_w = jnp.exp(s_w - m)
            p_c = jnp.exp(s_c - m)
            l = jnp.sum(p_w, axis=-1, keepdims=True) + jnp.sum(p_c, axis=-1, keepdims=True)
            outs.append((jnp.dot(p_w.astype(BF16), vb, preferred_element_type=F32)
                         + jnp.dot(p_c.astype(BF16), vc, preferred_element_type=F32)) / l)
        o_ref[...] = jnp.where(lane < NA_DH, outs[0], outs[1]).astype(BF16)


def _na_bias_tables(rpb, band, keys, kh):
    w = jnp.arange(GRID_W)
    c0 = jnp.clip(w - NA_KW // 2, 0, GRID_W - NA_KW)
    col_ok = (w[None, :] >= c0[:, None]) & (w[None, :] < c0[:, None] + NA_KW)
    col_off = jnp.clip(w[None, :] - w[:, None] + (NA_KW - 1), 0, 2 * NA_KW - 2)
    onehot = (col_off[:, :, None] == jnp.arange(2 * NA_KW - 1)).astype(F32)
    base = jnp.einsum('hrm,wcm->hrwc', rpb.astype(F32), onehot, precision=HIGHEST)
    base = jnp.where(col_ok[None, None], base, NEG_BIG)
    masked = jnp.full((NA_HEADS, GRID_W, GRID_W), NEG_BIG, F32)
    tabs = []
    for ub_off, win_offs in keys:
        per_row = []
        for a in range(NA_GROUP):
            blocks = []
            for u in range(band):
                rel = ub_off + u
                inside = win_offs[a] <= rel < win_offs[a] + kh
                blocks.append(base[:, rel - a + NA_KH - 1] if inside else masked)
            per_row.append(jnp.concatenate(blocks, axis=-1))
        tabs.append(jnp.concatenate(per_row, axis=1))
    return jnp.stack(tabs)


def _na(p, rpb, T):
    B, L, _ = p.shape
    rows = (L - T) // GRID_W
    kh = min(NA_KH, rows)
    tq = NA_GROUP * GRID_W
    assert T % tq == 0 and rows % NA_GROUP == 0
    nct = T // tq
    hp = NA_HEADS // 2
    band, plan, keys = _na_plan(rows, kh)
    bias = _na_bias_tables(rpb, band, keys, kh)
    plan = jnp.asarray(plan, jnp.int32).T
    return pl.pallas_call(
        functools.partial(_na_kernel, T=T, nct=nct, band=band, scale=NA_DH ** -0.5),
        grid_spec=pltpu.PrefetchScalarGridSpec(
            num_scalar_prefetch=1, grid=(B, hp, nct + rows // NA_GROUP),
            in_specs=[pl.BlockSpec((None, tq, LANES), lambda b, h, i, pr: (b, i, h)),
                      pl.BlockSpec((None, L, LANES), lambda b, h, i, pr: (b, 0, hp + h)),
                      pl.BlockSpec((None, L, LANES), lambda b, h, i, pr: (b, 0, 2 * hp + h)),
                      pl.BlockSpec((None, 2, tq, band * GRID_W),
                                   lambda b, h, i, pr: (pr[1, jnp.maximum(i - nct, 0)], h, 0, 0))],
            out_specs=pl.BlockSpec((None, tq, LANES), lambda b, h, i, pr: (b, i, h))),
        out_shape=jax.ShapeDtypeStruct((B, L, NA_HEADS * NA_DH), BF16),
        compiler_params=_cparams(("parallel", "parallel", "arbitrary")),
    )(plan, p, p, p, bias)


def _hgrn_kernel(q_ref, zf_ref, zb_ref, v_ref, gz_ref, llb_ref, l1m_ref, oml_ref, gain_ref,
                 o_ref,
                 qg_ref, ut_ref, oi_ref, el_ref, of_ref, ob_ref, *, T, L):
    n_chunks = L // CHUNK
    nc = T // CHUNK
    ri = lax.broadcasted_iota(jnp.int32, (CHUNK, CHUNK), 0)
    ci = lax.broadcasted_iota(jnp.int32, (CHUNK, CHUNK), 1)
    row = lax.broadcasted_iota(jnp.int32, (CHUNK, LANES), 0)
    eye = ri == ci
    top_bit = 31 - lax.clz(ri ^ ci)
    split_at = [jnp.where(ci < ri, top_bit, -1), jnp.where(ci > ri, top_bit, -1)]

    def shifted(x, k):
        return x if k == 0 else pltpu.roll(x, (-k) % CHUNK, 0)

    def scan_cumsum(x, d):
        k = 1
        while k < CHUNK:
            if d == 0:
                x = x + jnp.where(row >= k, shifted(x, -k), 0.0)
            else:
                x = x + jnp.where(row < CHUNK - k, shifted(x, k), 0.0)
            k *= 2
        return x

    def block_ref(g, size, r_in):
        if size >= 8:
            g3 = g.reshape(CHUNK // size, size, LANES)
            return jnp.broadcast_to(g3[:, r_in:r_in + 1, :], g3.shape).reshape(CHUNK, LANES)
        pos = row & (size - 1)
        out = shifted(g, r_in)
        for m in range(1, size):
            out = jnp.where(pos == m, shifted(g, r_in - m), out)
        return out

    def chunk_prep(n, carry):
        rows = pl.ds(pl.multiple_of(n * CHUNK, CHUNK), CHUNK)
        qs = _silu(q_ref[rows, :])
        v = v_ref[rows, :]
        vt = v.T
        for d in range(2):
            z = (zf_ref if d == 0 else zb_ref)[rows, :]
            t = jnp.exp(-jnp.abs(z))
            log_sig = jnp.minimum(z, 0.0) - jnp.log1p(t)
            logf = jnp.logaddexp(llb_ref[d:d + 1, :], l1m_ref[d:d + 1, :] + log_sig)
            kk = oml_ref[d:d + 1, :] * (jnp.where(z > 0, t, 1.0) / (1.0 + t))
            g = scan_cumsum(logf, d)
            g_tot = g[CHUNK - 1:CHUNK, :] if d == 0 else g[0:1, :]
            a = jnp.where(eye, jnp.sum(qs * kk, axis=1, keepdims=True), 0.0)
            half = CHUNK // 2
            while half >= 1:
                size = 2 * half
                e = jnp.exp(-jnp.abs(g - block_ref(g, size, half if d == 0 else half - 1)))
                a = jnp.where(split_at[d] == half.bit_length() - 1, _mm_nt(qs * e, kk * e), a)
                half //= 2
            oi_ref[d, rows, :] = _mm(a, v)
            qg_ref[d, rows, :] = (qs * jnp.exp(g)).astype(BF16)
            ut_ref[d * n_chunks + n] = _mm(vt, kk * jnp.exp(g_tot - g))
            el_ref[d * n_chunks + n] = jnp.broadcast_to(jnp.exp(g_tot), (8, LANES))
        return carry

    lax.fori_loop(0, n_chunks, chunk_prep, 0, unroll=4)

    def step(d, n, st):
        rows = pl.ds(pl.multiple_of(n * CHUNK, CHUNK), CHUNK)
        o = _mm_nt(qg_ref[d, rows, :], st) + oi_ref[d, rows, :]
        st = st * el_ref[d * n_chunks + n][0:1, :] + ut_ref[d * n_chunks + n]
        return st, o, rows

    def scan_body(i, carry):
        sf, sb = carry
        sf, o_f, rows_f = step(0, i, sf)
        sb, o_b, rows_b = step(1, _bwd_chunk(i, nc, n_chunks), sb)
        of_ref[rows_f, :] = o_f
        ob_ref[rows_b, :] = o_b
        return sf, sb

    zero = jnp.zeros((HG_DV, HG_DK), F32)
    lax.fori_loop(0, n_chunks, scan_body, (zero, zero), unroll=SCAN_UNROLL)
    o = of_ref[...] + ob_ref[...]
    o_ref[...] = (_rms(o) * gain_ref[...] * _silu(gz_ref[...])).astype(BF16)


def _hgrn(p, llb, l1m, oml, gain, T):
    B, L, _ = p.shape
    H = HG_HEADS
    n_chunks = L // CHUNK
    col = lambda off: pl.BlockSpec((None, L, LANES), lambda b, h: (b, 0, off + h))
    lbs = pl.BlockSpec((2, LANES), lambda b, h: (0, h))
    return pl.pallas_call(
        functools.partial(_hgrn_kernel, T=T, L=L),
        grid=(B, H),
        in_specs=[col(12), col(16), col(20), col(24), col(28), lbs, lbs, lbs,
                  pl.BlockSpec((1, LANES), lambda b, h: (0, 0))],
        out_specs=pl.BlockSpec((None, L, LANES), lambda b, h: (b, 0, h)),
        out_shape=jax.ShapeDtypeStruct((B, L, H * HG_DV), BF16),
        scratch_shapes=[pltpu.VMEM((2, L, LANES), BF16),
                        pltpu.VMEM((2 * n_chunks, HG_DV, HG_DK), F32),
                        pltpu.VMEM((2, L, LANES), F32),
                        pltpu.VMEM((2 * n_chunks, 8, LANES), F32),
                        pltpu.VMEM((L, LANES), F32), pltpu.VMEM((L, LANES), F32)],
        compiler_params=_cparams(("parallel", "parallel")),
    )(p, p, p, p, p, llb, l1m, oml, gain.reshape(1, LANES))


def _rot_cols(w):
    q = MLA_ROPE // 4
    return jnp.concatenate([-w[..., q:2 * q], w[..., 0:q], -w[..., 3 * q:4 * q], w[..., 2 * q:3 * q]], -1)


def _even_w_in(w):
    d = w.shape[0]
    qkv_z = GDN_HEADS * (2 * GDN_DK + GDN_DV) + GDN_HEADS * GDN_DV
    ab = w[:, qkv_z:qkv_z + 4 * GDN_HEADS]
    o = qkv_z + 4 * GDN_HEADS
    lora = w[:, o:o + MLA_Q_LORA + MLA_KV_LORA]
    kr = w[:, o + MLA_Q_LORA + MLA_KV_LORA:]
    pad = jnp.zeros((d, LANES - 4 * GDN_HEADS), w.dtype)
    return jnp.concatenate([w[:, :qkv_z], lora, kr, _rot_cols(kr), ab, pad], axis=1).astype(BF16)


def _mla_wq(w):
    w = w.reshape(w.shape[0], MLA_HEADS, MLA_NOPE + MLA_ROPE)
    rope = w[..., MLA_NOPE:]
    return jnp.concatenate([w[..., :MLA_NOPE], rope, _rot_cols(rope)], -1).reshape(w.shape[0], -1).astype(BF16)


def _rope_table(T, S):
    half = MLA_ROPE // 4
    inv = ROPE_THETA ** (-jnp.arange(half, dtype=F32) / half)
    pos = jnp.arange(S)
    ang_r = (pos // GRID_W).astype(F32)[:, None] * inv
    ang_c = (pos % GRID_W).astype(F32)[:, None] * inv
    ang = jnp.concatenate([ang_r, ang_r, ang_c, ang_c], -1)
    lat = jnp.concatenate([jnp.cos(ang), jnp.sin(ang)], -1)
    ctx = jnp.concatenate([jnp.ones((T, MLA_ROPE), F32), jnp.zeros((T, MLA_ROPE), F32)], -1)
    return jnp.concatenate([ctx, lat], 0)


def kernel(x, c, ctx, c_ctx, w_mod, b_mod, ln_g, ln_b, even_w_in, gdn_conv, gdn_a_log, gdn_dt_bias, gdn_norm,
           mla_q_norm, mla_kv_norm, mla_w_uq, mla_w_ukv, even_w_out, odd_w_in, na_rpb, hg_lb, hg_norm, odd_w_out,
           moe_router, moe_w_gate, moe_w_up, moe_w_down):
    B, S, D = x.shape
    T = ctx.shape[1]
    L = T + S
    tm = min(256, T)
    depth = w_mod.shape[0]

    xx = jnp.concatenate([ctx, x], axis=1)
    cc = jnp.concatenate([c, c_ctx[None, :], jnp.zeros((16 - B - 1, D), F32)], axis=0)
    mods = _modulation(cc, w_mod, b_mod)
    mods = mods.reshape(depth, 16, N_MOD, D)
    tabs = jnp.stack([jnp.broadcast_to(mods[:, B:B + 1], (depth, B, N_MOD, D)), mods[:, :B]], axis=2)
    tabs = jnp.transpose(tabs, (0, 3, 1, 2, 4))[:, :, :, :, None, :]

    lb_all = jnp.cumsum(jax.nn.softmax(hg_lb.astype(F32), axis=0), axis=0)
    lb_all = lb_all - lb_all[:1]
    cs = _rope_table(T, S)

    for layer in range(depth):
        j = layer // 2
        tab = tabs[layer]
        if layer % 2 == 0:
            p = _inproj(xx, tab[0], tab[1], _even_w_in(even_w_in[j]), tm, T)
            mix_a = _gdn(p, gdn_conv[j], (-jnp.exp(gdn_a_log[j].astype(F32))).reshape(-1),
                         gdn_dt_bias[j].astype(F32).reshape(-1), gdn_norm[j], T)
            q, k, v = _mlaproj(p, cs, mla_q_norm[j], mla_kv_norm[j], _mla_wq(mla_w_uq[j]),
                               mla_w_ukv[j].astype(BF16), tm)
            mix_b = _mla_attn(q, k, v, T, tm)
            w_out = even_w_out[j]
        else:
            p = _inproj(xx, tab[0], tab[1], odd_w_in[j].astype(BF16), tm, T)
            mix_a = _na(p, na_rpb[j], T)
            lb = lb_all[j]
            mix_b = _hgrn(p, jnp.log(lb), jnp.log1p(-lb), 1.0 - lb, hg_norm[j], T)
            w_out = odd_w_out[j]
        xx, aff = _outproj(mix_a, mix_b, w_out.astype(BF16), xx, tab[2], ln_g[layer, 0], ln_b[layer, 0],
                           tab[3], tab[4], moe_router[layer].T, tm, T)
        idx, gate, cap_c, cap = _topk(aff, T)
        xe = _gather(idx, xx, tab[3], tab[4], cap_c, cap)
        m_rows = B * cap
        ye = _experts(xe.reshape(N_EXPERTS, m_rows, D), moe_w_gate, moe_w_up, moe_w_down, layer,
                      next(t for t in (768, 576, m_rows) if m_rows % t == 0))
        xx = _combine(idx, gate, ye.reshape(N_EXPERTS, B, cap, D), xx, tab[5], ln_g[layer, 1], ln_b[layer, 1],
                      tm, T)
    return xx[:, T:, :]
```

```python
import functools
import math

import jax
import jax.numpy as jnp
from jax import lax
from jax.experimental import pallas as pl
from jax.experimental.pallas import tpu as pltpu

F32 = jnp.float32
BF16 = jnp.bfloat16
HIGHEST = lax.Precision.HIGHEST

DEPTH = 4
GRID_W = 64
N_MOD = 6
EPS = 1e-6
DEEPNORM_ALPHA = (2.0 * DEPTH) ** 0.25

GDN_HEADS = 4
GDN_DK = 128
GDN_DV = 128
GDN_CONV = 5
MLA_HEADS = 4
MLA_Q_LORA = 256
MLA_KV_LORA = 256
MLA_NOPE = 128
MLA_ROPE = 64
MLA_DV = 128
ROPE_THETA = 10000.0
NA_HEADS = 8
NA_DH = 64
NA_KH = 8
NA_KW = 16
HG_HEADS = 4
HG_DK = 128
HG_DV = 128
N_EXPERTS = 16
EC_CAPACITY = 2

CHUNK = 64
NA_GROUP = 4
PREP_CHUNKS = 12
SCAN_UNROLL = 4
SCATTER_GROUP = 8
LANES = 128
MLA_QK_PAD = 256
NEG_BIG = -1e30
VMEM_LIMIT = 56 * 1024 * 1024

_NT = (((1,), (1,)), ((), ()))


def _cparams(sem):
    return pltpu.CompilerParams(dimension_semantics=sem, vmem_limit_bytes=VMEM_LIMIT)


def _mm(a, b):
    return jnp.dot(a.astype(BF16), b.astype(BF16), preferred_element_type=F32)


def _mm_nt(a, b):
    return lax.dot_general(a.astype(BF16), b.astype(BF16), _NT, preferred_element_type=F32)


def _mm32(a, b):
    return jnp.dot(a, b, precision=HIGHEST, preferred_element_type=F32)


def _silu(x):
    return x * jax.nn.sigmoid(x)


def _rms(x):
    return x * lax.rsqrt(jnp.mean(x * x, -1, keepdims=True) + EPS)


def _ln(y, g, b):
    yc = y - jnp.mean(y, -1, keepdims=True)
    return yc * lax.rsqrt(jnp.mean(yc * yc, -1, keepdims=True) + EPS) * g + b


def _mod_kernel(c_ref, w_ref, b_ref, o_ref):
    o_ref[...] = _mm32(_silu(c_ref[...]), w_ref[...]) + b_ref[...]


def _modulation(cc, w_mod, b_mod):
    depth, d, n = w_mod.shape
    tn = n // 4
    return pl.pallas_call(
        _mod_kernel,
        grid=(depth, n // tn),
        in_specs=[pl.BlockSpec((cc.shape[0], d), lambda l, j: (0, 0)),
                  pl.BlockSpec((None, d, tn), lambda l, j: (l, 0, j)),
                  pl.BlockSpec((None, 1, tn), lambda l, j: (l, 0, j))],
        out_specs=pl.BlockSpec((None, cc.shape[0], tn), lambda l, j: (l, 0, j)),
        out_shape=jax.ShapeDtypeStruct((depth, cc.shape[0], n), F32),
        compiler_params=_cparams(("parallel", "parallel")),
    )(cc, w_mod, b_mod.reshape(depth, 1, n))


def _inproj_kernel(x_ref, sh_ref, sc_ref, w_ref, o_ref):
    u = x_ref[...] * (1.0 + sc_ref[...]) + sh_ref[...]
    o_ref[...] = jnp.dot(u.astype(BF16), w_ref[...], preferred_element_type=F32)


def _tab_spec(d, tps, nct):
    return pl.BlockSpec((None, None, 1, d), lambda i: (i // tps, jnp.where(i % tps >= nct, 1, 0), 0, 0))


def _inproj(xx, sh, sc, w, tm, T):
    B, L, D = xx.shape
    n = w.shape[1]
    tps, nct = L // tm, T // tm
    out = pl.pallas_call(
        _inproj_kernel,
        grid=(B * tps,),
        in_specs=[pl.BlockSpec((tm, D), lambda i: (i, 0)),
                  _tab_spec(D, tps, nct), _tab_spec(D, tps, nct),
                  pl.BlockSpec((D, n), lambda i: (0, 0))],
        out_specs=pl.BlockSpec((tm, n), lambda i: (i, 0)),
        out_shape=jax.ShapeDtypeStruct((B * L, n), F32),
        compiler_params=_cparams(("parallel",)),
    )(xx.reshape(B * L, D), sh, sc, w)
    return out.reshape(B, L, n)


def _scan_masks(d):
    ri = lax.broadcasted_iota(jnp.int32, (CHUNK, CHUNK), 0)
    ci = lax.broadcasted_iota(jnp.int32, (CHUNK, CHUNK), 1)
    if d == 0:
        return ci <= ri, ci < ri, ri <= ci, ci == ri
    return ci >= ri, ci > ri, ri >= ci, ci == ri


def _bwd_chunk(i, nc, n):
    return jnp.where(i < nc, nc - 1 - i, n - 1 - (i - nc))


def _gdn_kernel(nega_ref, dtb_ref, q_ref, k_ref, v_ref, z_ref, ab_ref, cwq_ref, cwk_ref, cwv_ref, gain_ref,
                o_ref,
                pad_ref, qn_ref, kn_ref, vn_ref, ncs_ref, kw_ref, ou_ref, qp_ref, el_ref, of_ref, ob_ref,
                *, T, L):
    h = pl.program_id(1)
    n_chunks = L // CHUNK
    nc = T // CHUNK
    prep = max(c for c in range(1, PREP_CHUNKS + 1) if n_chunks % c == 0)

    def conv_prep(src_ref, cw_ref, dst_ref, post):
        zero8 = jnp.zeros((8, LANES), F32)
        pad_ref[0:8, :] = zero8
        pad_ref[8:8 + T, :] = src_ref[0:T, :]
        pad_ref[8 + T:16 + T, :] = zero8
        pad_ref[16 + T:16 + L, :] = src_ref[T:L, :]
        pad_ref[16 + L:24 + L, :] = zero8
        cw = cw_ref[...]

        def body(n, carry):
            r0 = pl.multiple_of(n * CHUNK, CHUNK)
            p0 = r0 + jnp.where(r0 < T, 8, 16) - GDN_CONV // 2
            acc = jnp.zeros((CHUNK, LANES), F32)
            for j in range(GDN_CONV):
                acc = acc + pad_ref[pl.ds(p0 + j, CHUNK), :] * cw[j:j + 1, :]
            dst_ref[pl.ds(r0, CHUNK), :] = post(_silu(acc))
            return carry

        lax.fori_loop(0, n_chunks, body, 0, unroll=prep)

    def l2n(y):
        return y * lax.rsqrt(jnp.sum(y * y, -1, keepdims=True) + EPS)

    conv_prep(q_ref, cwq_ref, qn_ref, lambda y: l2n(y) * (GDN_DK ** -0.5))
    conv_prep(k_ref, cwk_ref, kn_ref, l2n)
    conv_prep(v_ref, cwv_ref, vn_ref, lambda y: y)

    lane = lax.broadcasted_iota(jnp.int32, (CHUNK, LANES), 1)

    def chunk_prep(i, carry):
        chains = []
        for c in range(prep):
            n = prep * i + c
            rows = pl.ds(pl.multiple_of(n * CHUNK, CHUNK), CHUNK)
            q, k, v, ab = qn_ref[rows, :], kn_ref[rows, :], vn_ref[rows, :], ab_ref[rows, :]
            kq = _mm_nt(jnp.concatenate([k, q], axis=0), k)
            for d in range(2):
                chains.append(dict(n=n, d=d, rows=rows, q=q, k=k, v=v, ab=ab, kk=kq[:CHUNK], qk=kq[CHUNK:]))
        for ch in chains:
            d = ch["d"]
            incl, strict, incl_t, eye = _scan_masks(d)
            a_col = jnp.sum(jnp.where(lane == 4 * d + h, ch["ab"], 0.0), axis=1, keepdims=True)
            b_col = jnp.sum(jnp.where(lane == 8 + 4 * d + h, ch["ab"], 0.0), axis=1, keepdims=True)
            g_col = nega_ref[4 * d + h] * jax.nn.softplus(a_col + dtb_ref[4 * d + h])
            beta = jax.nn.sigmoid(b_col)
            g_row = jnp.sum(jnp.where(eye, g_col, 0.0), axis=0, keepdims=True)
            gc_col = jnp.sum(jnp.where(incl, g_row, 0.0), axis=1, keepdims=True)
            gc_row = jnp.sum(jnp.where(incl_t, g_col, 0.0), axis=0, keepdims=True)
            g_tot = jnp.sum(g_col, axis=0, keepdims=True)
            ch["decay"] = jnp.where(incl, jnp.exp(jnp.where(incl, gc_col - gc_row, 0.0)), 0.0)
            ch["kb"] = ch["k"] * beta
            ch["vb"] = ch["v"] * beta
            ch["beta"] = beta
            ch["e_col"] = jnp.exp(gc_col)
            ch["kd"] = ch["k"] * jnp.exp(g_tot - gc_col)
            ch["e_tot"] = jnp.exp(g_tot)
        for ch in chains:
            _, strict, _, eye = _scan_masks(ch["d"])
            a = jnp.where(strict, ch["beta"] * ch["kk"] * ch["decay"], 0.0)
            ch["p"] = jnp.where(eye, 1.0, 0.0) - a
            ch["bp"] = a
        for _ in range(int(math.log2(CHUNK)) - 1):
            for ch in chains:
                ch["bp"] = _mm(ch["bp"], ch["bp"])
            for ch in chains:
                ch["p"] = ch["p"] + _mm(ch["p"], ch["bp"])
        for ch in chains:
            incl = _scan_masks(ch["d"])[0]
            ch["uw"] = _mm(ch["p"], jnp.concatenate([ch["vb"], ch["kb"] * ch["e_col"]], axis=1)).astype(BF16)
            ch["qk"] = jnp.where(incl, ch["qk"] * ch["decay"], 0.0)
        for ch in chains:
            d, n, rows = ch["d"], ch["n"], ch["rows"]
            kd_uw = _mm(ch["kd"].T, ch["uw"])
            qk_uw = _mm(ch["qk"], ch["uw"])
            ncs_ref[d * n_chunks + n] = kd_uw[:, :GDN_DV]
            kw_ref[d * n_chunks + n] = kd_uw[:, GDN_DV:].astype(BF16)
            ou_ref[d, rows, :] = qk_uw[:, :GDN_DV]
            qp_ref[d, rows, :] = (ch["q"] * ch["e_col"] - qk_uw[:, GDN_DV:]).astype(BF16)
            el_ref[d * n_chunks + n] = jnp.broadcast_to(ch["e_tot"], (8, LANES))
        return carry

    lax.fori_loop(0, n_chunks // prep, chunk_prep, 0)

    def step(d, n, s):
        rows = pl.ds(pl.multiple_of(n * CHUNK, CHUNK), CHUNK)
        sb = s.astype(BF16)
        o = jnp.dot(qp_ref[d, rows, :], sb, preferred_element_type=F32) + ou_ref[d, rows, :]
        s = (s * el_ref[d * n_chunks + n][0:1, :] + ncs_ref[d * n_chunks + n]
             - jnp.dot(kw_ref[d * n_chunks + n], sb, preferred_element_type=F32))
        return s, o, rows

    def scan_body(i, carry):
        sf, sb = carry
        sf, o_f, rows_f = step(0, i, sf)
        sb, o_b, rows_b = step(1, _bwd_chunk(i, nc, n_chunks), sb)
        of_ref[rows_f, :] = o_f
        ob_ref[rows_b, :] = o_b
        return sf, sb

    zero = jnp.zeros((GDN_DK, GDN_DV), F32)
    lax.fori_loop(0, n_chunks, scan_body, (zero, zero), unroll=SCAN_UNROLL)

    o = of_ref[...] + ob_ref[...]
    o_ref[...] = (_rms(o) * gain_ref[...] * _silu(z_ref[...])).astype(BF16)


def _gdn(p, conv_w, neg_a, dt_bias, gain, T):
    B, L, _ = p.shape
    H = GDN_HEADS
    n_chunks = L // CHUNK
    col = lambda off: pl.BlockSpec((None, L, LANES), lambda b, h: (b, 0, off + h))
    cw = lambda off: pl.BlockSpec((GDN_CONV, LANES), lambda b, h: (0, off + h))
    smem = pl.BlockSpec(memory_space=pltpu.SMEM)
    return pl.pallas_call(
        functools.partial(_gdn_kernel, T=T, L=L),
        grid=(B, H),
        in_specs=[smem, smem, col(0), col(H), col(2 * H), col(3 * H),
                  pl.BlockSpec((None, L, LANES), lambda b, h: (b, 0, 21)),
                  cw(0), cw(H), cw(2 * H),
                  pl.BlockSpec((1, LANES), lambda b, h: (0, 0))],
        out_specs=pl.BlockSpec((None, L, LANES), lambda b, h: (b, 0, h)),
        out_shape=jax.ShapeDtypeStruct((B, L, H * GDN_DV), BF16),
        scratch_shapes=[pltpu.VMEM((L + 24, LANES), F32),
                        pltpu.VMEM((L, LANES), F32), pltpu.VMEM((L, LANES), F32), pltpu.VMEM((L, LANES), F32),
                        pltpu.VMEM((2 * n_chunks, GDN_DK, GDN_DV), F32),
                        pltpu.VMEM((2 * n_chunks, GDN_DK, GDN_DK), BF16),
                        pltpu.VMEM((2, L, LANES), F32), pltpu.VMEM((2, L, LANES), BF16),
                        pltpu.VMEM((2 * n_chunks, 8, LANES), F32),
                        pltpu.VMEM((L, LANES), F32), pltpu.VMEM((L, LANES), F32)],
        compiler_params=_cparams(("parallel", "parallel")),
    )(neg_a, dt_bias, p, p, p, p, p, conv_w, conv_w, conv_w, gain.reshape(1, LANES))


def _mlaproj_kernel(ql_ref, kvl_ref, kr_ref, cs_ref, qg_ref, kvg_ref, wq_ref, wkv_ref, q_ref, k_ref, v_ref):
    q = jnp.dot((_rms(ql_ref[...]) * qg_ref[...]).astype(BF16), wq_ref[...], preferred_element_type=F32)
    kv = jnp.dot((_rms(kvl_ref[...]) * kvg_ref[...]).astype(BF16), wkv_ref[...], preferred_element_type=F32)
    cs = cs_ref[...]
    lane = lax.broadcasted_iota(jnp.int32, cs.shape, 1)

    def rope(blk):
        t = blk * cs
        return jnp.where(lane < MLA_ROPE, t + pltpu.roll(t, MLA_ROPE, 1), 0.0)

    kr = rope(kr_ref[...]).astype(BF16)
    for h in range(MLA_HEADS):
        o = h * MLA_QK_PAD
        q_ref[:, o:o + MLA_NOPE] = q[:, o:o + MLA_NOPE].astype(BF16)
        q_ref[:, o + MLA_NOPE:o + MLA_QK_PAD] = rope(q[:, o + MLA_NOPE:o + MLA_QK_PAD]).astype(BF16)
        k_ref[:, o:o + MLA_NOPE] = kv[:, o:o + MLA_NOPE].astype(BF16)
        k_ref[:, o + MLA_NOPE:o + MLA_QK_PAD] = kr
        v_ref[:, h * MLA_DV:(h + 1) * MLA_DV] = kv[:, o + MLA_NOPE:o + MLA_QK_PAD].astype(BF16)


def _mlaproj(p, cs, q_gain, kv_gain, wq, wkv, tm):
    B, L, n = p.shape
    tps = L // tm
    p2 = p.reshape(B * L, n)
    hq = MLA_HEADS * MLA_QK_PAD
    q, k, v = pl.pallas_call(
        _mlaproj_kernel,
        grid=(B * tps,),
        in_specs=[pl.BlockSpec((tm, MLA_Q_LORA), lambda i: (i, 8)),
                  pl.BlockSpec((tm, MLA_KV_LORA), lambda i: (i, 9)),
                  pl.BlockSpec((tm, LANES), lambda i: (i, 20)),
                  pl.BlockSpec((tm, LANES), lambda i: (i % tps, 0)),
                  pl.BlockSpec((1, MLA_Q_LORA), lambda i: (0, 0)),
                  pl.BlockSpec((1, MLA_KV_LORA), lambda i: (0, 0)),
                  pl.BlockSpec((MLA_Q_LORA, hq), lambda i: (0, 0)),
                  pl.BlockSpec((MLA_KV_LORA, hq), lambda i: (0, 0))],
        out_specs=[pl.BlockSpec((tm, hq), lambda i: (i, 0)),
                   pl.BlockSpec((tm, hq), lambda i: (i, 0)),
                   pl.BlockSpec((tm, MLA_HEADS * MLA_DV), lambda i: (i, 0))],
        out_shape=[jax.ShapeDtypeStruct((B * L, hq), BF16),
                   jax.ShapeDtypeStruct((B * L, hq), BF16),
                   jax.ShapeDtypeStruct((B * L, MLA_HEADS * MLA_DV), BF16)],
        compiler_params=_cparams(("parallel",)),
    )(p2, p2, p2, cs, q_gain.reshape(1, -1), kv_gain.reshape(1, -1), wq, wkv)
    return q.reshape(B, L, hq), k.reshape(B, L, hq), v.reshape(B, L, -1)


def _mla_attn_kernel(q_ref, k_ref, v_ref, o_ref, *, T, nct, scale):
    qt = pl.program_id(2)
    q = q_ref[...]

    def attend(n):
        s = lax.dot_general(q, k_ref[0:n, :], _NT, preferred_element_type=F32) * scale
        p = jnp.exp(s - jnp.max(s, axis=-1, keepdims=True))
        l = jnp.sum(p, axis=-1, keepdims=True)
        o_ref[...] = (jnp.dot(p.astype(BF16), v_ref[0:n, :], preferred_element_type=F32) / l).astype(BF16)

    @pl.when(qt < nct)
    def _():
        attend(T)

    @pl.when(qt >= nct)
    def _():
        attend(k_ref.shape[0])


def _mla_attn(q, k, v, T, tq):
    B, L, _ = q.shape
    H = MLA_HEADS
    scale = (MLA_NOPE + MLA_ROPE) ** -0.5
    return pl.pallas_call(
        functools.partial(_mla_attn_kernel, T=T, nct=T // tq, scale=scale),
        grid=(B, H, L // tq),
        in_specs=[pl.BlockSpec((None, tq, MLA_QK_PAD), lambda b, h, i: (b, i, h)),
                  pl.BlockSpec((None, L, MLA_QK_PAD), lambda b, h, i: (b, 0, h)),
                  pl.BlockSpec((None, L, MLA_DV), lambda b, h, i: (b, 0, h))],
        out_specs=pl.BlockSpec((None, tq, MLA_DV), lambda b, h, i: (b, i, h)),
        out_shape=jax.ShapeDtypeStruct((B, L, H * MLA_DV), BF16),
        compiler_params=_cparams(("parallel", "parallel", "arbitrary")),
    )(q, k, v)


def _outproj_kernel(a_ref, b_ref, w_ref, x_ref, gate_ref, lng_ref, lnb_ref, sh_ref, sc_ref, wr_ref,
                    xo_ref, aff_ref):
    ka = a_ref.shape[1]
    o = (jnp.dot(a_ref[...], w_ref[0:ka, :], preferred_element_type=F32)
         + jnp.dot(b_ref[...], w_ref[ka:, :], preferred_element_type=F32))
    xn = _ln(DEEPNORM_ALPHA * x_ref[...] + gate_ref[...] * o, lng_ref[...], lnb_ref[...])
    xo_ref[...] = xn
    hmod = xn * (1.0 + sc_ref[...]) + sh_ref[...]
    logits = _mm_nt(wr_ref[...], hmod)
    e = jnp.exp(logits - jnp.max(logits, axis=0, keepdims=True))
    aff_ref[...] = e / jnp.sum(e, axis=0, keepdims=True)


def _outproj(a, b, w, xx, gate, ln_g, ln_b, sh, sc, wr_t, tm, T):
    B, L, D = xx.shape
    tps, nct = L // tm, T // tm
    ka, kb = a.shape[-1], b.shape[-1]
    row = lambda n: pl.BlockSpec((1, n), lambda i: (0, 0))
    xo, aff = pl.pallas_call(
        _outproj_kernel,
        grid=(B * tps,),
        in_specs=[pl.BlockSpec((tm, ka), lambda i: (i, 0)),
                  pl.BlockSpec((tm, kb), lambda i: (i, 0)),
                  pl.BlockSpec((ka + kb, D), lambda i: (0, 0)),
                  pl.BlockSpec((tm, D), lambda i: (i, 0)),
                  _tab_spec(D, tps, nct), row(D), row(D),
                  _tab_spec(D, tps, nct), _tab_spec(D, tps, nct),
                  pl.BlockSpec((N_EXPERTS, D), lambda i: (0, 0))],
        out_specs=[pl.BlockSpec((tm, D), lambda i: (i, 0)),
                   pl.BlockSpec((None, N_EXPERTS, tm), lambda i: (i // tps, 0, i % tps))],
        out_shape=[jax.ShapeDtypeStruct((B * L, D), F32),
                   jax.ShapeDtypeStruct((B, N_EXPERTS, L), F32)],
        compiler_params=_cparams(("parallel",)),
    )(a.reshape(B * L, ka), b.reshape(B * L, kb), w, xx.reshape(B * L, D), gate,
      ln_g.reshape(1, D), ln_b.reshape(1, D), sh, sc, wr_t)
    return xo.reshape(B, L, D), aff


def _topk_kernel(aff_ref, idx_ref, gate_ref, *, T, cap_c, cap_l):
    ri = lax.broadcasted_iota(jnp.int32, (LANES, LANES), 0)
    ci = lax.broadcasted_iota(jnp.int32, (LANES, LANES), 1)
    tri = jnp.where(ri <= ci, 1.0, 0.0).astype(BF16)

    def prefix_incl(x):
        carry = jnp.zeros((x.shape[0], 1), F32)
        outs = []
        for j in range(x.shape[1] // LANES):
            pj = jnp.dot(x[:, j * LANES:(j + 1) * LANES].astype(BF16), tri, preferred_element_type=F32) + carry
            outs.append(pj)
            carry = pj[:, LANES - 1:LANES]
        return jnp.concatenate(outs, axis=1)

    def thresholds(segs):
        half = lambda a, b: a + lax.shift_right_logical(b - a, 1)

        def ge_cap(a, cap, bits):
            cnt = jnp.sum(jnp.where(a >= pltpu.bitcast(bits, F32), 1.0, 0.0), axis=1, keepdims=True)
            return cnt >= float(cap)

        def bisect2(_, state):
            out = []
            for (a, cap), (lo, hi) in zip(segs, state):
                mid = half(lo, hi)
                ml, mh = half(lo, mid), half(mid, hi)
                ge_m, ge_l, ge_h = ge_cap(a, cap, mid), ge_cap(a, cap, ml), ge_cap(a, cap, mh)
                out.append((jnp.where(ge_m, jnp.where(ge_h, mh, mid), jnp.where(ge_l, ml, lo)),
                            jnp.where(ge_m, jnp.where(ge_h, hi, mh), jnp.where(ge_l, mid, ml))))
            return tuple(out)

        ne = segs[0][0].shape[0]
        init = tuple((jnp.zeros((ne, 1), jnp.int32), jnp.full((ne, 1), 0x7F800000, jnp.int32)) for _ in segs)
        return [pltpu.bitcast(lo, F32) for lo, _ in lax.fori_loop(0, 16, bisect2, init)]

    def select(aff, thr, cap, base, row0):
        ne, n = aff.shape
        gt = aff > thr
        eqf = jnp.where(aff == thr, 1.0, 0.0)
        need = float(cap) - jnp.sum(jnp.where(gt, 1.0, 0.0), axis=1, keepdims=True)
        eq_before = prefix_incl(eqf) - eqf
        self_ = jnp.where(gt, 1.0, jnp.where(eq_before < need, eqf, 0.0))
        cnt = prefix_incl(self_)
        slot = lax.broadcasted_iota(jnp.int32, (cap, 1), 0).astype(F32)
        lane = lax.broadcasted_iota(jnp.int32, (cap, LANES), 1)
        idx_acc = jnp.zeros((cap, LANES), F32)
        gate_acc = jnp.zeros((cap, LANES), F32)
        for e in range(ne):
            ce, picked = cnt[e:e + 1, :], self_[e:e + 1, :] * aff[e:e + 1, :]
            idx_e = jnp.sum(jnp.where(ce <= slot, 1.0, 0.0), axis=1, keepdims=True)
            gate_e = jnp.sum(jnp.where(ce == slot + 1.0, picked, 0.0), axis=1, keepdims=True)
            idx_acc = jnp.where(lane == e, idx_e + float(base), idx_acc)
            gate_acc = jnp.where(lane == e, gate_e, gate_acc)
        idx_ref[row0:row0 + cap, :] = idx_acc.astype(jnp.int32)
        gate_ref[row0:row0 + cap, :] = gate_acc

    aff = aff_ref[...]
    aff_c, aff_l = aff[:, 0:T], aff[:, T:]
    thr_c, thr_l = thresholds([(aff_c, cap_c), (aff_l, cap_l)])
    select(aff_c, thr_c, cap_c, 0, 0)
    select(aff_l, thr_l, cap_l, T, cap_c)


def _topk(aff, T):
    B, E, L = aff.shape
    cap_c = EC_CAPACITY * T // E
    cap_l = EC_CAPACITY * (L - T) // E
    cap = cap_c + cap_l
    idx, gate = pl.pallas_call(
        functools.partial(_topk_kernel, T=T, cap_c=cap_c, cap_l=cap_l),
        grid=(B,),
        in_specs=[pl.BlockSpec((None, E, L), lambda b: (b, 0, 0))],
        out_specs=[pl.BlockSpec((None, cap, LANES), lambda b: (b, 0, 0)),
                   pl.BlockSpec((None, cap, LANES), lambda b: (b, 0, 0))],
        out_shape=[jax.ShapeDtypeStruct((B, cap, LANES), jnp.int32),
                   jax.ShapeDtypeStruct((B, cap, LANES), F32)],
        compiler_params=_cparams(("parallel",)),
    )(aff)
    idx = jnp.swapaxes(idx[:, :, :E], 1, 2).reshape(B * E * cap)
    gate = jnp.swapaxes(gate[:, :, :E], 1, 2).reshape(B * E * cap)
    return idx, gate, cap_c, cap


def _gather_kernel(idx_ref, x_ref, sh_ref, sc_ref, o_ref, stage_ref, *, cap_c, cap):
    base = (pl.program_id(0) * pl.num_programs(1) + pl.program_id(1)) * cap

    def body(s, carry):
        stage_ref[pl.ds(s, 1), :] = x_ref[pl.ds(idx_ref[base + s], 1), :]
        return carry

    lax.fori_loop(0, cap, body, 0, unroll=8)
    o_ref[0:cap_c, :] = (stage_ref[0:cap_c, :] * (1.0 + sc_ref[0]) + sh_ref[0]).astype(BF16)
    o_ref[cap_c:, :] = (stage_ref[cap_c:, :] * (1.0 + sc_ref[1]) + sh_ref[1]).astype(BF16)


def _gather(idx, xx, sh, sc, cap_c, cap):
    B, L, D = xx.shape
    E = N_EXPERTS
    tab = pl.BlockSpec((None, 2, 1, D), lambda b, e, idx: (b, 0, 0, 0))
    return pl.pallas_call(
        functools.partial(_gather_kernel, cap_c=cap_c, cap=cap),
        grid_spec=pltpu.PrefetchScalarGridSpec(
            num_scalar_prefetch=1, grid=(B, E),
            in_specs=[pl.BlockSpec((None, L, D), lambda b, e, idx: (b, 0, 0)), tab, tab],
            out_specs=pl.BlockSpec((None, None, cap, D), lambda b, e, idx: (e, b, 0, 0)),
            scratch_shapes=[pltpu.VMEM((cap, D), F32)]),
        out_shape=jax.ShapeDtypeStruct((E, B, cap, D), BF16),
        compiler_params=_cparams(("parallel", "arbitrary")),
    )(idx, xx, sh, sc)


def _expert_kernel(x_ref, wg_ref, wu_ref, wd_ref, o_ref):
    x = x_ref[...]
    g = jnp.dot(x, wg_ref[...].astype(BF16), preferred_element_type=F32)
    u = jnp.dot(x, wu_ref[...].astype(BF16), preferred_element_type=F32)
    hid = (_silu(g) * u).astype(BF16)
    o_ref[...] = jnp.dot(hid, wd_ref[...].astype(BF16), preferred_element_type=F32)


def _experts(xe, w_gate, w_up, w_down, layer, tm):
    E, M, D = xe.shape
    F = w_gate.shape[-1]
    return pl.pallas_call(
        _expert_kernel,
        grid=(E, M // tm),
        in_specs=[pl.BlockSpec((None, tm, D), lambda e, m: (e, m, 0)),
                  pl.BlockSpec((None, None, D, F), lambda e, m: (layer, e, 0, 0)),
                  pl.BlockSpec((None, None, D, F), lambda e, m: (layer, e, 0, 0)),
                  pl.BlockSpec((None, None, F, D), lambda e, m: (layer, e, 0, 0))],
        out_specs=pl.BlockSpec((None, tm, D), lambda e, m: (e, m, 0)),
        out_shape=jax.ShapeDtypeStruct((E, M, D), F32),
        compiler_params=_cparams(("parallel", "arbitrary")),
    )(xe, w_gate, w_up, w_down)


def _combine_kernel(idx_ref, gate_ref, ye_ref, x_ref, gt_ref, lng_ref, lnb_ref, o_ref, *, cap, T, tm):
    e = pl.program_id(1)
    n_e = pl.num_programs(1)
    base = (pl.program_id(0) * n_e + e) * cap

    @pl.when(e == 0)
    def _():
        o_ref[...] = jnp.zeros(o_ref.shape, F32)

    def body(i, carry):
        s0 = base + i * SCATTER_GROUP
        rs = [idx_ref[s0 + j] for j in range(SCATTER_GROUP)]
        new = [o_ref[pl.ds(rs[j], 1), :] + gate_ref[s0 + j] * ye_ref[pl.ds(i * SCATTER_GROUP + j, 1), :]
               for j in range(SCATTER_GROUP)]
        for j in range(SCATTER_GROUP):
            o_ref[pl.ds(rs[j], 1), :] = new[j]
        return carry

    lax.fori_loop(0, cap // SCATTER_GROUP, body, 0)

    @pl.when(e == n_e - 1)
    def _():
        def tile(t, carry):
            rows = pl.ds(pl.multiple_of(t * tm, tm), tm)
            gt = jnp.where(t < T // tm, gt_ref[0], gt_ref[1])
            o_ref[rows, :] = _ln(DEEPNORM_ALPHA * x_ref[rows, :] + gt * o_ref[rows, :], lng_ref[...], lnb_ref[...])
            return carry

        lax.fori_loop(0, o_ref.shape[0] // tm, tile, 0)


def _combine(idx, gate, ye, xx, gt, ln_g, ln_b, tm, T):
    E, B, cap, D = ye.shape
    L = xx.shape[1]
    row = pl.BlockSpec((1, D), lambda b, e, i, g: (0, 0))
    return pl.pallas_call(
        functools.partial(_combine_kernel, cap=cap, T=T, tm=tm),
        grid_spec=pltpu.PrefetchScalarGridSpec(
            num_scalar_prefetch=2, grid=(B, E),
            in_specs=[pl.BlockSpec((None, None, cap, D), lambda b, e, i, g: (e, b, 0, 0)),
                      pl.BlockSpec((None, L, D), lambda b, e, i, g: (b, 0, 0)),
                      pl.BlockSpec((None, 2, 1, D), lambda b, e, i, g: (b, 0, 0, 0)),
                      row, row],
            out_specs=pl.BlockSpec((None, L, D), lambda b, e, i, g: (b, 0, 0))),
        out_shape=jax.ShapeDtypeStruct((B, L, D), F32),
        compiler_params=_cparams(("parallel", "arbitrary")),
    )(idx, gate, ye, xx, gt, ln_g.reshape(1, D), ln_b.reshape(1, D))


def _na_plan(rows, kh):
    band = min(kh + NA_GROUP - 1, rows)
    plan, keys = [], []
    for g in range(rows // NA_GROUP):
        q0 = NA_GROUP * g
        win = [min(max(q0 + a - kh // 2, 0), rows - kh) for a in range(NA_GROUP)]
        ub = min(win[0], rows - band)
        key = (ub - q0, tuple(w - q0 for w in win))
        if key not in keys:
            keys.append(key)
        plan.append((ub, keys.index(key)))
    return band, plan, keys


def _na_kernel(plan_ref, q_ref, k_ref, v_ref, bias_ref, o_ref, *, T, nct, band, scale):
    i = pl.program_id(2)
    q = q_ref[...]
    lane = lax.broadcasted_iota(jnp.int32, q.shape, 1)
    kc = k_ref[0:T, :].astype(BF16)
    vc = v_ref[0:T, :].astype(BF16)
    qh = [jnp.where(lane < NA_DH, q, 0.0).astype(BF16), jnp.where(lane >= NA_DH, q, 0.0).astype(BF16)]

    @pl.when(i < nct)
    def _():
        outs = []
        for j in range(2):
            s_c = lax.dot_general(qh[j], kc, _NT, preferred_element_type=F32) * scale
            p_c = jnp.exp(s_c - jnp.max(s_c, axis=-1, keepdims=True))
            outs.append(jnp.dot(p_c.astype(BF16), vc, preferred_element_type=F32)
                        / jnp.sum(p_c, axis=-1, keepdims=True))
        o_ref[...] = jnp.where(lane < NA_DH, outs[0], outs[1]).astype(BF16)

    @pl.when(i >= nct)
    def _():
        t0 = pl.multiple_of(T + plan_ref[0, i - nct] * GRID_W, GRID_W)
        kb = k_ref[pl.ds(t0, band * GRID_W), :].astype(BF16)
        vb = v_ref[pl.ds(t0, band * GRID_W), :].astype(BF16)
        outs = []
        for j in range(2):
            s_w = lax.dot_general(qh[j], kb, _NT, preferred_element_type=F32) * scale + bias_ref[j]
            s_c = lax.dot_general(qh[j], kc, _NT, preferred_element_type=F32) * scale
            m = jnp.maximum(jnp.max(s_w, axis=-1, keepdims=True), jnp.max(s_c, axis=-1, keepdims=True))
            p_w = jnp.exp(s_w - m)
            p_c = jnp.exp(s_c - m)
            l = jnp.sum(p_w, axis=-1, keepdims=True) + jnp.sum(p_c, axis=-1, keepdims=True)
            outs.append((jnp.dot(p_w.astype(BF16), vb, preferred_element_type=F32)
                         + jnp.dot(p_c.astype(BF16), vc, preferred_element_type=F32)) / l)
        o_ref[...] = jnp.where(lane < NA_DH, outs[0], outs[1]).astype(BF16)


def _na_bias_tables(rpb, band, keys, kh):
    w = jnp.arange(GRID_W)
    c0 = jnp.clip(w - NA_KW // 2, 0, GRID_W - NA_KW)
    col_ok = (w[None, :] >= c0[:, None]) & (w[None, :] < c0[:, None] + NA_KW)
    col_off = jnp.clip(w[None, :] - w[:, None] + (NA_KW - 1), 0, 2 * NA_KW - 2)
    onehot = (col_off[:, :, None] == jnp.arange(2 * NA_KW - 1)).astype(F32)
    base = jnp.einsum('hrm,wcm->hrwc', rpb.astype(F32), onehot, precision=HIGHEST)
    base = jnp.where(col_ok[None, None], base, NEG_BIG)
    masked = jnp.full((NA_HEADS, GRID_W, GRID_W), NEG_BIG, F32)
    tabs = []
    for ub_off, win_offs in keys:
        per_row = []
        for a in range(NA_GROUP):
            blocks = []
            for u in range(band):
                rel = ub_off + u
                inside = win_offs[a] <= rel < win_offs[a] + kh
                blocks.append(base[:, rel - a + NA_KH - 1] if inside else masked)
            per_row.append(jnp.concatenate(blocks, axis=-1))
        tabs.append(jnp.concatenate(per_row, axis=1))
    return jnp.stack(tabs)


def _na(p, rpb, T):
    B, L, _ = p.shape
    rows = (L - T) // GRID_W
    kh = min(NA_KH, rows)
    tq = NA_GROUP * GRID_W
    assert T % tq == 0 and rows % NA_GROUP == 0
    nct = T // tq
    hp = NA_HEADS // 2
    band, plan, keys = _na_plan(rows, kh)
    bias = _na_bias_tables(rpb, band, keys, kh)
    plan = jnp.asarray(plan, jnp.int32).T
    return pl.pallas_call(
        functools.partial(_na_kernel, T=T, nct=nct, band=band, scale=NA_DH ** -0.5),
        grid_spec=pltpu.PrefetchScalarGridSpec(
            num_scalar_prefetch=1, grid=(B, hp, nct + rows // NA_GROUP),
            in_specs=[pl.BlockSpec((None, tq, LANES), lambda b, h, i, pr: (b, i, h)),
                      pl.BlockSpec((None, L, LANES), lambda b, h, i, pr: (b, 0, hp + h)),
                      pl.BlockSpec((None, L, LANES), lambda b, h, i, pr: (b, 0, 2 * hp + h)),
                      pl.BlockSpec((None, 2, tq, band * GRID_W),
                                   lambda b, h, i, pr: (pr[1, jnp.maximum(i - nct, 0)], h, 0, 0))],
            out_specs=pl.BlockSpec((None, tq, LANES), lambda b, h, i, pr: (b, i, h))),
        out_shape=jax.ShapeDtypeStruct((B, L, NA_HEADS * NA_DH), BF16),
        compiler_params=_cparams(("parallel", "parallel", "arbitrary")),
    )(plan, p, p, p, bias)


def _hgrn_kernel(q_ref, zf_ref, zb_ref, v_ref, gz_ref, llb_ref, l1m_ref, oml_ref, gain_ref,
                 o_ref,
                 qg_ref, ut_ref, oi_ref, el_ref, of_ref, ob_ref, *, T, L):
    n_chunks = L // CHUNK
    nc = T // CHUNK
    ri = lax.broadcasted_iota(jnp.int32, (CHUNK, CHUNK), 0)
    ci = lax.broadcasted_iota(jnp.int32, (CHUNK, CHUNK), 1)
    row = lax.broadcasted_iota(jnp.int32, (CHUNK, LANES), 0)
    eye = ri == ci
    top_bit = 31 - lax.clz(ri ^ ci)
    split_at = [jnp.where(ci < ri, top_bit, -1), jnp.where(ci > ri, top_bit, -1)]

    def shifted(x, k):
        return x if k == 0 else pltpu.roll(x, (-k) % CHUNK, 0)

    def scan_cumsum(x, d):
        k = 1
        while k < CHUNK:
            if d == 0:
                x = x + jnp.where(row >= k, shifted(x, -k), 0.0)
            else:
                x = x + jnp.where(row < CHUNK - k, shifted(x, k), 0.0)
            k *= 2
        return x

    def block_ref(g, size, r_in):
        if size >= 8:
            g3 = g.reshape(CHUNK // size, size, LANES)
            return jnp.broadcast_to(g3[:, r_in:r_in + 1, :], g3.shape).reshape(CHUNK, LANES)
        pos = row & (size - 1)
        out = shifted(g, r_in)
        for m in range(1, size):
            out = jnp.where(pos == m, shifted(g, r_in - m), out)
        return out

    def chunk_prep(n, carry):
        rows = pl.ds(pl.multiple_of(n * CHUNK, CHUNK), CHUNK)
        qs = _silu(q_ref[rows, :])
        v = v_ref[rows, :]
        vt = v.T
        for d in range(2):
            z = (zf_ref if d == 0 else zb_ref)[rows, :]
            t = jnp.exp(-jnp.abs(z))
            log_sig = jnp.minimum(z, 0.0) - jnp.log1p(t)
            logf = jnp.logaddexp(llb_ref[d:d + 1, :], l1m_ref[d:d + 1, :] + log_sig)
            kk = oml_ref[d:d + 1, :] * (jnp.where(z > 0, t, 1.0) / (1.0 + t))
            g = scan_cumsum(logf, d)
            g_tot = g[CHUNK - 1:CHUNK, :] if d == 0 else g[0:1, :]
            a = jnp.where(eye, jnp.sum(qs * kk, axis=1, keepdims=True), 0.0)
            half = CHUNK // 2
            while half >= 1:
                size = 2 * half
                e = jnp.exp(-jnp.abs(g - block_ref(g, size, half if d == 0 else half - 1)))
                a = jnp.where(split_at[d] == half.bit_length() - 1, _mm_nt(qs * e, kk * e), a)
                half //= 2
            oi_ref[d, rows, :] = _mm(a, v)
            qg_ref[d, rows, :] = (qs * jnp.exp(g)).astype(BF16)
            ut_ref[d * n_chunks + n] = _mm(vt, kk * jnp.exp(g_tot - g))
            el_ref[d * n_chunks + n] = jnp.broadcast_to(jnp.exp(g_tot), (8, LANES))
        return carry

    lax.fori_loop(0, n_chunks, chunk_prep, 0, unroll=6)

    def step(d, n, st):
        rows = pl.ds(pl.multiple_of(n * CHUNK, CHUNK), CHUNK)
        o = _mm_nt(qg_ref[d, rows, :], st) + oi_ref[d, rows, :]
        st = st * el_ref[d * n_chunks + n][0:1, :] + ut_ref[d * n_chunks + n]
        return st, o, rows

    def scan_body(i, carry):
        sf, sb = carry
        sf, o_f, rows_f = step(0, i, sf)
        sb, o_b, rows_b = step(1, _bwd_chunk(i, nc, n_chunks), sb)
        of_ref[rows_f, :] = o_f
        ob_ref[rows_b, :] = o_b
        return sf, sb

    zero = jnp.zeros((HG_DV, HG_DK), F32)
    lax.fori_loop(0, n_chunks, scan_body, (zero, zero), unroll=SCAN_UNROLL)
    o = of_ref[...] + ob_ref[...]
    o_ref[...] = (_rms(o) * gain_ref[...] * _silu(gz_ref[...])).astype(BF16)


def _hgrn(p, llb, l1m, oml, gain, T):
    B, L, _ = p.shape
    H = HG_HEADS
    n_chunks = L // CHUNK
    col = lambda off: pl.BlockSpec((None, L, LANES), lambda b, h: (b, 0, off + h))
    lbs = pl.BlockSpec((2, LANES), lambda b, h: (0, h))
    return pl.pallas_call(
        functools.partial(_hgrn_kernel, T=T, L=L),
        grid=(B, H),
        in_specs=[col(12), col(16), col(20), col(24), col(28), lbs, lbs, lbs,
                  pl.BlockSpec((1, LANES), lambda b, h: (0, 0))],
        out_specs=pl.BlockSpec((None, L, LANES), lambda b, h: (b, 0, h)),
        out_shape=jax.ShapeDtypeStruct((B, L, H * HG_DV), BF16),
        scratch_shapes=[pltpu.VMEM((2, L, LANES), BF16),
                        pltpu.VMEM((2 * n_chunks, HG_DV, HG_DK), F32),
                        pltpu.VMEM((2, L, LANES), F32),
                        pltpu.VMEM((2 * n_chunks, 8, LANES), F32),
                        pltpu.VMEM((L, LANES), F32), pltpu.VMEM((L, LANES), F32)],
        compiler_params=_cparams(("parallel", "parallel")),
    )(p, p, p, p, p, llb, l1m, oml, gain.reshape(1, LANES))


def _rot_cols(w):
    q = MLA_ROPE // 4
    return jnp.concatenate([-w[..., q:2 * q], w[..., 0:q], -w[..., 3 * q:4 * q], w[..., 2 * q:3 * q]], -1)


def _even_w_in(w):
    d = w.shape[0]
    qkv_z = GDN_HEADS * (2 * GDN_DK + GDN_DV) + GDN_HEADS * GDN_DV
    ab = w[:, qkv_z:qkv_z + 4 * GDN_HEADS]
    o = qkv_z + 4 * GDN_HEADS
    lora = w[:, o:o + MLA_Q_LORA + MLA_KV_LORA]
    kr = w[:, o + MLA_Q_LORA + MLA_KV_LORA:]
    pad = jnp.zeros((d, LANES - 4 * GDN_HEADS), w.dtype)
    return jnp.concatenate([w[:, :qkv_z], lora, kr, _rot_cols(kr), ab, pad], axis=1).astype(BF16)


def _mla_wq(w):
    w = w.reshape(w.shape[0], MLA_HEADS, MLA_NOPE + MLA_ROPE)
    rope = w[..., MLA_NOPE:]
    return jnp.concatenate([w[..., :MLA_NOPE], rope, _rot_cols(rope)], -1).reshape(w.shape[0], -1).astype(BF16)


def _rope_table(T, S):
    half = MLA_ROPE // 4
    inv = ROPE_THETA ** (-jnp.arange(half, dtype=F32) / half)
    pos = jnp.arange(S)
    ang_r = (pos // GRID_W).astype(F32)[:, None] * inv
    ang_c = (pos % GRID_W).astype(F32)[:, None] * inv
    ang = jnp.concatenate([ang_r, ang_r, ang_c, ang_c], -1)
    lat = jnp.concatenate([jnp.cos(ang), jnp.sin(ang)], -1)
    ctx = jnp.concatenate([jnp.ones((T, MLA_ROPE), F32), jnp.zeros((T, MLA_ROPE), F32)], -1)
    return jnp.concatenate([ctx, lat], 0)


def kernel(x, c, ctx, c_ctx, w_mod, b_mod, ln_g, ln_b, even_w_in, gdn_conv, gdn_a_log, gdn_dt_bias, gdn_norm,
           mla_q_norm, mla_kv_norm, mla_w_uq, mla_w_ukv, even_w_out, odd_w_in, na_rpb, hg_lb, hg_norm, odd_w_out,
           moe_router, moe_w_gate, moe_w_up, moe_w_down):
    B, S, D = x.shape
    T = ctx.shape[1]
    L = T + S
    tm = min(256, T)
    depth = w_mod.shape[0]

    xx = jnp.concatenate([ctx, x], axis=1)
    cc = jnp.concatenate([c, c_ctx[None, :], jnp.zeros((16 - B - 1, D), F32)], axis=0)
    mods = _modulation(cc, w_mod, b_mod)
    mods = mods.reshape(depth, 16, N_MOD, D)
    tabs = jnp.stack([jnp.broadcast_to(mods[:, B:B + 1], (depth, B, N_MOD, D)), mods[:, :B]], axis=2)
    tabs = jnp.transpose(tabs, (0, 3, 1, 2, 4))[:, :, :, :, None, :]

    lb_all = jnp.cumsum(jax.nn.softmax(hg_lb.astype(F32), axis=0), axis=0)
    lb_all = lb_all - lb_all[:1]
    cs = _rope_table(T, S)

    for layer in range(depth):
        j = layer // 2
        tab = tabs[layer]
        if layer % 2 == 0:
            p = _inproj(xx, tab[0], tab[1], _even_w_in(even_w_in[j]), tm, T)
            mix_a = _gdn(p, gdn_conv[j], (-jnp.exp(gdn_a_log[j].astype(F32))).reshape(-1),
                         gdn_dt_bias[j].astype(F32).reshape(-1), gdn_norm[j], T)
            q, k, v = _mlaproj(p, cs, mla_q_norm[j], mla_kv_norm[j], _mla_wq(mla_w_uq[j]),
                               mla_w_ukv[j].astype(BF16), tm)
            mix_b = _mla_attn(q, k, v, T, tm)
            w_out = even_w_out[j]
        else:
            p = _inproj(xx, tab[0], tab[1], odd_w_in[j].astype(BF16), tm, T)
            mix_a = _na(p, na_rpb[j], T)
            lb = lb_all[j]
            mix_b = _hgrn(p, jnp.log(lb), jnp.log1p(-lb), 1.0 - lb, hg_norm[j], T)
            w_out = odd_w_out[j]
        xx, aff = _outproj(mix_a, mix_b, w_out.astype(BF16), xx, tab[2], ln_g[layer, 0], ln_b[layer, 0],
                           tab[3], tab[4], moe_router[layer].T, tm, T)
        idx, gate, cap_c, cap = _topk(aff, T)
        xe = _gather(idx, xx, tab[3], tab[4], cap_c, cap)
        m_rows = B * cap
        ye = _experts(xe.reshape(N_EXPERTS, m_rows, D), moe_w_gate, moe_w_up, moe_w_down, layer,
                      next(t for t in (768, 576, m_rows) if m_rows % t == 0))
        xx = _combine(idx, gate, ye.reshape(N_EXPERTS, B, cap, D), xx, tab[5], ln_g[layer, 1], ln_b[layer, 1],
                      tm, T)
    return xx[:, T:, :]
```

```python
import functools
import math

import jax
import jax.numpy as jnp
from jax import lax
from jax.experimental import pallas as pl
from jax.experimental.pallas import tpu as pltpu

F32 = jnp.float32
BF16 = jnp.bfloat16
HIGHEST = lax.Precision.HIGHEST

DEPTH = 4
GRID_W = 64
N_MOD = 6
EPS = 1e-6
DEEPNORM_ALPHA = (2.0 * DEPTH) ** 0.25

GDN_HEADS = 4
GDN_DK = 128
GDN_DV = 128
GDN_CONV = 5
MLA_HEADS = 4
MLA_Q_LORA = 256
MLA_KV_LORA = 256
MLA_NOPE = 128
MLA_ROPE = 64
MLA_DV = 128
ROPE_THETA = 10000.0
NA_HEADS = 8
NA_DH = 64
NA_KH = 8
NA_KW = 16
HG_HEADS = 4
HG_DK = 128
HG_DV = 128
N_EXPERTS = 16
EC_CAPACITY = 2

CHUNK = 64
NA_GROUP = 4
PREP_CHUNKS = 12
SCAN_UNROLL = 4
SCATTER_GROUP = 8
LANES = 128
MLA_QK_PAD = 256
NEG_BIG = -1e30
VMEM_LIMIT = 56 * 1024 * 1024

_NT = (((1,), (1,)), ((), ()))


def _cparams(sem):
    return pltpu.CompilerParams(dimension_semantics=sem, vmem_limit_bytes=VMEM_LIMIT)


def _mm(a, b):
    return jnp.dot(a.astype(BF16), b.astype(BF16), preferred_element_type=F32)


def _mm_nt(a, b):
    return lax.dot_general(a.astype(BF16), b.astype(BF16), _NT, preferred_element_type=F32)


def _mm32(a, b):
    return jnp.dot(a, b, precision=HIGHEST, preferred_element_type=F32)


def _silu(x):
    return x * jax.nn.sigmoid(x)


def _rms(x):
    return x * lax.rsqrt(jnp.mean(x * x, -1, keepdims=True) + EPS)


def _ln(y, g, b):
    yc = y - jnp.mean(y, -1, keepdims=True)
    return yc * lax.rsqrt(jnp.mean(yc * yc, -1, keepdims=True) + EPS) * g + b


def _mod_kernel(c_ref, w_ref, b_ref, o_ref):
    o_ref[...] = _mm32(_silu(c_ref[...]), w_ref[...]) + b_ref[...]


def _modulation(cc, w_mod, b_mod):
    depth, d, n = w_mod.shape
    tn = n // 4
    return pl.pallas_call(
        _mod_kernel,
        grid=(depth, n // tn),
        in_specs=[pl.BlockSpec((cc.shape[0], d), lambda l, j: (0, 0)),
                  pl.BlockSpec((None, d, tn), lambda l, j: (l, 0, j)),
                  pl.BlockSpec((None, 1, tn), lambda l, j: (l, 0, j))],
        out_specs=pl.BlockSpec((None, cc.shape[0], tn), lambda l, j: (l, 0, j)),
        out_shape=jax.ShapeDtypeStruct((depth, cc.shape[0], n), F32),
        compiler_params=_cparams(("parallel", "parallel")),
    )(cc, w_mod, b_mod.reshape(depth, 1, n))


def _inproj_kernel(x_ref, sh_ref, sc_ref, w_ref, o_ref):
    u = x_ref[...] * (1.0 + sc_ref[...]) + sh_ref[...]
    o_ref[...] = jnp.dot(u.astype(BF16), w_ref[...], preferred_element_type=F32)


def _tab_spec(d, tps, nct):
    return pl.BlockSpec((None, None, 1, d), lambda i: (i // tps, jnp.where(i % tps >= nct, 1, 0), 0, 0))


def _inproj(xx, sh, sc, w, tm, T):
    B, L, D = xx.shape
    n = w.shape[1]
    tps, nct = L // tm, T // tm
    out = pl.pallas_call(
        _inproj_kernel,
        grid=(B * tps,),
        in_specs=[pl.BlockSpec((tm, D), lambda i: (i, 0)),
                  _tab_spec(D, tps, nct), _tab_spec(D, tps, nct),
                  pl.BlockSpec((D, n), lambda i: (0, 0))],
        out_specs=pl.BlockSpec((tm, n), lambda i: (i, 0)),
        out_shape=jax.ShapeDtypeStruct((B * L, n), F32),
        compiler_params=_cparams(("parallel",)),
    )(xx.reshape(B * L, D), sh, sc, w)
    return out.reshape(B, L, n)


def _scan_masks(d):
    ri = lax.broadcasted_iota(jnp.int32, (CHUNK, CHUNK), 0)
    ci = lax.broadcasted_iota(jnp.int32, (CHUNK, CHUNK), 1)
    if d == 0:
        return ci <= ri, ci < ri, ri <= ci, ci == ri
    return ci >= ri, ci > ri, ri >= ci, ci == ri


def _bwd_chunk(i, nc, n):
    return jnp.where(i < nc, nc - 1 - i, n - 1 - (i - nc))


def _gdn_kernel(nega_ref, dtb_ref, q_ref, k_ref, v_ref, z_ref, ab_ref, cwq_ref, cwk_ref, cwv_ref, gain_ref,
                o_ref,
                pad_ref, qn_ref, kn_ref, vn_ref, ncs_ref, kw_ref, ou_ref, qp_ref, el_ref, of_ref, ob_ref,
                *, T, L):
    h = pl.program_id(1)
    n_chunks = L // CHUNK
    nc = T // CHUNK
    prep = max(c for c in range(1, PREP_CHUNKS + 1) if n_chunks % c == 0)

    def conv_prep(src_ref, cw_ref, dst_ref, post):
        zero8 = jnp.zeros((8, LANES), F32)
        pad_ref[0:8, :] = zero8
        pad_ref[8:8 + T, :] = src_ref[0:T, :]
        pad_ref[8 + T:16 + T, :] = zero8
        pad_ref[16 + T:16 + L, :] = src_ref[T:L, :]
        pad_ref[16 + L:24 + L, :] = zero8
        cw = cw_ref[...]

        def body(n, carry):
            r0 = pl.multiple_of(n * CHUNK, CHUNK)
            p0 = r0 + jnp.where(r0 < T, 8, 16) - GDN_CONV // 2
            acc = jnp.zeros((CHUNK, LANES), F32)
            for j in range(GDN_CONV):
                acc = acc + pad_ref[pl.ds(p0 + j, CHUNK), :] * cw[j:j + 1, :]
            dst_ref[pl.ds(r0, CHUNK), :] = post(_silu(acc))
            return carry

        lax.fori_loop(0, n_chunks, body, 0, unroll=prep)

    def l2n(y):
        return y * lax.rsqrt(jnp.sum(y * y, -1, keepdims=True) + EPS)

    conv_prep(q_ref, cwq_ref, qn_ref, lambda y: l2n(y) * (GDN_DK ** -0.5))
    conv_prep(k_ref, cwk_ref, kn_ref, l2n)
    conv_prep(v_ref, cwv_ref, vn_ref, lambda y: y)

    lane = lax.broadcasted_iota(jnp.int32, (CHUNK, LANES), 1)

    def chunk_prep(i, carry):
        chains = []
        for c in range(prep):
            n = prep * i + c
            rows = pl.ds(pl.multiple_of(n * CHUNK, CHUNK), CHUNK)
            q, k, v, ab = qn_ref[rows, :], kn_ref[rows, :], vn_ref[rows, :], ab_ref[rows, :]
            kq = _mm_nt(jnp.concatenate([k, q], axis=0), k)
            for d in range(2):
                chains.append(dict(n=n, d=d, rows=rows, q=q, k=k, v=v, ab=ab, kk=kq[:CHUNK], qk=kq[CHUNK:]))
        for ch in chains:
            d = ch["d"]
            incl, strict, incl_t, eye = _scan_masks(d)
            a_col = jnp.sum(jnp.where(lane == 4 * d + h, ch["ab"], 0.0), axis=1, keepdims=True)
            b_col = jnp.sum(jnp.where(lane == 8 + 4 * d + h, ch["ab"], 0.0), axis=1, keepdims=True)
            g_col = nega_ref[4 * d + h] * jax.nn.softplus(a_col + dtb_ref[4 * d + h])
            beta = jax.nn.sigmoid(b_col)
            g_row = jnp.sum(jnp.where(eye, g_col, 0.0), axis=0, keepdims=True)
            gc_col = jnp.sum(jnp.where(incl, g_row, 0.0), axis=1, keepdims=True)
            gc_row = jnp.sum(jnp.where(incl_t, g_col, 0.0), axis=0, keepdims=True)
            g_tot = jnp.sum(g_col, axis=0, keepdims=True)
            ch["decay"] = jnp.where(incl, jnp.exp(jnp.where(incl, gc_col - gc_row, 0.0)), 0.0)
            ch["kb"] = ch["k"] * beta
            ch["vb"] = ch["v"] * beta
            ch["beta"] = beta
            ch["e_col"] = jnp.exp(gc_col)
            ch["kd"] = ch["k"] * jnp.exp(g_tot - gc_col)
            ch["e_tot"] = jnp.exp(g_tot)
        for ch in chains:
            _, strict, _, eye = _scan_masks(ch["d"])
            a = jnp.where(strict, ch["beta"] * ch["kk"] * ch["decay"], 0.0)
            ch["p"] = jnp.where(eye, 1.0, 0.0) - a
            ch["bp"] = a
        for _ in range(int(math.log2(CHUNK)) - 1):
            for ch in chains:
                ch["bp"] = _mm(ch["bp"], ch["bp"])
            for ch in chains:
                ch["p"] = ch["p"] + _mm(ch["p"], ch["bp"])
        for ch in chains:
            incl = _scan_masks(ch["d"])[0]
            ch["uw"] = _mm(ch["p"], jnp.concatenate([ch["vb"], ch["kb"] * ch["e_col"]], axis=1)).astype(BF16)
            ch["qk"] = jnp.where(incl, ch["qk"] * ch["decay"], 0.0)
        for ch in chains:
            d, n, rows = ch["d"], ch["n"], ch["rows"]
            kd_uw = _mm(ch["kd"].T, ch["uw"])
            qk_uw = _mm(ch["qk"], ch["uw"])
            ncs_ref[d * n_chunks + n] = kd_uw[:, :GDN_DV]
            kw_ref[d * n_chunks + n] = kd_uw[:, GDN_DV:].astype(BF16)
            ou_ref[d, rows, :] = qk_uw[:, :GDN_DV]
            qp_ref[d, rows, :] = (ch["q"] * ch["e_col"] - qk_uw[:, GDN_DV:]).astype(BF16)
            el_ref[d * n_chunks + n] = jnp.broadcast_to(ch["e_tot"], (8, LANES))
        return carry

    lax.fori_loop(0, n_chunks // prep, chunk_prep, 0)

    def step(d, n, s):
        rows = pl.ds(pl.multiple_of(n * CHUNK, CHUNK), CHUNK)
        sb = s.astype(BF16)
        o = jnp.dot(qp_ref[d, rows, :], sb, preferred_element_type=F32) + ou_ref[d, rows, :]
        s = (s * el_ref[d * n_chunks + n][0:1, :] + ncs_ref[d * n_chunks + n]
             - jnp.dot(kw_ref[d * n_chunks + n], sb, preferred_element_type=F32))
        return s, o, rows

    def scan_body(i, carry):
        sf, sb = carry
        sf, o_f, rows_f = step(0, i, sf)
        sb, o_b, rows_b = step(1, _bwd_chunk(i, nc, n_chunks), sb)
        of_ref[rows_f, :] = o_f
        ob_ref[rows_b, :] = o_b
        return sf, sb

    zero = jnp.zeros((GDN_DK, GDN_DV), F32)
    lax.fori_loop(0, n_chunks, scan_body, (zero, zero), unroll=SCAN_UNROLL)

    o = of_ref[...] + ob_ref[...]
    o_ref[...] = (_rms(o) * gain_ref[...] * _silu(z_ref[...])).astype(BF16)


def _gdn(p, conv_w, neg_a, dt_bias, gain, T):
    B, L, _ = p.shape
    H = GDN_HEADS
    n_chunks = L // CHUNK
    col = lambda off: pl.BlockSpec((None, L, LANES), lambda b, h: (b, 0, off + h))
    cw = lambda off: pl.BlockSpec((GDN_CONV, LANES), lambda b, h: (0, off + h))
    smem = pl.BlockSpec(memory_space=pltpu.SMEM)
    return pl.pallas_call(
        functools.partial(_gdn_kernel, T=T, L=L),
        grid=(B, H),
        in_specs=[smem, smem, col(0), col(H), col(2 * H), col(3 * H),
                  pl.BlockSpec((None, L, LANES), lambda b, h: (b, 0, 21)),
                  cw(0), cw(H), cw(2 * H),
                  pl.BlockSpec((1, LANES), lambda b, h: (0, 0))],
        out_specs=pl.BlockSpec((None, L, LANES), lambda b, h: (b, 0, h)),
        out_shape=jax.ShapeDtypeStruct((B, L, H * GDN_DV), BF16),
        scratch_shapes=[pltpu.VMEM((L + 24, LANES), F32),
                        pltpu.VMEM((L, LANES), F32), pltpu.VMEM((L, LANES), F32), pltpu.VMEM((L, LANES), F32),
                        pltpu.VMEM((2 * n_chunks, GDN_DK, GDN_DV), F32),
                        pltpu.VMEM((2 * n_chunks, GDN_DK, GDN_DK), BF16),
                        pltpu.VMEM((2, L, LANES), F32), pltpu.VMEM((2, L, LANES), BF16),
                        pltpu.VMEM((2 * n_chunks, 8, LANES), F32),
                        pltpu.VMEM((L, LANES), F32), pltpu.VMEM((L, LANES), F32)],
        compiler_params=_cparams(("parallel", "parallel")),
    )(neg_a, dt_bias, p, p, p, p, p, conv_w, conv_w, conv_w, gain.reshape(1, LANES))


def _mlaproj_kernel(ql_ref, kvl_ref, kr_ref, cs_ref, qg_ref, kvg_ref, wq_ref, wkv_ref, q_ref, k_ref, v_ref):
    q = jnp.dot((_rms(ql_ref[...]) * qg_ref[...]).astype(BF16), wq_ref[...], preferred_element_type=F32)
    kv = jnp.dot((_rms(kvl_ref[...]) * kvg_ref[...]).astype(BF16), wkv_ref[...], preferred_element_type=F32)
    cs = cs_ref[...]
    lane = lax.broadcasted_iota(jnp.int32, cs.shape, 1)

    def rope(blk):
        t = blk * cs
        return jnp.where(lane < MLA_ROPE, t + pltpu.roll(t, MLA_ROPE, 1), 0.0)

    kr = rope(kr_ref[...]).astype(BF16)
    for h in range(MLA_HEADS):
        o = h * MLA_QK_PAD
        q_ref[:, o:o + MLA_NOPE] = q[:, o:o + MLA_NOPE].astype(BF16)
        q_ref[:, o + MLA_NOPE:o + MLA_QK_PAD] = rope(q[:, o + MLA_NOPE:o + MLA_QK_PAD]).astype(BF16)
        k_ref[:, o:o + MLA_NOPE] = kv[:, o:o + MLA_NOPE].astype(BF16)
        k_ref[:, o + MLA_NOPE:o + MLA_QK_PAD] = kr
        v_ref[:, h * MLA_DV:(h + 1) * MLA_DV] = kv[:, o + MLA_NOPE:o + MLA_QK_PAD].astype(BF16)


def _mlaproj(p, cs, q_gain, kv_gain, wq, wkv, tm):
    B, L, n = p.shape
    tps = L // tm
    p2 = p.reshape(B * L, n)
    hq = MLA_HEADS * MLA_QK_PAD
    q, k, v = pl.pallas_call(
        _mlaproj_kernel,
        grid=(B * tps,),
        in_specs=[pl.BlockSpec((tm, MLA_Q_LORA), lambda i: (i, 8)),
                  pl.BlockSpec((tm, MLA_KV_LORA), lambda i: (i, 9)),
                  pl.BlockSpec((tm, LANES), lambda i: (i, 20)),
                  pl.BlockSpec((tm, LANES), lambda i: (i % tps, 0)),
                  pl.BlockSpec((1, MLA_Q_LORA), lambda i: (0, 0)),
                  pl.BlockSpec((1, MLA_KV_LORA), lambda i: (0, 0)),
                  pl.BlockSpec((MLA_Q_LORA, hq), lambda i: (0, 0)),
                  pl.BlockSpec((MLA_KV_LORA, hq), lambda i: (0, 0))],
        out_specs=[pl.BlockSpec((tm, hq), lambda i: (i, 0)),
                   pl.BlockSpec((tm, hq), lambda i: (i, 0)),
                   pl.BlockSpec((tm, MLA_HEADS * MLA_DV), lambda i: (i, 0))],
        out_shape=[jax.ShapeDtypeStruct((B * L, hq), BF16),
                   jax.ShapeDtypeStruct((B * L, hq), BF16),
                   jax.ShapeDtypeStruct((B * L, MLA_HEADS * MLA_DV), BF16)],
        compiler_params=_cparams(("parallel",)),
    )(p2, p2, p2, cs, q_gain.reshape(1, -1), kv_gain.reshape(1, -1), wq, wkv)
    return q.reshape(B, L, hq), k.reshape(B, L, hq), v.reshape(B, L, -1)


def _mla_attn_kernel(q_ref, k_ref, v_ref, o_ref, *, T, nct, scale):
    qt = pl.program_id(2)
    q = q_ref[...]

    def attend(n):
        s = lax.dot_general(q, k_ref[0:n, :], _NT, preferred_element_type=F32) * scale
        p = jnp.exp(s - jnp.max(s, axis=-1, keepdims=True))
        l = jnp.sum(p, axis=-1, keepdims=True)
        o_ref[...] = (jnp.dot(p.astype(BF16), v_ref[0:n, :], preferred_element_type=F32) / l).astype(BF16)

    @pl.when(qt < nct)
    def _():
        attend(T)

    @pl.when(qt >= nct)
    def _():
        attend(k_ref.shape[0])


def _mla_attn(q, k, v, T, tq):
    B, L, _ = q.shape
    H = MLA_HEADS
    scale = (MLA_NOPE + MLA_ROPE) ** -0.5
    return pl.pallas_call(
        functools.partial(_mla_attn_kernel, T=T, nct=T // tq, scale=scale),
        grid=(B, H, L // tq),
        in_specs=[pl.BlockSpec((None, tq, MLA_QK_PAD), lambda b, h, i: (b, i, h)),
                  pl.BlockSpec((None, L, MLA_QK_PAD), lambda b, h, i: (b, 0, h)),
                  pl.BlockSpec((None, L, MLA_DV), lambda b, h, i: (b, 0, h))],
        out_specs=pl.BlockSpec((None, tq, MLA_DV), lambda b, h, i: (b, i, h)),
        out_shape=jax.ShapeDtypeStruct((B, L, H * MLA_DV), BF16),
        compiler_params=_cparams(("parallel", "parallel", "arbitrary")),
    )(q, k, v)


def _outproj_kernel(a_ref, b_ref, w_ref, x_ref, gate_ref, lng_ref, lnb_ref, sh_ref, sc_ref, wr_ref,
                    xo_ref, aff_ref):
    ka = a_ref.shape[1]
    o = (jnp.dot(a_ref[...], w_ref[0:ka, :], preferred_element_type=F32)
         + jnp.dot(b_ref[...], w_ref[ka:, :], preferred_element_type=F32))
    xn = _ln(DEEPNORM_ALPHA * x_ref[...] + gate_ref[...] * o, lng_ref[...], lnb_ref[...])
    xo_ref[...] = xn
    hmod = xn * (1.0 + sc_ref[...]) + sh_ref[...]
    logits = _mm_nt(wr_ref[...], hmod)
    e = jnp.exp(logits - jnp.max(logits, axis=0, keepdims=True))
    aff_ref[...] = e / jnp.sum(e, axis=0, keepdims=True)


def _outproj(a, b, w, xx, gate, ln_g, ln_b, sh, sc, wr_t, tm, T):
    B, L, D = xx.shape
    tps, nct = L // tm, T // tm
    ka, kb = a.shape[-1], b.shape[-1]
    row = lambda n: pl.BlockSpec((1, n), lambda i: (0, 0))
    xo, aff = pl.pallas_call(
        _outproj_kernel,
        grid=(B * tps,),
        in_specs=[pl.BlockSpec((tm, ka), lambda i: (i, 0)),
                  pl.BlockSpec((tm, kb), lambda i: (i, 0)),
                  pl.BlockSpec((ka + kb, D), lambda i: (0, 0)),
                  pl.BlockSpec((tm, D), lambda i: (i, 0)),
                  _tab_spec(D, tps, nct), row(D), row(D),
                  _tab_spec(D, tps, nct), _tab_spec(D, tps, nct),
                  pl.BlockSpec((N_EXPERTS, D), lambda i: (0, 0))],
        out_specs=[pl.BlockSpec((tm, D), lambda i: (i, 0)),
                   pl.BlockSpec((None, N_EXPERTS, tm), lambda i: (i // tps, 0, i % tps))],
        out_shape=[jax.ShapeDtypeStruct((B * L, D), F32),
                   jax.ShapeDtypeStruct((B, N_EXPERTS, L), F32)],
        compiler_params=_cparams(("parallel",)),
    )(a.reshape(B * L, ka), b.reshape(B * L, kb), w, xx.reshape(B * L, D), gate,
      ln_g.reshape(1, D), ln_b.reshape(1, D), sh, sc, wr_t)
    return xo.reshape(B, L, D), aff


def _topk_kernel(aff_ref, idx_ref, gate_ref, *, T, cap_c, cap_l):
    ri = lax.broadcasted_iota(jnp.int32, (LANES, LANES), 0)
    ci = lax.broadcasted_iota(jnp.int32, (LANES, LANES), 1)
    tri = jnp.where(ri <= ci, 1.0, 0.0).astype(BF16)

    def prefix_incl(x):
        carry = jnp.zeros((x.shape[0], 1), F32)
        outs = []
        for j in range(x.shape[1] // LANES):
            pj = jnp.dot(x[:, j * LANES:(j + 1) * LANES].astype(BF16), tri, preferred_element_type=F32) + carry
            outs.append(pj)
            carry = pj[:, LANES - 1:LANES]
        return jnp.concatenate(outs, axis=1)

    def thresholds(segs):
        half = lambda a, b: a + lax.shift_right_logical(b - a, 1)

        def ge_cap(a, cap, bits):
            cnt = jnp.sum(jnp.where(a >= pltpu.bitcast(bits, F32), 1.0, 0.0), axis=1, keepdims=True)
            return cnt >= float(cap)

        def bisect2(_, state):
            out = []
            for (a, cap), (lo, hi) in zip(segs, state):
                mid = half(lo, hi)
                ml, mh = half(lo, mid), half(mid, hi)
                ge_m, ge_l, ge_h = ge_cap(a, cap, mid), ge_cap(a, cap, ml), ge_cap(a, cap, mh)
                out.append((jnp.where(ge_m, jnp.where(ge_h, mh, mid), jnp.where(ge_l, ml, lo)),
                            jnp.where(ge_m, jnp.where(ge_h, hi, mh), jnp.where(ge_l, mid, ml))))
            return tuple(out)

        ne = segs[0][0].shape[0]
        init = tuple((jnp.zeros((ne, 1), jnp.int32), jnp.full((ne, 1), 0x7F800000, jnp.int32)) for _ in segs)
        return [pltpu.bitcast(lo, F32) for lo, _ in lax.fori_loop(0, 16, bisect2, init)]

    def select(aff, thr, cap, base, row0):
        ne, n = aff.shape
        gt = aff > thr
        eqf = jnp.where(aff == thr, 1.0, 0.0)
        need = float(cap) - jnp.sum(jnp.where(gt, 1.0, 0.0), axis=1, keepdims=True)
        eq_before = prefix_incl(eqf) - eqf
        self_ = jnp.where(gt, 1.0, jnp.where(eq_before < need, eqf, 0.0))
        cnt = prefix_incl(self_)
        slot = lax.broadcasted_iota(jnp.int32, (cap, 1), 0).astype(F32)
        lane = lax.broadcasted_iota(jnp.int32, (cap, LANES), 1)
        idx_acc = jnp.zeros((cap, LANES), F32)
        gate_acc = jnp.zeros((cap, LANES), F32)
        for e in range(ne):
            ce, picked = cnt[e:e + 1, :], self_[e:e + 1, :] * aff[e:e + 1, :]
            idx_e = jnp.sum(jnp.where(ce <= slot, 1.0, 0.0), axis=1, keepdims=True)
            gate_e = jnp.sum(jnp.where(ce == slot + 1.0, picked, 0.0), axis=1, keepdims=True)
            idx_acc = jnp.where(lane == e, idx_e + float(base), idx_acc)
            gate_acc = jnp.where(lane == e, gate_e, gate_acc)
        idx_ref[row0:row0 + cap, :] = idx_acc.astype(jnp.int32)
        gate_ref[row0:row0 + cap, :] = gate_acc

    aff = aff_ref[...]
    aff_c, aff_l = aff[:, 0:T], aff[:, T:]
    thr_c, thr_l = thresholds([(aff_c, cap_c), (aff_l, cap_l)])
    select(aff_c, thr_c, cap_c, 0, 0)
    select(aff_l, thr_l, cap_l, T, cap_c)


def _topk(aff, T):
    B, E, L = aff.shape
    cap_c = EC_CAPACITY * T // E
    cap_l = EC_CAPACITY * (L - T) // E
    cap = cap_c + cap_l
    idx, gate = pl.pallas_call(
        functools.partial(_topk_kernel, T=T, cap_c=cap_c, cap_l=cap_l),
        grid=(B,),
        in_specs=[pl.BlockSpec((None, E, L), lambda b: (b, 0, 0))],
        out_specs=[pl.BlockSpec((None, cap, LANES), lambda b: (b, 0, 0)),
                   pl.BlockSpec((None, cap, LANES), lambda b: (b, 0, 0))],
        out_shape=[jax.ShapeDtypeStruct((B, cap, LANES), jnp.int32),
                   jax.ShapeDtypeStruct((B, cap, LANES), F32)],
        compiler_params=_cparams(("parallel",)),
    )(aff)
    idx = jnp.swapaxes(idx[:, :, :E], 1, 2).reshape(B * E * cap)
    gate = jnp.swapaxes(gate[:, :, :E], 1, 2).reshape(B * E * cap)
    return idx, gate, cap_c, cap


def _gather_kernel(idx_ref, x_ref, sh_ref, sc_ref, o_ref, stage_ref, *, cap_c, cap):
    base = (pl.program_id(0) * pl.num_programs(1) + pl.program_id(1)) * cap

    def body(s, carry):
        stage_ref[pl.ds(s, 1), :] = x_ref[pl.ds(idx_ref[base + s], 1), :]
        return carry

    lax.fori_loop(0, cap, body, 0, unroll=8)
    o_ref[0:cap_c, :] = (stage_ref[0:cap_c, :] * (1.0 + sc_ref[0]) + sh_ref[0]).astype(BF16)
    o_ref[cap_c:, :] = (stage_ref[cap_c:, :] * (1.0 + sc_ref[1]) + sh_ref[1]).astype(BF16)


def _gather(idx, xx, sh, sc, cap_c, cap):
    B, L, D = xx.shape
    E = N_EXPERTS
    tab = pl.BlockSpec((None, 2, 1, D), lambda b, e, idx: (b, 0, 0, 0))
    return pl.pallas_call(
        functools.partial(_gather_kernel, cap_c=cap_c, cap=cap),
        grid_spec=pltpu.PrefetchScalarGridSpec(
            num_scalar_prefetch=1, grid=(B, E),
            in_specs=[pl.BlockSpec((None, L, D), lambda b, e, idx: (b, 0, 0)), tab, tab],
            out_specs=pl.BlockSpec((None, None, cap, D), lambda b, e, idx: (e, b, 0, 0)),
            scratch_shapes=[pltpu.VMEM((cap, D), F32)]),
        out_shape=jax.ShapeDtypeStruct((E, B, cap, D), BF16),
        compiler_params=_cparams(("parallel", "arbitrary")),
    )(idx, xx, sh, sc)


def _expert_kernel(x_ref, wg_ref, wu_ref, wd_ref, o_ref, wg_s, wu_s, wd_s):
    @pl.when(pl.program_id(1) == 0)
    def _():
        wg_s[...] = wg_ref[...].astype(BF16)
        wu_s[...] = wu_ref[...].astype(BF16)
        wd_s[...] = wd_ref[...].astype(BF16)

    x = x_ref[...]
    g = jnp.dot(x, wg_s[...], preferred_element_type=F32)
    u = jnp.dot(x, wu_s[...], preferred_element_type=F32)
    hid = (_silu(g) * u).astype(BF16)
    o_ref[...] = jnp.dot(hid, wd_s[...], preferred_element_type=F32)


def _experts(xe, w_gate, w_up, w_down, layer, tm):
    E, M, D = xe.shape
    F = w_gate.shape[-1]
    return pl.pallas_call(
        _expert_kernel,
        grid=(E, M // tm),
        in_specs=[pl.BlockSpec((None, tm, D), lambda e, m: (e, m, 0)),
                  pl.BlockSpec((None, None, D, F), lambda e, m: (layer, e, 0, 0)),
                  pl.BlockSpec((None, None, D, F), lambda e, m: (layer, e, 0, 0)),
                  pl.BlockSpec((None, None, F, D), lambda e, m: (layer, e, 0, 0))],
        out_specs=pl.BlockSpec((None, tm, D), lambda e, m: (e, m, 0)),
        out_shape=jax.ShapeDtypeStruct((E, M, D), F32),
        scratch_shapes=[pltpu.VMEM((D, F), BF16), pltpu.VMEM((D, F), BF16), pltpu.VMEM((F, D), BF16)],
        compiler_params=_cparams(("arbitrary", "arbitrary")),
    )(xe, w_gate, w_up, w_down)


def _combine_kernel(idx_ref, gate_ref, ye_ref, x_ref, gt_ref, lng_ref, lnb_ref, o_ref, *, cap, T, tm):
    e = pl.program_id(1)
    n_e = pl.num_programs(1)
    base = (pl.program_id(0) * n_e + e) * cap

    @pl.when(e == 0)
    def _():
        o_ref[...] = jnp.zeros(o_ref.shape, F32)

    def body(i, carry):
        s0 = base + i * SCATTER_GROUP
        rs = [idx_ref[s0 + j] for j in range(SCATTER_GROUP)]
        new = [o_ref[pl.ds(rs[j], 1), :] + gate_ref[s0 + j] * ye_ref[pl.ds(i * SCATTER_GROUP + j, 1), :]
               for j in range(SCATTER_GROUP)]
        for j in range(SCATTER_GROUP):
            o_ref[pl.ds(rs[j], 1), :] = new[j]
        return carry

    lax.fori_loop(0, cap // SCATTER_GROUP, body, 0)

    @pl.when(e == n_e - 1)
    def _():
        def tile(t, carry):
            rows = pl.ds(pl.multiple_of(t * tm, tm), tm)
            gt = jnp.where(t < T // tm, gt_ref[0], gt_ref[1])
            o_ref[rows, :] = _ln(DEEPNORM_ALPHA * x_ref[rows, :] + gt * o_ref[rows, :], lng_ref[...], lnb_ref[...])
            return carry

        lax.fori_loop(0, o_ref.shape[0] // tm, tile, 0)


def _combine(idx, gate, ye, xx, gt, ln_g, ln_b, tm, T):
    E, B, cap, D = ye.shape
    L = xx.shape[1]
    row = pl.BlockSpec((1, D), lambda b, e, i, g: (0, 0))
    return pl.pallas_call(
        functools.partial(_combine_kernel, cap=cap, T=T, tm=tm),
        grid_spec=pltpu.PrefetchScalarGridSpec(
            num_scalar_prefetch=2, grid=(B, E),
            in_specs=[pl.BlockSpec((None, None, cap, D), lambda b, e, i, g: (e, b, 0, 0)),
                      pl.BlockSpec((None, L, D), lambda b, e, i, g: (b, 0, 0)),
                      pl.BlockSpec((None, 2, 1, D), lambda b, e, i, g: (b, 0, 0, 0)),
                      row, row],
            out_specs=pl.BlockSpec((None, L, D), lambda b, e, i, g: (b, 0, 0))),
        out_shape=jax.ShapeDtypeStruct((B, L, D), F32),
        compiler_params=_cparams(("parallel", "arbitrary")),
    )(idx, gate, ye, xx, gt, ln_g.reshape(1, D), ln_b.reshape(1, D))


def _na_plan(rows, kh):
    band = min(kh + NA_GROUP - 1, rows)
    plan, keys = [], []
    for g in range(rows // NA_GROUP):
        q0 = NA_GROUP * g
        win = [min(max(q0 + a - kh // 2, 0), rows - kh) for a in range(NA_GROUP)]
        ub = min(win[0], rows - band)
        key = (ub - q0, tuple(w - q0 for w in win))
        if key not in keys:
            keys.append(key)
        plan.append((ub, keys.index(key)))
    return band, plan, keys


def _na_kernel(plan_ref, q_ref, k_ref, v_ref, bias_ref, o_ref, *, T, nct, band, scale):
    i = pl.program_id(2)
    q = q_ref[...]
    lane = lax.broadcasted_iota(jnp.int32, q.shape, 1)
    kc = k_ref[0:T, :].astype(BF16)
    vc = v_ref[0:T, :].astype(BF16)
    qh = [jnp.where(lane < NA_DH, q, 0.0).astype(BF16), jnp.where(lane >= NA_DH, q, 0.0).astype(BF16)]

    @pl.when(i < nct)
    def _():
        outs = []
        for j in range(2):
            s_c = lax.dot_general(qh[j], kc, _NT, preferred_element_type=F32) * scale
            p_c = jnp.exp(s_c - jnp.max(s_c, axis=-1, keepdims=True))
            outs.append(jnp.dot(p_c.astype(BF16), vc, preferred_element_type=F32)
                        / jnp.sum(p_c, axis=-1, keepdims=True))
        o_ref[...] = jnp.where(lane < NA_DH, outs[0], outs[1]).astype(BF16)

    @pl.when(i >= nct)
    def _():
        t0 = pl.multiple_of(T + plan_ref[0, i - nct] * GRID_W, GRID_W)
        kb = k_ref[pl.ds(t0, band * GRID_W), :].astype(BF16)
        vb = v_ref[pl.ds(t0, band * GRID_W), :].astype(BF16)
        outs = []
        for j in range(2):
            s_w = lax.dot_general(qh[j], kb, _NT, preferred_element_type=F32) * scale + bias_ref[j]
            s_c = lax.dot_general(qh[j], kc, _NT, preferred_element_type=F32) * scale
            m = jnp.maximum(jnp.max(s_w, axis=-1, keepdims=True), jnp.max(s_c, axis=-1, keepdims=True))
            p_w = jnp.exp(s_w - m)
            p_c = jnp.exp(s_c - m)
            l = jnp.sum(p_w, axis=-1, keepdims=True) + jnp.sum(p_c, axis=-1, keepdims=True)
            outs.append((jnp.dot(p_w.astype(BF16), vb, preferred_element_type=F32)
                         + jnp.dot(p_c.astype(BF16), vc, preferred_element_type=F32)) / l)
        o_ref[...] = jnp.where(lane < NA_DH, outs[0], outs[1]).astype(BF16)


def _na_bias_tables(rpb, band, keys, kh):
    w = jnp.arange(GRID_W)
    c0 = jnp.clip(w - NA_KW // 2, 0, GRID_W - NA_KW)
    col_ok = (w[None, :] >= c0[:, None]) & (w[None, :] < c0[:, None] + NA_KW)
    col_off = jnp.clip(w[None, :] - w[:, None] + (NA_KW - 1), 0, 2 * NA_KW - 2)
    onehot = (col_off[:, :, None] == jnp.arange(2 * NA_KW - 1)).astype(F32)
    base = jnp.einsum('hrm,wcm->hrwc', rpb.astype(F32), onehot, precision=HIGHEST)
    base = jnp.where(col_ok[None, None], base, NEG_BIG)
    masked = jnp.full((NA_HEADS, GRID_W, GRID_W), NEG_BIG, F32)
    tabs = []
    for ub_off, win_offs in keys:
        per_row = []
        for a in range(NA_GROUP):
            blocks = []
            for u in range(band):
                rel = ub_off + u
                inside = win_offs[a] <= rel < win_offs[a] + kh
                blocks.append(base[:, rel - a + NA_KH - 1] if inside else masked)
            per_row.append(jnp.concatenate(blocks, axis=-1))
        tabs.append(jnp.concatenate(per_row, axis=1))
    return jnp.stack(tabs)


def _na(p, rpb, T):
    B, L, _ = p.shape
    rows = (L - T) // GRID_W
    kh = min(NA_KH, rows)
    tq = NA_GROUP * GRID_W
    assert T % tq == 0 and rows % NA_GROUP == 0
    nct = T // tq
    hp = NA_HEADS // 2
    band, plan, keys = _na_plan(rows, kh)
    bias = _na_bias_tables(rpb, band, keys, kh)
    plan = jnp.asarray(plan, jnp.int32).T
    return pl.pallas_call(
        functools.partial(_na_kernel, T=T, nct=nct, band=band, scale=NA_DH ** -0.5),
        grid_spec=pltpu.PrefetchScalarGridSpec(
            num_scalar_prefetch=1, grid=(B, hp, nct + rows // NA_GROUP),
            in_specs=[pl.BlockSpec((None, tq, LANES), lambda b, h, i, pr: (b, i, h)),
                      pl.BlockSpec((None, L, LANES), lambda b, h, i, pr: (b, 0, hp + h)),
                      pl.BlockSpec((None, L, LANES), lambda b, h, i, pr: (b, 0, 2 * hp + h)),
                      pl.BlockSpec((None, 2, tq, band * GRID_W),
                                   lambda b, h, i, pr: (pr[1, jnp.maximum(i - nct, 0)], h, 0, 0))],
            out_specs=pl.BlockSpec((None, tq, LANES), lambda b, h, i, pr: (b, i, h))),
        out_shape=jax.ShapeDtypeStruct((B, L, NA_HEADS * NA_DH), BF16),
        compiler_params=_cparams(("parallel", "parallel", "arbitrary")),
    )(plan, p, p, p, bias)


def _hgrn_kernel(q_ref, zf_ref, zb_ref, v_ref, gz_ref, llb_ref, l1m_ref, oml_ref, gain_ref,
                 o_ref,
                 qg_ref, ut_ref, oi_ref, el_ref, of_ref, ob_ref, *, T, L):
    n_chunks = L // CHUNK
    nc = T // CHUNK
    ri = lax.broadcasted_iota(jnp.int32, (CHUNK, CHUNK), 0)
    ci = lax.broadcasted_iota(jnp.int32, (CHUNK, CHUNK), 1)
    row = lax.broadcasted_iota(jnp.int32, (CHUNK, LANES), 0)
    eye = ri == ci
    top_bit = 31 - lax.clz(ri ^ ci)
    split_at = [jnp.where(ci < ri, top_bit, -1), jnp.where(ci > ri, top_bit, -1)]

    def shifted(x, k):
        return x if k == 0 else pltpu.roll(x, (-k) % CHUNK, 0)

    def scan_cumsum(x, d):
        k = 1
        while k < CHUNK:
            if d == 0:
                x = x + jnp.where(row >= k, shifted(x, -k), 0.0)
            else:
                x = x + jnp.where(row < CHUNK - k, shifted(x, k), 0.0)
            k *= 2
        return x

    def block_ref(g, size, r_in):
        if size >= 8:
            g3 = g.reshape(CHUNK // size, size, LANES)
            return jnp.broadcast_to(g3[:, r_in:r_in + 1, :], g3.shape).reshape(CHUNK, LANES)
        pos = row & (size - 1)
        out = shifted(g, r_in)
        for m in range(1, size):
            out = jnp.where(pos == m, shifted(g, r_in - m), out)
        return out

    def chunk_prep(n, carry):
        rows = pl.ds(pl.multiple_of(n * CHUNK, CHUNK), CHUNK)
        qs = _silu(q_ref[rows, :])
        v = v_ref[rows, :]
        vt = v.T
        for d in range(2):
            z = (zf_ref if d == 0 else zb_ref)[rows, :]
            t = jnp.exp(-jnp.abs(z))
            log_sig = jnp.minimum(z, 0.0) - jnp.log1p(t)
            logf = jnp.logaddexp(llb_ref[d:d + 1, :], l1m_ref[d:d + 1, :] + log_sig)
            kk = oml_ref[d:d + 1, :] * (jnp.where(z > 0, t, 1.0) / (1.0 + t))
            g = scan_cumsum(logf, d)
            g_tot = g[CHUNK - 1:CHUNK, :] if d == 0 else g[0:1, :]
            a = jnp.where(eye, jnp.sum(qs * kk, axis=1, keepdims=True), 0.0)
            half = CHUNK // 2
            while half >= 1:
                size = 2 * half
                e = jnp.exp(-jnp.abs(g - block_ref(g, size, half if d == 0 else half - 1)))
                a = jnp.where(split_at[d] == half.bit_length() - 1, _mm_nt(qs * e, kk * e), a)
                half //= 2
            oi_ref[d, rows, :] = _mm(a, v)
            qg_ref[d, rows, :] = (qs * jnp.exp(g)).astype(BF16)
            ut_ref[d * n_chunks + n] = _mm(vt, kk * jnp.exp(g_tot - g))
            el_ref[d * n_chunks + n] = jnp.broadcast_to(jnp.exp(g_tot), (8, LANES))
        return carry

    lax.fori_loop(0, n_chunks, chunk_prep, 0, unroll=6)

    def step(d, n, st):
        rows = pl.ds(pl.multiple_of(n * CHUNK, CHUNK), CHUNK)
        o = _mm_nt(qg_ref[d, rows, :], st) + oi_ref[d, rows, :]
        st = st * el_ref[d * n_chunks + n][0:1, :] + ut_ref[d * n_chunks + n]
        return st, o, rows

    def scan_body(i, carry):
        sf, sb = carry
        sf, o_f, rows_f = step(0, i, sf)
        sb, o_b, rows_b = step(1, _bwd_chunk(i, nc, n_chunks), sb)
        of_ref[rows_f, :] = o_f
        ob_ref[rows_b, :] = o_b
        return sf, sb

    zero = jnp.zeros((HG_DV, HG_DK), F32)
    lax.fori_loop(0, n_chunks, scan_body, (zero, zero), unroll=SCAN_UNROLL)
    o = of_ref[...] + ob_ref[...]
    o_ref[...] = (_rms(o) * gain_ref[...] * _silu(gz_ref[...])).astype(BF16)


def _hgrn(p, llb, l1m, oml, gain, T):
    B, L, _ = p.shape
    H = HG_HEADS
    n_chunks = L // CHUNK
    col = lambda off: pl.BlockSpec((None, L, LANES), lambda b, h: (b, 0, off + h))
    lbs = pl.BlockSpec((2, LANES), lambda b, h: (0, h))
    return pl.pallas_call(
        functools.partial(_hgrn_kernel, T=T, L=L),
        grid=(B, H),
        in_specs=[col(12), col(16), col(20), col(24), col(28), lbs, lbs, lbs,
                  pl.BlockSpec((1, LANES), lambda b, h: (0, 0))],
        out_specs=pl.BlockSpec((None, L, LANES), lambda b, h: (b, 0, h)),
        out_shape=jax.ShapeDtypeStruct((B, L, H * HG_DV), BF16),
        scratch_shapes=[pltpu.VMEM((2, L, LANES), BF16),
                        pltpu.VMEM((2 * n_chunks, HG_DV, HG_DK), F32),
                        pltpu.VMEM((2, L, LANES), F32),
                        pltpu.VMEM((2 * n_chunks, 8, LANES), F32),
                        pltpu.VMEM((L, LANES), F32), pltpu.VMEM((L, LANES), F32)],
        compiler_params=_cparams(("parallel", "parallel")),
    )(p, p, p, p, p, llb, l1m, oml, gain.reshape(1, LANES))


def _rot_cols(w):
    q = MLA_ROPE // 4
    return jnp.concatenate([-w[..., q:2 * q], w[..., 0:q], -w[..., 3 * q:4 * q], w[..., 2 * q:3 * q]], -1)


def _even_w_in(w):
    d = w.shape[0]
    qkv_z = GDN_HEADS * (2 * GDN_DK + GDN_DV) + GDN_HEADS * GDN_DV
    ab = w[:, qkv_z:qkv_z + 4 * GDN_HEADS]
    o = qkv_z + 4 * GDN_HEADS
    lora = w[:, o:o + MLA_Q_LORA + MLA_KV_LORA]
    kr = w[:, o + MLA_Q_LORA + MLA_KV_LORA:]
    pad = jnp.zeros((d, LANES - 4 * GDN_HEADS), w.dtype)
    return jnp.concatenate([w[:, :qkv_z], lora, kr, _rot_cols(kr), ab, pad], axis=1).astype(BF16)


def _mla_wq(w):
    w = w.reshape(w.shape[0], MLA_HEADS, MLA_NOPE + MLA_ROPE)
    rope = w[..., MLA_NOPE:]
    return jnp.concatenate([w[..., :MLA_NOPE], rope, _rot_cols(rope)], -1).reshape(w.shape[0], -1).astype(BF16)


def _rope_table(T, S):
    half = MLA_ROPE // 4
    inv = ROPE_THETA ** (-jnp.arange(half, dtype=F32) / half)
    pos = jnp.arange(S)
    ang_r = (pos // GRID_W).astype(F32)[:, None] * inv
    ang_c = (pos % GRID_W).astype(F32)[:, None] * inv
    ang = jnp.concatenate([ang_r, ang_r, ang_c, ang_c], -1)
    lat = jnp.concatenate([jnp.cos(ang), jnp.sin(ang)], -1)
    ctx = jnp.concatenate([jnp.ones((T, MLA_ROPE), F32), jnp.zeros((T, MLA_ROPE), F32)], -1)
    return jnp.concatenate([ctx, lat], 0)


def kernel(x, c, ctx, c_ctx, w_mod, b_mod, ln_g, ln_b, even_w_in, gdn_conv, gdn_a_log, gdn_dt_bias, gdn_norm,
           mla_q_norm, mla_kv_norm, mla_w_uq, mla_w_ukv, even_w_out, odd_w_in, na_rpb, hg_lb, hg_norm, odd_w_out,
           moe_router, moe_w_gate, moe_w_up, moe_w_down):
    B, S, D = x.shape
    T = ctx.shape[1]
    L = T + S
    tm = min(256, T)
    depth = w_mod.shape[0]

    xx = jnp.concatenate([ctx, x], axis=1)
    cc = jnp.concatenate([c, c_ctx[None, :], jnp.zeros((16 - B - 1, D), F32)], axis=0)
    mods = _modulation(cc, w_mod, b_mod)
    mods = mods.reshape(depth, 16, N_MOD, D)
    tabs = jnp.stack([jnp.broadcast_to(mods[:, B:B + 1], (depth, B, N_MOD, D)), mods[:, :B]], axis=2)
    tabs = jnp.transpose(tabs, (0, 3, 1, 2, 4))[:, :, :, :, None, :]

    lb_all = jnp.cumsum(jax.nn.softmax(hg_lb.astype(F32), axis=0), axis=0)
    lb_all = lb_all - lb_all[:1]
    cs = _rope_table(T, S)

    for layer in range(depth):
        j = layer // 2
        tab = tabs[layer]
        if layer % 2 == 0:
            p = _inproj(xx, tab[0], tab[1], _even_w_in(even_w_in[j]), tm, T)
            mix_a = _gdn(p, gdn_conv[j], (-jnp.exp(gdn_a_log[j].astype(F32))).reshape(-1),
                         gdn_dt_bias[j].astype(F32).reshape(-1), gdn_norm[j], T)
            q, k, v = _mlaproj(p, cs, mla_q_norm[j], mla_kv_norm[j], _mla_wq(mla_w_uq[j]),
                               mla_w_ukv[j].astype(BF16), tm)
            mix_b = _mla_attn(q, k, v, T, tm)
            w_out = even_w_out[j]
        else:
            p = _inproj(xx, tab[0], tab[1], odd_w_in[j].astype(BF16), tm, T)
            mix_a = _na(p, na_rpb[j], T)
            lb = lb_all[j]
            mix_b = _hgrn(p, jnp.log(lb), jnp.log1p(-lb), 1.0 - lb, hg_norm[j], T)
            w_out = odd_w_out[j]
        xx, aff = _outproj(mix_a, mix_b, w_out.astype(BF16), xx, tab[2], ln_g[layer, 0], ln_b[layer, 0],
                           tab[3], tab[4], moe_router[layer].T, tm, T)
        idx, gate, cap_c, cap = _topk(aff, T)
        xe = _gather(idx, xx, tab[3], tab[4], cap_c, cap)
        m_rows = B * cap
        ye = _experts(xe.reshape(N_EXPERTS, m_rows, D), moe_w_gate, moe_w_up, moe_w_down, layer,
                      next(t for t in (768, 576, m_rows) if m_rows % t == 0))
        xx = _combine(idx, gate, ye.reshape(N_EXPERTS, B, cap, D), xx, tab[5], ln_g[layer, 1], ln_b[layer, 1],
                      tm, T)
    return xx[:, T:, :]
```
